```python
import math
import jax, jax.numpy as jnp
from jax import lax
import numpy as np

D_MODEL = 2048
BATCH = 1
SEQ = 8192
DEPTH = 4

HEAD_DIM = 128
N_Q_HEADS = 12
N_KV_HEADS = 4
GQA_GROUP = N_Q_HEADS // N_KV_HEADS
WINDOW = 128
ATTN_BLOCK = 128
N_FOURIER_GROUPS = 4
FOURIER_GROUP_DIM = 128
Q_DIM = N_Q_HEADS * HEAD_DIM
KV_DIM = N_KV_HEADS * HEAD_DIM
FOURIER_DIM = N_FOURIER_GROUPS * FOURIER_GROUP_DIM
EVEN_IN_DIM = Q_DIM + 2 * KV_DIM + FOURIER_DIM
EVEN_MIX_DIM = Q_DIM + FOURIER_DIM

N_REL_BUCKETS = 32
REL_MAX_DISTANCE = 128

CONV_GROUPS = 8
CONV_GROUP_DIM = 128
CONV_DIM = CONV_GROUPS * CONV_GROUP_DIM
CONV_WIDTH = 3
POOL_WINDOWS = (2, 4, 8, 16)
POOL_GROUPS = len(POOL_WINDOWS)
POOL_GROUP_DIM = 256
POOL_DIM = POOL_GROUPS * POOL_GROUP_DIM
ODD_IN_DIM = 3 * CONV_DIM + POOL_DIM
ODD_MIX_DIM = CONV_DIM + POOL_DIM

N_EXPERTS = 16
EXPERT_HIDDEN = 1024
CAPACITY_FACTOR = 2

RMS_EPS = 1e-6
NEG_INF = -1e30

kernel_name = "hybrid_swa_fourier_conv_pool_ec_moe_encoder"


def rmsnorm(x, g):
    xf = x.astype(jnp.float32)
    y = xf * lax.rsqrt(jnp.mean(xf * xf, axis=-1, keepdims=True) + RMS_EPS)
    return (y * g.astype(jnp.float32)).astype(x.dtype)


def t5_bucket(rel):
    nb = N_REL_BUCKETS // 2
    max_exact = nb // 2
    ret = (rel > 0).astype(jnp.int32) * nb
    n = jnp.abs(rel)
    nf = jnp.maximum(n, 1).astype(jnp.float32)
    large = max_exact + (jnp.log(nf / max_exact) / math.log(REL_MAX_DISTANCE / max_exact)
                         * (nb - max_exact)).astype(jnp.int32)
    large = jnp.minimum(large, nb - 1)
    return ret + jnp.where(n < max_exact, n, large)


def window_rel_bias(table):
    i = jnp.arange(ATTN_BLOCK, dtype=jnp.int32)[:, None]
    j = jnp.arange(3 * ATTN_BLOCK, dtype=jnp.int32)[None, :]
    rel = (j - WINDOW) - i
    return jnp.transpose(table[t5_bucket(rel)], (2, 0, 1))


def windowed_gqa(q, k, v, sink, bias):
    bsz, seq = q.shape[0], q.shape[1]
    nb = seq // ATTN_BLOCK
    qb = q.reshape(bsz, nb, ATTN_BLOCK, N_KV_HEADS, GQA_GROUP, HEAD_DIM)
    pad = ((0, 0), (WINDOW, WINDOW), (0, 0), (0, 0))

    def band(t):
        tp = jnp.pad(t, pad).reshape(bsz, nb + 2, ATTN_BLOCK, N_KV_HEADS, HEAD_DIM)
        return jnp.concatenate([tp[:, 0:nb], tp[:, 1:nb + 1], tp[:, 2:nb + 2]], axis=2)

    kb, vb = band(k), band(v)
    s = jnp.einsum('bnqkgd,bnjkd->bnkgqj', qb, kb,
                   preferred_element_type=jnp.float32) * (HEAD_DIM ** -0.5)
    s = s + bias.astype(jnp.float32).reshape(N_KV_HEADS, GQA_GROUP, ATTN_BLOCK, 3 * ATTN_BLOCK)
    blk = jnp.arange(nb, dtype=jnp.int32)[:, None, None]
    qi = jnp.arange(ATTN_BLOCK, dtype=jnp.int32)[None, :, None]
    kj = jnp.arange(3 * ATTN_BLOCK, dtype=jnp.int32)[None, None, :]
    kpos = blk * ATTN_BLOCK - WINDOW + kj
    valid = (jnp.abs(kj - WINDOW - qi) <= WINDOW) & (kpos >= 0) & (kpos < seq)
    s = jnp.where(valid[None, :, None, None], s, NEG_INF)
    sk = sink.astype(jnp.float32).reshape(1, 1, N_KV_HEADS, GQA_GROUP, 1)
    m = jnp.maximum(jnp.max(s, axis=-1), sk)
    p = jnp.exp(s - m[..., None])
    denom = jnp.sum(p, axis=-1) + jnp.exp(sk - m)
    w = (p / denom[..., None]).astype(v.dtype)
    o = jnp.einsum('bnkgqj,bnjkd->bnqkgd', w, vb)
    return o.reshape(bsz, seq, Q_DIM)


def fourier_mix(f):
    spec = jnp.fft.fft2(f.astype(jnp.float32), axes=(1, 3), norm="ortho")
    return jnp.real(spec).astype(f.dtype)


def even_mixer(h, w_in, sink, w_out, bias):
    bsz, seq, _ = h.shape
    z = h @ w_in
    q = z[..., :Q_DIM].reshape(bsz, seq, N_Q_HEADS, HEAD_DIM)
    k = z[..., Q_DIM:Q_DIM + KV_DIM].reshape(bsz, seq, N_KV_HEADS, HEAD_DIM)
    v = z[..., Q_DIM + KV_DIM:Q_DIM + 2 * KV_DIM].reshape(bsz, seq, N_KV_HEADS, HEAD_DIM)
    f = z[..., Q_DIM + 2 * KV_DIM:].reshape(bsz, seq, N_FOURIER_GROUPS, FOURIER_GROUP_DIM)
    a_out = windowed_gqa(q, k, v, sink, bias)
    b_out = fourier_mix(f).reshape(bsz, seq, FOURIER_DIM)
    return jnp.concatenate([a_out, b_out], axis=-1) @ w_out


def multiscale_pool(xd):
    seq = xd.shape[1]
    xf = xd.astype(jnp.float32)
    cs = jnp.concatenate([jnp.zeros_like(xf[:, :1]), lax.cumsum(xf, axis=1)], axis=1)
    win = jnp.array(POOL_WINDOWS, dtype=jnp.int32)
    lo = win // 2
    hi = win - 1 - lo
    t = jnp.arange(seq, dtype=jnp.int32)[:, None]
    a = jnp.maximum(t - lo[None, :], 0)
    b = jnp.minimum(t + hi[None, :], seq - 1)
    gi = jnp.arange(POOL_GROUPS, dtype=jnp.int32)[None, :]
    total = cs[:, b + 1, gi] - cs[:, a, gi]
    count = (b - a + 1).astype(jnp.float32)[None, :, :, None]
    return (total / count - xf).astype(xd.dtype)


def odd_mixer(h, w_in, conv_w, pool_w, pool_scale, w_out):
    bsz, seq, _ = h.shape
    z = h @ w_in
    bg = z[..., :CONV_DIM]
    cg = z[..., CONV_DIM:2 * CONV_DIM]
    u = z[..., 2 * CONV_DIM:3 * CONV_DIM]
    xd = z[..., 3 * CONV_DIM:].reshape(bsz, seq, POOL_GROUPS, POOL_GROUP_DIM)
    conv = lax.conv_general_dilated(
        cg * u, conv_w[:, None, :], window_strides=(1,), padding=((1, 1),),
        dimension_numbers=('NWC', 'WIO', 'NWC'), feature_group_count=CONV_DIM)
    c_out = bg * conv
    pooled = multiscale_pool(xd)
    d_out = jnp.einsum('bsgc,gcd->bsgd', pooled, pool_w)
    d_out = (d_out * pool_scale.reshape(POOL_GROUPS, POOL_GROUP_DIM)).reshape(bsz, seq, POOL_DIM)
    return jnp.concatenate([c_out, d_out], axis=-1) @ w_out


def ec_moe(h, w_router, w_gate, w_up, w_down):
    bsz, seq, dm = h.shape
    cap = CAPACITY_FACTOR * seq // N_EXPERTS
    logits = jnp.einsum('bsd,de->bse', h, w_router, preferred_element_type=jnp.float32)
    aff = jax.nn.softmax(logits, axis=-1)
    gate, idx = lax.top_k(jnp.transpose(aff, (0, 2, 1)), cap)
    bidx = jnp.arange(bsz, dtype=jnp.int32)[:, None, None]
    xe = h[bidx, idx]
    hid = jax.nn.silu(jnp.einsum('becd,edh->bech', xe, w_gate)) * jnp.einsum('becd,edh->bech', xe, w_up)
    ye = jnp.einsum('bech,ehd->becd', hid, w_down) * gate.astype(h.dtype)[..., None]
    flat = (bidx * seq + idx).reshape(-1)
    out = jnp.zeros((bsz * seq, dm), ye.dtype).at[flat].add(ye.reshape(-1, dm))
    return out.reshape(bsz, seq, dm)


def setup_inputs(seed: int = 0) -> dict:
    key = jax.random.key(seed)
    ks = jax.random.split(key, 20)
    n_even = (DEPTH + 1) // 2
    n_odd = DEPTH // 2
    nrm = jax.random.normal
    f32 = jnp.float32
    return {
        "x": nrm(ks[0], (BATCH, SEQ, D_MODEL), f32),
        "rel_bias": 0.1 * nrm(ks[1], (N_REL_BUCKETS, N_Q_HEADS), f32),
        "norm_mix_g": 1.0 + 0.02 * nrm(ks[2], (DEPTH, D_MODEL), f32),
        "norm_ffn_g": 1.0 + 0.02 * nrm(ks[3], (DEPTH, D_MODEL), f32),
        "final_norm_g": 1.0 + 0.02 * nrm(ks[4], (D_MODEL,), f32),
        "ev_w_in": nrm(ks[5], (n_even, D_MODEL, EVEN_IN_DIM), f32) * D_MODEL ** -0.5,
        "ev_sink": 0.5 * nrm(ks[6], (n_even, N_Q_HEADS), f32),
        "ev_w_out": nrm(ks[7], (n_even, EVEN_MIX_DIM, D_MODEL), f32) * EVEN_MIX_DIM ** -0.5,
        "od_w_in": nrm(ks[8], (n_odd, D_MODEL, ODD_IN_DIM), f32) * D_MODEL ** -0.5,
        "od_conv_w": nrm(ks[9], (n_odd, CONV_WIDTH, CONV_DIM), f32) * CONV_WIDTH ** -0.5,
        "od_pool_w": nrm(ks[10], (n_odd, POOL_GROUPS, POOL_GROUP_DIM, POOL_GROUP_DIM), f32) * POOL_GROUP_DIM ** -0.5,
        "od_pool_scale": 1.0 + 0.1 * nrm(ks[11], (n_odd, POOL_DIM), f32),
        "od_w_out": nrm(ks[12], (n_odd, ODD_MIX_DIM, D_MODEL), f32) * ODD_MIX_DIM ** -0.5,
        "moe_w_router": nrm(ks[13], (DEPTH, D_MODEL, N_EXPERTS), f32) * D_MODEL ** -0.5,
        "moe_w_gate": nrm(ks[14], (DEPTH, N_EXPERTS, D_MODEL, EXPERT_HIDDEN), f32) * D_MODEL ** -0.5,
        "moe_w_up": nrm(ks[15], (DEPTH, N_EXPERTS, D_MODEL, EXPERT_HIDDEN), f32) * D_MODEL ** -0.5,
        "moe_w_down": nrm(ks[16], (DEPTH, N_EXPERTS, EXPERT_HIDDEN, D_MODEL), f32) * EXPERT_HIDDEN ** -0.5,
    }


def reference(x, rel_bias, norm_mix_g, norm_ffn_g, final_norm_g,
              ev_w_in, ev_sink, ev_w_out,
              od_w_in, od_conv_w, od_pool_w, od_pool_scale, od_w_out,
              moe_w_router, moe_w_gate, moe_w_up, moe_w_down):
    bias = window_rel_bias(rel_bias)
    for layer in range(DEPTH):
        h = rmsnorm(x, norm_mix_g[layer])
        if layer % 2 == 0:
            i = layer // 2
            x = x + even_mixer(h, ev_w_in[i], ev_sink[i], ev_w_out[i], bias)
        else:
            i = layer // 2
            x = x + odd_mixer(h, od_w_in[i], od_conv_w[i], od_pool_w[i], od_pool_scale[i], od_w_out[i])
        h = rmsnorm(x, norm_ffn_g[layer])
        x = x + ec_moe(h, moe_w_router[layer], moe_w_gate[layer], moe_w_up[layer], moe_w_down[layer])
    return rmsnorm(x, final_norm_g)
```

```python
import functools
import math

import numpy as np
import jax
import jax.numpy as jnp
from jax import lax
from jax.experimental import pallas as pl
from jax.experimental.pallas import tpu as pltpu

F32 = jnp.float32
BF16 = jnp.bfloat16
I32 = jnp.int32

HEAD_DIM = 128
N_Q_HEADS = 12
N_KV_HEADS = 4
GQA_GROUP = N_Q_HEADS // N_KV_HEADS
WINDOW = 128
ATTN_BLOCK = 128
N_FOURIER_GROUPS = 4
FOURIER_GROUP_DIM = 128
Q_DIM = N_Q_HEADS * HEAD_DIM
KV_DIM = N_KV_HEADS * HEAD_DIM
FOURIER_DIM = N_FOURIER_GROUPS * FOURIER_GROUP_DIM
N_REL_BUCKETS = 32
REL_MAX_DISTANCE = 128
CONV_DIM = 1024
POOL_WINDOWS = (2, 4, 8, 16)
POOL_GROUPS = len(POOL_WINDOWS)
POOL_GROUP_DIM = 256
POOL_DIM = POOL_GROUPS * POOL_GROUP_DIM
N_EXPERTS = 16
CAPACITY_FACTOR = 2
RMS_EPS = 1e-6
NEG_INF = -1e30

LANES = 128
SUBLANES = 8
BF16_ROWS = 16
VMEM_LIMIT = 56 * 1024 * 1024

HALO = BF16_ROWS
COMBINE_PIECE = 512


def _cparams(sem):
    return pltpu.CompilerParams(dimension_semantics=sem, vmem_limit_bytes=VMEM_LIMIT)


def _dot(a, b):
    return jnp.dot(a, b, preferred_element_type=F32)


def _dot_nt(a, b):
    return lax.dot_general(a, b, (((1,), (1,)), ((), ())), preferred_element_type=F32)


def _norm_matmul_kernel(x_ref, g_ref, w_ref, o_ref, h_ref):
    @pl.when(pl.program_id(1) == 0)
    def _():
        x = x_ref[...]
        ms = jnp.mean(x * x, axis=-1, keepdims=True)
        h_ref[...] = (x * lax.rsqrt(ms + RMS_EPS) * g_ref[...]).astype(BF16)

    o_ref[...] = _dot(h_ref[...], w_ref[...].astype(BF16)).astype(o_ref.dtype)


def norm_matmul(x, g, w, *, tm, tn):
    s, d = x.shape
    n = w.shape[1]
    return pl.pallas_call(
        _norm_matmul_kernel,
        grid=(s // tm, n // tn),
        in_specs=[
            pl.BlockSpec((tm, d), lambda i, j: (i, 0)),
            pl.BlockSpec((1, d), lambda i, j: (0, 0)),
            pl.BlockSpec((d, tn), lambda i, j: (0, j)),
        ],
        out_specs=pl.BlockSpec((tm, tn), lambda i, j: (i, j)),
        out_shape=jax.ShapeDtypeStruct((s, n), BF16),
        scratch_shapes=[pltpu.VMEM((tm, d), BF16)],
        compiler_params=_cparams(("parallel", "arbitrary")),
        name="norm_matmul",
    )(x, g.reshape(1, d), w)


def _out_proj_kernel(*refs, nparts):
    x_ref = refs[0]
    p_refs = refs[1:1 + nparts]
    w_refs = refs[1 + nparts:1 + 2 * nparts]
    o_ref = refs[1 + 2 * nparts]
    acc = x_ref[...]
    for p_ref, w_ref in zip(p_refs, w_refs):
        acc = acc + _dot(p_ref[...].astype(BF16), w_ref[...].astype(BF16))
    o_ref[...] = acc


def out_proj(x, parts, w, *, tm, tn):
    s, d = x.shape
    nparts = len(parts)
    in_specs = [pl.BlockSpec((tm, tn), lambda i, j: (i, j))]
    for p in parts:
        in_specs.append(pl.BlockSpec((tm, p.shape[1]), lambda i, j: (i, 0)))
    off = 0
    for p in parts:
        width = p.shape[1]
        assert off % width == 0
        rb = off // width
        in_specs.append(pl.BlockSpec((width, tn), lambda i, j, rb=rb: (rb, j)))
        off += width
    assert off == w.shape[0]
    return pl.pallas_call(
        functools.partial(_out_proj_kernel, nparts=nparts),
        grid=(s // tm, d // tn),
        in_specs=in_specs,
        out_specs=pl.BlockSpec((tm, tn), lambda i, j: (i, j)),
        out_shape=jax.ShapeDtypeStruct((s, d), F32),
        compiler_params=_cparams(("parallel", "arbitrary")),
        name="out_proj",
    )(x, *parts, *([w] * nparts))


def _t5_bucket(rel):
    nb = N_REL_BUCKETS // 2
    max_exact = nb // 2
    ret = (rel > 0).astype(jnp.int32) * nb
    n = jnp.abs(rel)
    nf = jnp.maximum(n, 1).astype(jnp.float32)
    large = max_exact + (jnp.log(nf / max_exact) / math.log(REL_MAX_DISTANCE / max_exact)
                         * (nb - max_exact)).astype(jnp.int32)
    large = jnp.minimum(large, nb - 1)
    return ret + jnp.where(n < max_exact, n, large)


def _bias_kernel(tab_ref, bucket_ref, o_ref):
    h = pl.program_id(0)
    bucket = bucket_ref[...]
    acc = jnp.zeros(bucket.shape, F32)
    for b in range(N_REL_BUCKETS):
        acc = jnp.where(bucket == b, tab_ref[b * N_Q_HEADS + h], acc)
    i = lax.broadcasted_iota(I32, bucket.shape, 0)
    j = lax.broadcasted_iota(I32, bucket.shape, 1)
    valid = jnp.abs(j - WINDOW - i) <= WINDOW
    o_ref[0] = jnp.where(valid, acc, NEG_INF)


def window_bias(rel_bias):
    i = jnp.arange(ATTN_BLOCK, dtype=jnp.int32)[:, None]
    j = jnp.arange(3 * ATTN_BLOCK, dtype=jnp.int32)[None, :]
    bucket = _t5_bucket((j - WINDOW) - i)
    return pl.pallas_call(
        _bias_kernel,
        grid_spec=pltpu.PrefetchScalarGridSpec(
            num_scalar_prefetch=1,
            grid=(N_Q_HEADS,),
            in_specs=[pl.BlockSpec((ATTN_BLOCK, 3 * ATTN_BLOCK), lambda h, t: (0, 0))],
            out_specs=pl.BlockSpec((1, ATTN_BLOCK, 3 * ATTN_BLOCK), lambda h, t: (h, 0, 0)),
        ),
        out_shape=jax.ShapeDtypeStruct((N_Q_HEADS, ATTN_BLOCK, 3 * ATTN_BLOCK), F32),
        compiler_params=_cparams(("arbitrary",)),
        name="window_bias",
    )(rel_bias.reshape(-1), bucket)


def _attn_kernel(sink_ref, q_ref, kp_ref, kc_ref, kn_ref, vp_ref, vc_ref, vn_ref, bias_ref, o_ref, *, nb):
    n = pl.program_id(0)
    blk = ATTN_BLOCK
    rows = GQA_GROUP * blk
    col = lax.broadcasted_iota(I32, (rows, 3 * blk), 1)
    edge_ok = ((col >= blk) | (n > 0)) & ((col < 2 * blk) | (n < nb - 1))
    row = lax.broadcasted_iota(I32, (rows, 1), 0)
    scale = HEAD_DIM ** -0.5
    for kv in range(N_KV_HEADS):
        cs = slice(kv * HEAD_DIM, (kv + 1) * HEAD_DIM)
        k = jnp.concatenate([kp_ref[:, cs], kc_ref[:, cs], kn_ref[:, cs]], axis=0)
        v = jnp.concatenate([vp_ref[:, cs], vc_ref[:, cs], vn_ref[:, cs]], axis=0)
        h0 = kv * GQA_GROUP
        q = jnp.concatenate(
            [q_ref[:, (h0 + g) * HEAD_DIM:(h0 + g + 1) * HEAD_DIM] for g in range(GQA_GROUP)], axis=0)
        s = _dot_nt(q, k) * scale + bias_ref[kv]
        s = jnp.where(edge_ok, s, NEG_INF)
        sk = jnp.full((rows, 1), sink_ref[h0 + GQA_GROUP - 1], F32)
        for g in range(GQA_GROUP - 2, -1, -1):
            sk = jnp.where(row < (g + 1) * blk, sink_ref[h0 + g], sk)
        m = jnp.maximum(jnp.max(s, axis=-1, keepdims=True), sk)
        p = jnp.exp(s - m)
        denom = jnp.sum(p, axis=-1, keepdims=True) + jnp.exp(sk - m)
        o = _dot(p.astype(BF16), v) / denom
        for g in range(GQA_GROUP):
            o_ref[:, (h0 + g) * HEAD_DIM:(h0 + g + 1) * HEAD_DIM] = o[g * blk:(g + 1) * blk].astype(o_ref.dtype)


def windowed_attention(z, sink, bias):
    s = z.shape[0]
    blk = ATTN_BLOCK
    nb = s // blk
    kcol = Q_DIM // KV_DIM
    vcol = kcol + 1

    def prev(n, t):
        return jnp.maximum(n - 1, 0)

    def nxt(n, t):
        return jnp.minimum(n + 1, nb - 1)

    in_specs = [
        pl.BlockSpec((blk, Q_DIM), lambda n, t: (n, 0)),
        pl.BlockSpec((blk, KV_DIM), lambda n, t: (prev(n, t), kcol)),
        pl.BlockSpec((blk, KV_DIM), lambda n, t: (n, kcol)),
        pl.BlockSpec((blk, KV_DIM), lambda n, t: (nxt(n, t), kcol)),
        pl.BlockSpec((blk, KV_DIM), lambda n, t: (prev(n, t), vcol)),
        pl.BlockSpec((blk, KV_DIM), lambda n, t: (n, vcol)),
        pl.BlockSpec((blk, KV_DIM), lambda n, t: (nxt(n, t), vcol)),
        pl.BlockSpec((N_KV_HEADS, GQA_GROUP * blk, 3 * blk), lambda n, t: (0, 0, 0)),
    ]
    return pl.pallas_call(
        functools.partial(_attn_kernel, nb=nb),
        grid_spec=pltpu.PrefetchScalarGridSpec(
            num_scalar_prefetch=1,
            grid=(nb,),
            in_specs=in_specs,
            out_specs=pl.BlockSpec((blk, Q_DIM), lambda n, t: (n, 0)),
        ),
        out_shape=jax.ShapeDtypeStruct((s, Q_DIM), BF16),
        compiler_params=_cparams(("arbitrary",)),
        name="windowed_attention",
    )(sink, z, z, z, z, z, z, z, bias.reshape(N_KV_HEADS, GQA_GROUP * blk, 3 * blk))


def _fourier_tables(s):
    n2 = LANES
    n1 = s // n2
    k1 = np.arange(n1)[None, :, None]
    s1 = np.arange(n1)[None, None, :]
    s2 = np.arange(n2)[:, None, None]
    ang = 2.0 * np.pi * ((k1 * (s2 + n2 * s1)) % s) / s
    ma = np.concatenate([np.cos(ang), -np.sin(ang)], axis=1)
    a = 2.0 * np.pi * ((np.arange(n2)[:, None] * np.arange(n2)[None, :]) % n2) / n2
    c, sn = np.cos(a), np.sin(a)
    mc = np.block([[c, sn], [-sn, c]])
    scale = 1.0 / math.sqrt(s * FOURIER_GROUP_DIM)
    md = np.concatenate([c, sn], axis=0) * scale
    return (jnp.asarray(ma, BF16), jnp.asarray(mc, BF16), jnp.asarray(md, BF16))


def _fourier_kernel(x_ref, ma_ref, mc_ref, md_ref, o_ref, xf, yr, yi, *, n1):
    n2 = LANES
    xf[...] = x_ref[...].astype(F32)

    def stage_a(s2, carry):
        xs = xf[:, s2, :].astype(BF16)
        y = _dot(ma_ref[s2], xs)
        yr[:, s2, :] = y[:n1]
        yi[:, s2, :] = y[n1:]
        return carry

    lax.fori_loop(0, n2, stage_a, 0)

    def stage_c(k1, carry):
        y = jnp.concatenate([yr[k1], yi[k1]], axis=0).astype(BF16)
        xc = _dot(mc_ref[...], y)
        xx = jnp.concatenate([xc[:n2], xc[n2:]], axis=1).astype(BF16)
        o_ref[:, k1, :] = _dot(xx, md_ref[...])
        return carry

    lax.fori_loop(0, n1, stage_c, 0)


def fourier_mix(z, col0):
    s, w = z.shape
    n2 = LANES
    n1 = s // n2
    c = FOURIER_GROUP_DIM
    ma, mc, md = _fourier_tables(s)
    cb0 = col0 // c
    out = pl.pallas_call(
        functools.partial(_fourier_kernel, n1=n1),
        grid=(N_FOURIER_GROUPS,),
        in_specs=[
            pl.BlockSpec((n1, n2, c), lambda g: (0, 0, cb0 + g)),
            pl.BlockSpec((n2, 2 * n1, n1), lambda g: (0, 0, 0)),
            pl.BlockSpec((2 * n2, 2 * n2), lambda g: (0, 0)),
            pl.BlockSpec((2 * c, c), lambda g: (0, 0)),
        ],
        out_specs=pl.BlockSpec((n2, n1, c), lambda g: (0, 0, g)),
        out_shape=jax.ShapeDtypeStruct((n2, n1, FOURIER_DIM), F32),
        scratch_shapes=[pltpu.VMEM((n1, n2, c), F32)] * 3,
        compiler_params=_cparams(("arbitrary",)),
        name="fourier_mix",
    )(z.reshape(n1, n2, w), ma, mc, md)
    return out.reshape(s, FOURIER_DIM)


def _convpool_kernel(zp_ref, zc_ref, zn_ref, cw_ref, pw_ref, ps_ref, o_ref, *, seq, tt):
    i = pl.program_id(0)
    ext_rows = tt + 2 * HALO
    grow = i * tt - HALO + lax.broadcasted_iota(I32, (ext_rows, 1), 0)
    row_ok = (grow >= 0) & (grow < seq)

    def ext(c0, c1):
        e = jnp.concatenate([zp_ref[:, c0:c1], zc_ref[:, c0:c1], zn_ref[:, c0:c1]], axis=0).astype(F32)
        return jnp.where(row_ok, e, 0.0)

    def shifted(e, d):
        return e[HALO + d:HALO + d + tt]

    prod = ext(CONV_DIM, 2 * CONV_DIM) * ext(2 * CONV_DIM, 3 * CONV_DIM)
    cw = cw_ref[...]
    conv = shifted(prod, -1) * cw[0:1] + shifted(prod, 0) * cw[1:2] + shifted(prod, 1) * cw[2:3]
    o_ref[:, :CONV_DIM] = (zc_ref[:, :CONV_DIM].astype(F32) * conv).astype(o_ref.dtype)

    t = grow[HALO:HALO + tt]
    for g, win in enumerate(POOL_WINDOWS):
        lo = win // 2
        hi = win - 1 - lo
        c0 = 3 * CONV_DIM + g * POOL_GROUP_DIM
        e = ext(c0, c0 + POOL_GROUP_DIM)
        total = shifted(e, -lo)
        for d in range(-lo + 1, hi + 1):
            total = total + shifted(e, d)
        count = (jnp.minimum(t + hi, seq - 1) - jnp.maximum(t - lo, 0) + 1).astype(F32)
        pooled = total / count - shifted(e, 0)
        dg = _dot(pooled.astype(BF16), pw_ref[g].astype(BF16)) * ps_ref[:, g * POOL_GROUP_DIM:(g + 1) * POOL_GROUP_DIM]
        o_ref[:, CONV_DIM + g * POOL_GROUP_DIM:CONV_DIM + (g + 1) * POOL_GROUP_DIM] = dg.astype(o_ref.dtype)


def conv_pool(z, conv_w, pool_w, pool_scale, *, tt):
    s, w = z.shape
    nh = tt // HALO
    last_h = s // HALO - 1
    return pl.pallas_call(
        functools.partial(_convpool_kernel, seq=s, tt=tt),
        grid=(s // tt,),
        in_specs=[
            pl.BlockSpec((HALO, w), lambda i: (jnp.maximum(i * nh - 1, 0), 0)),
            pl.BlockSpec((tt, w), lambda i: (i, 0)),
            pl.BlockSpec((HALO, w), lambda i: (jnp.minimum((i + 1) * nh, last_h), 0)),
            pl.BlockSpec((3, CONV_DIM), lambda i: (0, 0)),
            pl.BlockSpec((POOL_GROUPS, POOL_GROUP_DIM, POOL_GROUP_DIM), lambda i: (0, 0, 0)),
            pl.BlockSpec((1, POOL_DIM), lambda i: (0, 0)),
        ],
        out_specs=pl.BlockSpec((tt, CONV_DIM + POOL_DIM), lambda i: (i, 0)),
        out_shape=jax.ShapeDtypeStruct((s, CONV_DIM + POOL_DIM), BF16),
        compiler_params=_cparams(("arbitrary",)),
        name="conv_pool",
    )(z, z, z, conv_w, pool_w, pool_scale.reshape(1, POOL_DIM))


def _router_kernel(x_ref, g_ref, wr_ref, h_ref, aff_ref):
    x = x_ref[...]
    ms = jnp.mean(x * x, axis=-1, keepdims=True)
    h = x * lax.rsqrt(ms + RMS_EPS) * g_ref[...]
    h_ref[...] = h
    logits = lax.dot_general(wr_ref[...], h, (((1,), (1,)), ((), ())),
                             preferred_element_type=F32, precision=lax.Precision.HIGHEST)
    m = jnp.max(logits, axis=0, keepdims=True)
    p = jnp.exp(logits - m)
    aff_ref[...] = p / jnp.sum(p, axis=0, keepdims=True)


def router(x, g, w_router, *, tm):
    s, d = x.shape
    e = w_router.shape[1]
    return pl.pallas_call(
        _router_kernel,
        grid=(s // tm,),
        in_specs=[
            pl.BlockSpec((tm, d), lambda i: (i, 0)),
            pl.BlockSpec((1, d), lambda i: (0, 0)),
            pl.BlockSpec((e, d), lambda i: (0, 0)),
        ],
        out_specs=[pl.BlockSpec((tm, d), lambda i: (i, 0)), pl.BlockSpec((e, tm), lambda i: (0, i))],
        out_shape=[jax.ShapeDtypeStruct((s, d), F32), jax.ShapeDtypeStruct((e, s), F32)],
        compiler_params=_cparams(("parallel",)),
        name="router",
    )(x, g.reshape(1, d), w_router.T)


def _excl_cumsum_lanes(m, upper):
    r, s = m.shape
    off = jnp.zeros((r, 1), F32)
    pieces = []
    for c in range(s // LANES):
        mc = m[:, c * LANES:(c + 1) * LANES]
        pieces.append(_dot(mc.astype(BF16), upper) + off)
        off = off + jnp.sum(mc, axis=1, keepdims=True)
    return jnp.concatenate(pieces, axis=1)


def _select_kernel(aff_ref, upper_ref, lower_ref, pos_ref, sel_ref, cum_ref, v_ref, *, cap):
    a = aff_ref[...]
    e, s = a.shape
    bits = pltpu.bitcast(a, I32)
    thr = jnp.zeros((e, 1), I32)
    for bit in range(30, -1, -1):
        cand = thr | (1 << bit)
        cnt = jnp.sum((bits >= cand).astype(F32), axis=1, keepdims=True)
        thr = jnp.where(cnt >= cap, cand, thr)
    gt = bits > thr
    eq = (bits == thr).astype(F32)
    need = cap - jnp.sum(gt.astype(F32), axis=1, keepdims=True)
    upper = upper_ref[...]
    eq_rank = _excl_cumsum_lanes(eq, upper)
    sel = jnp.where(gt | ((eq > 0.5) & (eq_rank < need)), 1.0, 0.0)
    pos = _excl_cumsum_lanes(sel, upper)
    pos_ref[...] = pos
    sel_ref[...] = sel
    count = jnp.sum(sel, axis=0, keepdims=True)
    cum = _excl_cumsum_lanes(jnp.broadcast_to(count, (SUBLANES, s)), upper)[0:1]
    cum_ref[...] = jnp.concatenate([cum, count, jnp.zeros((SUBLANES - 2, s), F32)], axis=0)
    before = _dot(lower_ref[...], sel.astype(BF16))
    dest = cum + before
    tok = lax.broadcasted_iota(I32, (1, s), 1).astype(F32)
    tok_hi = jnp.floor(tok * (1.0 / LANES))
    tok_lo = tok - tok_hi * LANES
    d_hi = jnp.floor(dest * (1.0 / LANES))
    d_lo = dest - d_hi * LANES
    g1 = a.astype(BF16).astype(F32)
    g2 = (a - g1).astype(BF16).astype(F32)
    g3 = a - g1 - g2
    zero = jnp.zeros((1, s), F32)
    for x in range(e):
        v_ref[x] = jnp.concatenate(
            [tok_hi, tok_lo, d_hi[x:x + 1], d_lo[x:x + 1], g1[x:x + 1], g2[x:x + 1], g3[x:x + 1], zero], axis=0)


def select_tokens(aff, cap):
    e, s = aff.shape
    upper = jnp.asarray(np.triu(np.ones((LANES, LANES), np.float32), k=1), BF16)
    lower = jnp.asarray(np.tril(np.ones((e, e), np.float32), k=-1), BF16)
    full2 = lambda shp: pl.BlockSpec(shp, lambda i: (0,) * len(shp))
    return pl.pallas_call(
        functools.partial(_select_kernel, cap=cap),
        grid=(1,),
        in_specs=[full2((e, s)), full2((LANES, LANES)), full2((e, e))],
        out_specs=[full2((e, s)), full2((e, s)), full2((SUBLANES, s)), full2((e, SUBLANES, s))],
        out_shape=[
            jax.ShapeDtypeStruct((e, s), F32),
            jax.ShapeDtypeStruct((e, s), F32),
            jax.ShapeDtypeStruct((SUBLANES, s), F32),
            jax.ShapeDtypeStruct((e, SUBLANES, s), F32),
        ],
        compiler_params=_cparams(("arbitrary",)),
        name="select_tokens",
    )(aff, upper, lower)


def _lists_kernel(off_ref, pos_ref, sel_ref, v_ref, o_ref, acc, *, nchunk, ntile):
    e = pl.program_id(0)
    acc[...] = jnp.zeros(acc.shape, F32)
    slot = lax.broadcasted_iota(I32, (LANES, LANES), 0).astype(F32)

    def body(c, carry):
        j0 = off_ref[e * nchunk + c] // LANES
        st = pl.multiple_of(c * LANES, LANES)
        p = pos_ref[0, :, pl.ds(st, LANES)]
        chosen = sel_ref[0, :, pl.ds(st, LANES)] > 0.5
        v = v_ref[0, :, pl.ds(st, LANES)].astype(BF16)
        for d in range(2):
            base = ((j0 + d) * LANES).astype(F32)
            onehot = jnp.where(((p - base) == slot) & chosen, 1.0, 0.0).astype(BF16)
            acc[j0 + d] += _dot_nt(v, onehot)
        return carry

    lax.fori_loop(0, nchunk, body, 0)
    for j in range(ntile):
        o_ref[0, :, j * LANES:(j + 1) * LANES] = acc[j]


def build_lists(pos, sel, vals, cap):
    e, s = pos.shape
    nchunk = s // LANES
    ntile = cap // LANES
    off = pos[:, ::LANES].astype(I32).reshape(-1)
    row = lambda nrow: pl.BlockSpec((1, nrow, s), lambda x, t: (x, 0, 0))
    lists = pl.pallas_call(
        functools.partial(_lists_kernel, nchunk=nchunk, ntile=ntile),
        grid_spec=pltpu.PrefetchScalarGridSpec(
            num_scalar_prefetch=1,
            grid=(e,),
            in_specs=[row(1), row(1), row(SUBLANES)],
            out_specs=pl.BlockSpec((1, SUBLANES, cap), lambda x, t: (x, 0, 0)),
            scratch_shapes=[pltpu.VMEM((ntile + 1, SUBLANES, LANES), F32)],
        ),
        out_shape=jax.ShapeDtypeStruct((e, SUBLANES, cap), F32),
        compiler_params=_cparams(("arbitrary",)),
        name="build_lists",
    )(off, pos.reshape(e, 1, s), sel.reshape(e, 1, s), vals)
    idx = (lists[:, 0] * LANES + lists[:, 1]).astype(I32)
    dest = (lists[:, 2] * LANES + lists[:, 3]).astype(I32)
    gate = (lists[:, 4] + lists[:, 5] + lists[:, 6])
    return idx, dest, gate


def _ffn_kernel(idx_ref, dest_ref, h_hbm, gate_ref, wg_ref, wu_ref, wd_ref, r_hbm,
                xbuf, xb, acc, sem_g, sem_s, *, cap, nh):
    e = pl.program_id(0)
    hs = pl.program_id(1)
    base = e * cap

    @pl.when(hs == 0)
    def _():
        def issue(r, carry):
            t = idx_ref[base + r]
            pltpu.make_async_copy(h_hbm.at[pl.ds(t, 1)], xbuf.at[pl.ds(r, 1)], sem_g).start()
            return carry

        lax.fori_loop(0, cap, issue, 0, unroll=8)

        def drain(r, carry):
            pltpu.make_async_copy(h_hbm.at[pl.ds(0, 1)], xbuf.at[pl.ds(r, 1)], sem_g).wait()
            return carry

        lax.fori_loop(0, cap, drain, 0, unroll=8)
        xb[...] = xbuf[...].astype(BF16)

    x = xb[...]
    gp = _dot(x, wg_ref[0].astype(BF16))
    up = _dot(x, wu_ref[0].astype(BF16))
    hid = (gp * (1.0 / (1.0 + jnp.exp(-gp))) * up).astype(BF16)
    contrib = _dot(hid, wd_ref[0].astype(BF16))

    @pl.when(hs == 0)
    def _():
        acc[...] = contrib

    @pl.when(hs > 0)
    def _():
        acc[...] += contrib

    @pl.when(hs == nh - 1)
    def _():
        xbuf[...] = acc[...] * gate_ref[0]

        def issue(r, carry):
            d = dest_ref[base + r]
            pltpu.make_async_copy(xbuf.at[pl.ds(r, 1)], r_hbm.at[pl.ds(d, 1)], sem_s).start()
            return carry

        lax.fori_loop(0, cap, issue, 0, unroll=8)

        def drain(r, carry):
            pltpu.make_async_copy(xbuf.at[pl.ds(r, 1)], r_hbm.at[pl.ds(0, 1)], sem_s).wait()
            return carry

        lax.fori_loop(0, cap, drain, 0, unroll=8)


def expert_ffn(h, idx, dest, gate, w_gate, w_up, w_down, *, th):
    s, d = h.shape
    e, cap = idx.shape
    hidden = w_gate.shape[2]
    nh = hidden // th
    return pl.pallas_call(
        functools.partial(_ffn_kernel, cap=cap, nh=nh),
        grid_spec=pltpu.PrefetchScalarGridSpec(
            num_scalar_prefetch=2,
            grid=(e, nh),
            in_specs=[
                pl.BlockSpec(memory_space=pl.ANY),
                pl.BlockSpec((1, cap, 1), lambda x, j, a, b: (x, 0, 0)),
                pl.BlockSpec((1, d, th), lambda x, j, a, b: (x, 0, j)),
                pl.BlockSpec((1, d, th), lambda x, j, a, b: (x, 0, j)),
                pl.BlockSpec((1, th, d), lambda x, j, a, b: (x, j, 0)),
            ],
            out_specs=pl.BlockSpec(memory_space=pl.ANY),
            scratch_shapes=[
                pltpu.VMEM((cap, d), F32),
                pltpu.VMEM((cap, d), BF16),
                pltpu.VMEM((cap, d), F32),
                pltpu.SemaphoreType.DMA(()),
                pltpu.SemaphoreType.DMA(()),
            ],
        ),
        out_shape=jax.ShapeDtypeStruct((e * cap, d), F32),
        compiler_params=_cparams(("arbitrary", "arbitrary")),
        name="expert_ffn",
    )(idx.reshape(-1), dest.reshape(-1), h, gate.reshape(e, cap, 1), w_gate, w_up, w_down)


def _combine_kernel(rs_ref, x_ref, lo_ref, hi_ref, g_ref, r_hbm, o_ref, buf, acc, sem, *, nrows, final):
    c = pl.program_id(0)
    piece = COMBINE_PIECE
    start = (rs_ref[c] // SUBLANES) * SUBLANES
    npieces = (rs_ref[c + 1] - start + piece - 1) // piece
    acc[...] = jnp.zeros(acc.shape, F32)
    lo = lo_ref[...]
    hi = hi_ref[...]

    def body(i, carry):
        want = start + i * piece
        st = pl.multiple_of(jnp.minimum(want, nrows - piece), SUBLANES)
        cp = pltpu.make_async_copy(r_hbm.at[pl.ds(st, piece)], buf, sem)
        cp.start()
        cp.wait()
        p = st + lax.broadcasted_iota(I32, (1, piece), 1)
        member = jnp.where((p >= lo) & (p < hi) & (p >= want), 1.0, 0.0).astype(BF16)
        b = buf[...]
        b_hi = b.astype(BF16)
        b_lo = (b - b_hi.astype(F32)).astype(BF16)
        acc[...] += _dot(member, b_hi) + _dot(member, b_lo)
        return carry

    lax.fori_loop(0, npieces, body, 0)
    y = x_ref[...] + acc[...]
    if final:
        ms = jnp.mean(y * y, axis=-1, keepdims=True)
        y = y * lax.rsqrt(ms + RMS_EPS) * g_ref[...]
    o_ref[...] = y


def combine(x, rows, cum, count, final_g, *, tc, final):
    s, d = x.shape
    nrows = rows.shape[0]
    cum_i = cum.astype(I32)
    lo = cum_i.reshape(s, 1)
    hi = (cum_i + count.astype(I32)).reshape(s, 1)
    rs = jnp.concatenate([cum_i[::tc], jnp.full((1,), nrows, I32)])
    return pl.pallas_call(
        functools.partial(_combine_kernel, nrows=nrows, final=final),
        grid_spec=pltpu.PrefetchScalarGridSpec(
            num_scalar_prefetch=1,
            grid=(s // tc,),
            in_specs=[
                pl.BlockSpec((tc, d), lambda c, t: (c, 0)),
                pl.BlockSpec((tc, 1), lambda c, t: (c, 0)),
                pl.BlockSpec((tc, 1), lambda c, t: (c, 0)),
                pl.BlockSpec((1, d), lambda c, t: (0, 0)),
                pl.BlockSpec(memory_space=pl.ANY),
            ],
            out_specs=pl.BlockSpec((tc, d), lambda c, t: (c, 0)),
            scratch_shapes=[
                pltpu.VMEM((COMBINE_PIECE, d), F32),
                pltpu.VMEM((tc, d), F32),
                pltpu.SemaphoreType.DMA(()),
            ],
        ),
        out_shape=jax.ShapeDtypeStruct((s, d), F32),
        compiler_params=_cparams(("arbitrary",)),
        name="combine",
    )(rs, x, lo, hi, final_g.reshape(1, d), rows)


def ec_moe_block(x, g, w_router, w_gate, w_up, w_down, final_g, *, final, tm, th, tc):
    s = x.shape[0]
    cap = CAPACITY_FACTOR * s // N_EXPERTS
    h, aff = router(x, g, w_router, tm=tm)
    pos, sel, meta, vals = select_tokens(aff, cap)
    idx, dest, gate = build_lists(pos, sel, vals, cap)
    rows = expert_ffn(h, idx, dest, gate, w_gate, w_up, w_down, th=th)
    return combine(x, rows, meta[0], meta[1], final_g, tc=tc, final=final)


def _tiles(s):
    return dict(tm=min(1024, s), tn=512, tr=min(512, s), tt=min(512, s), th=256, tc=min(256, s))


def kernel(x, rel_bias, norm_mix_g, norm_ffn_g, final_norm_g, ev_w_in, ev_sink, ev_w_out, od_w_in, od_conv_w,
           od_pool_w, od_pool_scale, od_w_out, moe_w_router, moe_w_gate, moe_w_up, moe_w_down):
    b, s, d = x.shape
    assert b == 1
    t = _tiles(s)
    depth = norm_mix_g.shape[0]
    bias = window_bias(rel_bias)
    xs = x.reshape(s, d)
    for layer in range(depth):
        i = layer // 2
        if layer % 2 == 0:
            z = norm_matmul(xs, norm_mix_g[layer], ev_w_in[i], tm=t["tm"], tn=t["tn"])
            a_out = windowed_attention(z, ev_sink[i], bias)
            b_out = fourier_mix(z, Q_DIM + 2 * KV_DIM)
            xs = out_proj(xs, [a_out, b_out], ev_w_out[i], tm=t["tm"], tn=t["tn"])
        else:
            z = norm_matmul(xs, norm_mix_g[layer], od_w_in[i], tm=t["tm"], tn=t["tn"])
            mix = conv_pool(z, od_conv_w[i], od_pool_w[i], od_pool_scale[i], tt=t["tt"])
            xs = out_proj(xs, [mix], od_w_out[i], tm=t["tm"], tn=t["tn"])
        xs = ec_moe_block(xs, norm_ffn_g[layer], moe_w_router[layer], moe_w_gate[layer], moe_w_up[layer],
                          moe_w_down[layer], final_norm_g, final=(layer == depth - 1),
                          tm=t["tr"], th=t["th"], tc=t["tc"])
    return xs.reshape(b, s, d)
```

```python
import functools
import math

import numpy as np
import jax
import jax.numpy as jnp
from jax import lax
from jax.experimental import pallas as pl
from jax.experimental.pallas import tpu as pltpu

F32 = jnp.float32
BF16 = jnp.bfloat16
I32 = jnp.int32

HEAD_DIM = 128
N_Q_HEADS = 12
N_KV_HEADS = 4
GQA_GROUP = N_Q_HEADS // N_KV_HEADS
WINDOW = 128
ATTN_BLOCK = 128
N_FOURIER_GROUPS = 4
FOURIER_GROUP_DIM = 128
Q_DIM = N_Q_HEADS * HEAD_DIM
KV_DIM = N_KV_HEADS * HEAD_DIM
FOURIER_DIM = N_FOURIER_GROUPS * FOURIER_GROUP_DIM
N_REL_BUCKETS = 32
REL_MAX_DISTANCE = 128
CONV_DIM = 1024
POOL_WINDOWS = (2, 4, 8, 16)
POOL_GROUPS = len(POOL_WINDOWS)
POOL_GROUP_DIM = 256
POOL_DIM = POOL_GROUPS * POOL_GROUP_DIM
N_EXPERTS = 16
CAPACITY_FACTOR = 2
RMS_EPS = 1e-6
NEG_INF = -1e30

LANES = 128
SUBLANES = 8
BF16_ROWS = 16
VMEM_LIMIT = 56 * 1024 * 1024

HALO = BF16_ROWS
COMBINE_WINDOW = 64


def _cparams(sem):
    return pltpu.CompilerParams(dimension_semantics=sem, vmem_limit_bytes=VMEM_LIMIT)


def _dot(a, b):
    return jnp.dot(a, b, preferred_element_type=F32)


def _dot_nt(a, b):
    return lax.dot_general(a, b, (((1,), (1,)), ((), ())), preferred_element_type=F32)


def _norm_matmul_kernel(x_ref, g_ref, w_ref, o_ref, h_ref):
    @pl.when(pl.program_id(1) == 0)
    def _():
        x = x_ref[...]
        ms = jnp.mean(x * x, axis=-1, keepdims=True)
        h_ref[...] = (x * lax.rsqrt(ms + RMS_EPS) * g_ref[...]).astype(BF16)

    o_ref[...] = _dot(h_ref[...], w_ref[...].astype(BF16)).astype(o_ref.dtype)


def norm_matmul(x, g, w, li, *, tm, tn):
    s, d = x.shape
    n = w.shape[2]
    return pl.pallas_call(
        _norm_matmul_kernel,
        grid=(s // tm, n // tn),
        in_specs=[
            pl.BlockSpec((tm, d), lambda i, j: (i, 0)),
            pl.BlockSpec((1, d), lambda i, j: (0, 0)),
            pl.BlockSpec((None, d, tn), lambda i, j: (li, 0, j)),
        ],
        out_specs=pl.BlockSpec((tm, tn), lambda i, j: (i, j)),
        out_shape=jax.ShapeDtypeStruct((s, n), BF16),
        scratch_shapes=[pltpu.VMEM((tm, d), BF16)],
        compiler_params=_cparams(("parallel", "arbitrary")),
        name="norm_matmul",
    )(x, g.reshape(1, d), w)


def _out_proj_kernel(*refs, nparts):
    x_ref = refs[0]
    p_refs = refs[1:1 + nparts]
    w_refs = refs[1 + nparts:1 + 2 * nparts]
    o_ref = refs[1 + 2 * nparts]
    acc = x_ref[...]
    for p_ref, w_ref in zip(p_refs, w_refs):
        acc = acc + _dot(p_ref[...].astype(BF16), w_ref[...].astype(BF16))
    o_ref[...] = acc


def out_proj(x, parts, w, li, *, tm, tn):
    s, d = x.shape
    nparts = len(parts)
    in_specs = [pl.BlockSpec((tm, tn), lambda i, j: (i, j))]
    for p in parts:
        in_specs.append(pl.BlockSpec((tm, p.shape[1]), lambda i, j: (i, 0)))
    off = 0
    for p in parts:
        width = p.shape[1]
        assert off % width == 0
        rb = off // width
        in_specs.append(pl.BlockSpec((None, width, tn), lambda i, j, rb=rb: (li, rb, j)))
        off += width
    assert off == w.shape[1]
    return pl.pallas_call(
        functools.partial(_out_proj_kernel, nparts=nparts),
        grid=(s // tm, d // tn),
        in_specs=in_specs,
        out_specs=pl.BlockSpec((tm, tn), lambda i, j: (i, j)),
        out_shape=jax.ShapeDtypeStruct((s, d), F32),
        compiler_params=_cparams(("parallel", "arbitrary")),
        name="out_proj",
    )(x, *parts, *([w] * nparts))


def _t5_bucket(rel):
    nb = N_REL_BUCKETS // 2
    max_exact = nb // 2
    ret = (rel > 0).astype(jnp.int32) * nb
    n = jnp.abs(rel)
    nf = jnp.maximum(n, 1).astype(jnp.float32)
    large = max_exact + (jnp.log(nf / max_exact) / math.log(REL_MAX_DISTANCE / max_exact)
                         * (nb - max_exact)).astype(jnp.int32)
    large = jnp.minimum(large, nb - 1)
    return ret + jnp.where(n < max_exact, n, large)


def _bias_kernel(tab_ref, bucket_ref, o_ref):
    h = pl.program_id(0)
    bucket = bucket_ref[...]
    acc = jnp.zeros(bucket.shape, F32)
    for b in range(N_REL_BUCKETS):
        acc = jnp.where(bucket == b, tab_ref[b * N_Q_HEADS + h], acc)
    i = lax.broadcasted_iota(I32, bucket.shape, 0)
    j = lax.broadcasted_iota(I32, bucket.shape, 1)
    valid = jnp.abs(j - WINDOW - i) <= WINDOW
    o_ref[0] = jnp.where(valid, acc, NEG_INF)


def window_bias(rel_bias):
    i = jnp.arange(ATTN_BLOCK, dtype=jnp.int32)[:, None]
    j = jnp.arange(3 * ATTN_BLOCK, dtype=jnp.int32)[None, :]
    bucket = _t5_bucket((j - WINDOW) - i)
    return pl.pallas_call(
        _bias_kernel,
        grid_spec=pltpu.PrefetchScalarGridSpec(
            num_scalar_prefetch=1,
            grid=(N_Q_HEADS,),
            in_specs=[pl.BlockSpec((ATTN_BLOCK, 3 * ATTN_BLOCK), lambda h, t: (0, 0))],
            out_specs=pl.BlockSpec((1, ATTN_BLOCK, 3 * ATTN_BLOCK), lambda h, t: (h, 0, 0)),
        ),
        out_shape=jax.ShapeDtypeStruct((N_Q_HEADS, ATTN_BLOCK, 3 * ATTN_BLOCK), F32),
        compiler_params=_cparams(("arbitrary",)),
        name="window_bias",
    )(rel_bias.reshape(-1), bucket)


def _attn_kernel(sink_ref, q_ref, kp_ref, kc_ref, kn_ref, vp_ref, vc_ref, vn_ref, bias_ref, o_ref, *, nb):
    n = pl.program_id(0)
    blk = ATTN_BLOCK
    rows = GQA_GROUP * blk
    col = lax.broadcasted_iota(I32, (rows, 3 * blk), 1)
    edge_ok = ((col >= blk) | (n > 0)) & ((col < 2 * blk) | (n < nb - 1))
    row = lax.broadcasted_iota(I32, (rows, 1), 0)
    scale = HEAD_DIM ** -0.5
    for kv in range(N_KV_HEADS):
        cs = slice(kv * HEAD_DIM, (kv + 1) * HEAD_DIM)
        k = jnp.concatenate([kp_ref[:, cs], kc_ref[:, cs], kn_ref[:, cs]], axis=0)
        v = jnp.concatenate([vp_ref[:, cs], vc_ref[:, cs], vn_ref[:, cs]], axis=0)
        h0 = kv * GQA_GROUP
        q = jnp.concatenate(
            [q_ref[:, (h0 + g) * HEAD_DIM:(h0 + g + 1) * HEAD_DIM] for g in range(GQA_GROUP)], axis=0)
        s = _dot_nt(q, k) * scale + bias_ref[kv]
        s = jnp.where(edge_ok, s, NEG_INF)
        sk = jnp.full((rows, 1), sink_ref[h0 + GQA_GROUP - 1], F32)
        for g in range(GQA_GROUP - 2, -1, -1):
            sk = jnp.where(row < (g + 1) * blk, sink_ref[h0 + g], sk)
        m = jnp.maximum(jnp.max(s, axis=-1, keepdims=True), sk)
        p = jnp.exp(s - m)
        denom = jnp.sum(p, axis=-1, keepdims=True) + jnp.exp(sk - m)
        o = _dot(p.astype(BF16), v) / denom
        for g in range(GQA_GROUP):
            o_ref[:, (h0 + g) * HEAD_DIM:(h0 + g + 1) * HEAD_DIM] = o[g * blk:(g + 1) * blk].astype(o_ref.dtype)


def windowed_attention(z, sink, bias):
    s = z.shape[0]
    blk = ATTN_BLOCK
    nb = s // blk
    kcol = Q_DIM // KV_DIM
    vcol = kcol + 1

    def prev(n, t):
        return jnp.maximum(n - 1, 0)

    def nxt(n, t):
        return jnp.minimum(n + 1, nb - 1)

    in_specs = [
        pl.BlockSpec((blk, Q_DIM), lambda n, t: (n, 0)),
        pl.BlockSpec((blk, KV_DIM), lambda n, t: (prev(n, t), kcol)),
        pl.BlockSpec((blk, KV_DIM), lambda n, t: (n, kcol)),
        pl.BlockSpec((blk, KV_DIM), lambda n, t: (nxt(n, t), kcol)),
        pl.BlockSpec((blk, KV_DIM), lambda n, t: (prev(n, t), vcol)),
        pl.BlockSpec((blk, KV_DIM), lambda n, t: (n, vcol)),
        pl.BlockSpec((blk, KV_DIM), lambda n, t: (nxt(n, t), vcol)),
        pl.BlockSpec((N_KV_HEADS, GQA_GROUP * blk, 3 * blk), lambda n, t: (0, 0, 0)),
    ]
    return pl.pallas_call(
        functools.partial(_attn_kernel, nb=nb),
        grid_spec=pltpu.PrefetchScalarGridSpec(
            num_scalar_prefetch=1,
            grid=(nb,),
            in_specs=in_specs,
            out_specs=pl.BlockSpec((blk, Q_DIM), lambda n, t: (n, 0)),
        ),
        out_shape=jax.ShapeDtypeStruct((s, Q_DIM), BF16),
        compiler_params=_cparams(("arbitrary",)),
        name="windowed_attention",
    )(sink, z, z, z, z, z, z, z, bias.reshape(N_KV_HEADS, GQA_GROUP * blk, 3 * blk))


def _fourier_tables(s):
    n2 = LANES
    n1 = s // n2
    k1 = np.arange(n1)[None, :, None]
    s1 = np.arange(n1)[None, None, :]
    s2 = np.arange(n2)[:, None, None]
    ang = 2.0 * np.pi * ((k1 * (s2 + n2 * s1)) % s) / s
    ma = np.concatenate([np.cos(ang), -np.sin(ang)], axis=1)
    a = 2.0 * np.pi * ((np.arange(n2)[:, None] * np.arange(n2)[None, :]) % n2) / n2
    c, sn = np.cos(a), np.sin(a)
    mc = np.block([[c, sn], [-sn, c]])
    scale = 1.0 / math.sqrt(s * FOURIER_GROUP_DIM)
    md = np.concatenate([c, sn], axis=0) * scale
    return (jnp.asarray(ma, BF16), jnp.asarray(mc, BF16), jnp.asarray(md, BF16))


def _fourier_kernel(x_ref, ma_ref, mc_ref, md_ref, o_ref, xf, yr, yi, *, n1):
    n2 = LANES
    xf[...] = x_ref[...].astype(F32)

    def stage_a(s2, carry):
        xs = xf[:, s2, :].astype(BF16)
        y = _dot(ma_ref[s2], xs)
        yr[:, s2, :] = y[:n1]
        yi[:, s2, :] = y[n1:]
        return carry

    lax.fori_loop(0, n2, stage_a, 0)

    def stage_c(k1, carry):
        y = jnp.concatenate([yr[k1], yi[k1]], axis=0).astype(BF16)
        xc = _dot(mc_ref[...], y)
        xx = jnp.concatenate([xc[:n2], xc[n2:]], axis=1).astype(BF16)
        o_ref[:, k1, :] = _dot(xx, md_ref[...])
        return carry

    lax.fori_loop(0, n1, stage_c, 0)


def fourier_mix(z, col0):
    s, w = z.shape
    n2 = LANES
    n1 = s // n2
    c = FOURIER_GROUP_DIM
    ma, mc, md = _fourier_tables(s)
    cb0 = col0 // c
    out = pl.pallas_call(
        functools.partial(_fourier_kernel, n1=n1),
        grid=(N_FOURIER_GROUPS,),
        in_specs=[
            pl.BlockSpec((n1, n2, c), lambda g: (0, 0, cb0 + g)),
            pl.BlockSpec((n2, 2 * n1, n1), lambda g: (0, 0, 0)),
            pl.BlockSpec((2 * n2, 2 * n2), lambda g: (0, 0)),
            pl.BlockSpec((2 * c, c), lambda g: (0, 0)),
        ],
        out_specs=pl.BlockSpec((n2, n1, c), lambda g: (0, 0, g)),
        out_shape=jax.ShapeDtypeStruct((n2, n1, FOURIER_DIM), F32),
        scratch_shapes=[pltpu.VMEM((n1, n2, c), F32)] * 3,
        compiler_params=_cparams(("arbitrary",)),
        name="fourier_mix",
    )(z.reshape(n1, n2, w), ma, mc, md)
    return out.reshape(s, FOURIER_DIM)


def _convpool_kernel(zp_ref, zc_ref, zn_ref, cw_ref, pw_ref, ps_ref, o_ref, *, seq, tt):
    i = pl.program_id(0)
    ext_rows = tt + 2 * HALO
    grow = i * tt - HALO + lax.broadcasted_iota(I32, (ext_rows, 1), 0)
    row_ok = (grow >= 0) & (grow < seq)

    def ext(c0, c1):
        e = jnp.concatenate([zp_ref[:, c0:c1], zc_ref[:, c0:c1], zn_ref[:, c0:c1]], axis=0).astype(F32)
        return jnp.where(row_ok, e, 0.0)

    def shifted(e, d):
        return e[HALO + d:HALO + d + tt]

    prod = ext(CONV_DIM, 2 * CONV_DIM) * ext(2 * CONV_DIM, 3 * CONV_DIM)
    cw = cw_ref[...]
    conv = shifted(prod, -1) * cw[0:1] + shifted(prod, 0) * cw[1:2] + shifted(prod, 1) * cw[2:3]
    o_ref[:, :CONV_DIM] = (zc_ref[:, :CONV_DIM].astype(F32) * conv).astype(o_ref.dtype)

    t = grow[HALO:HALO + tt]
    for g, win in enumerate(POOL_WINDOWS):
        lo = win // 2
        hi = win - 1 - lo
        c0 = 3 * CONV_DIM + g * POOL_GROUP_DIM
        e = ext(c0, c0 + POOL_GROUP_DIM)
        total = shifted(e, -lo)
        for d in range(-lo + 1, hi + 1):
            total = total + shifted(e, d)
        count = (jnp.minimum(t + hi, seq - 1) - jnp.maximum(t - lo, 0) + 1).astype(F32)
        pooled = total / count - shifted(e, 0)
        dg = _dot(pooled.astype(BF16), pw_ref[g].astype(BF16)) * ps_ref[:, g * POOL_GROUP_DIM:(g + 1) * POOL_GROUP_DIM]
        o_ref[:, CONV_DIM + g * POOL_GROUP_DIM:CONV_DIM + (g + 1) * POOL_GROUP_DIM] = dg.astype(o_ref.dtype)


def conv_pool(z, conv_w, pool_w, pool_scale, *, tt):
    s, w = z.shape
    nh = tt // HALO
    last_h = s // HALO - 1
    return pl.pallas_call(
        functools.partial(_convpool_kernel, seq=s, tt=tt),
        grid=(s // tt,),
        in_specs=[
            pl.BlockSpec((HALO, w), lambda i: (jnp.maximum(i * nh - 1, 0), 0)),
            pl.BlockSpec((tt, w), lambda i: (i, 0)),
            pl.BlockSpec((HALO, w), lambda i: (jnp.minimum((i + 1) * nh, last_h), 0)),
            pl.BlockSpec((3, CONV_DIM), lambda i: (0, 0)),
            pl.BlockSpec((POOL_GROUPS, POOL_GROUP_DIM, POOL_GROUP_DIM), lambda i: (0, 0, 0)),
            pl.BlockSpec((1, POOL_DIM), lambda i: (0, 0)),
        ],
        out_specs=pl.BlockSpec((tt, CONV_DIM + POOL_DIM), lambda i: (i, 0)),
        out_shape=jax.ShapeDtypeStruct((s, CONV_DIM + POOL_DIM), BF16),
        compiler_params=_cparams(("arbitrary",)),
        name="conv_pool",
    )(z, z, z, conv_w, pool_w, pool_scale.reshape(1, POOL_DIM))


def _pack_bf16_pairs(y):
    n = y.shape[1] // 2
    lo = lax.bitcast_convert_type(y[:, :n].astype(BF16).astype(F32), I32)
    hi = lax.bitcast_convert_type(y[:, n:].astype(BF16).astype(F32), I32)
    return lax.shift_right_logical(lo, 16) | hi


def _unpack_bf16_pairs(p):
    lo = lax.bitcast_convert_type(lax.shift_left(p, 16), F32).astype(BF16)
    hi = lax.bitcast_convert_type(p & (-65536), F32).astype(BF16)
    return lo, hi


def _store_row_tiles(ref, packed):
    for j in range(SUBLANES):
        ref[:, j, :] = packed[:, j * LANES:(j + 1) * LANES]


def _load_row_tiles(ref):
    return jnp.concatenate([ref[:, j, :] for j in range(SUBLANES)], axis=1)


def _router_kernel(x_ref, g_ref, wr_ref, h_ref, aff_ref):
    x = x_ref[...]
    ms = jnp.mean(x * x, axis=-1, keepdims=True)
    h = x * lax.rsqrt(ms + RMS_EPS) * g_ref[...]
    _store_row_tiles(h_ref, _pack_bf16_pairs(h))
    logits = lax.dot_general(wr_ref[...], h, (((1,), (1,)), ((), ())),
                             preferred_element_type=F32, precision=lax.Precision.HIGHEST)
    m = jnp.max(logits, axis=0, keepdims=True)
    p = jnp.exp(logits - m)
    aff_ref[...] = p / jnp.sum(p, axis=0, keepdims=True)


def router(x, g, w_router, *, tm):
    s, d = x.shape
    assert d == 2 * SUBLANES * LANES
    e = w_router.shape[1]
    return pl.pallas_call(
        _router_kernel,
        grid=(s // tm,),
        in_specs=[
            pl.BlockSpec((tm, d), lambda i: (i, 0)),
            pl.BlockSpec((1, d), lambda i: (0, 0)),
            pl.BlockSpec((e, d), lambda i: (0, 0)),
        ],
        out_specs=[pl.BlockSpec((tm, SUBLANES, LANES), lambda i: (i, 0, 0)), pl.BlockSpec((e, tm), lambda i: (0, i))],
        out_shape=[jax.ShapeDtypeStruct((s, SUBLANES, LANES), I32), jax.ShapeDtypeStruct((e, s), F32)],
        compiler_params=_cparams(("parallel",)),
        name="router",
    )(x, g.reshape(1, d), w_router.T)


def _excl_cumsum_lanes(m, upper):
    r, s = m.shape
    off = jnp.zeros((r, 1), F32)
    pieces = []
    for c in range(s // LANES):
        mc = m[:, c * LANES:(c + 1) * LANES]
        pieces.append(_dot(mc.astype(BF16), upper) + off)
        off = off + jnp.sum(mc, axis=1, keepdims=True)
    return jnp.concatenate(pieces, axis=1)


def _select_kernel(aff_ref, upper_ref, pos_ref, sel_ref, v_ref, *, cap):
    a = aff_ref[...]
    e, s = a.shape
    bits = pltpu.bitcast(a, I32)
    thr = jnp.zeros((e, 1), I32)
    for bit in range(30, -1, -1):
        cand = thr | (1 << bit)
        cnt = jnp.sum((bits >= cand).astype(F32), axis=1, keepdims=True)
        thr = jnp.where(cnt >= cap, cand, thr)
    gt = bits > thr
    eq = (bits == thr).astype(F32)
    need = cap - jnp.sum(gt.astype(F32), axis=1, keepdims=True)
    upper = upper_ref[...]
    eq_rank = _excl_cumsum_lanes(eq, upper)
    sel = jnp.where(gt | ((eq > 0.5) & (eq_rank < need)), 1.0, 0.0)
    pos = _excl_cumsum_lanes(sel, upper)
    pos_ref[...] = pos
    sel_ref[...] = sel
    tok = lax.broadcasted_iota(I32, (1, s), 1).astype(F32)
    tok_hi = jnp.floor(tok * (1.0 / LANES))
    tok_lo = tok - tok_hi * LANES
    g1 = a.astype(BF16).astype(F32)
    g2 = (a - g1).astype(BF16).astype(F32)
    g3 = a - g1 - g2
    zero = jnp.zeros((SUBLANES - 5, s), F32)
    for x in range(e):
        v_ref[x] = jnp.concatenate([tok_hi, tok_lo, g1[x:x + 1], g2[x:x + 1], g3[x:x + 1], zero], axis=0)


def select_tokens(aff, cap):
    e, s = aff.shape
    upper = jnp.asarray(np.triu(np.ones((LANES, LANES), np.float32), k=1), BF16)
    full2 = lambda shp: pl.BlockSpec(shp, lambda i: (0,) * len(shp))
    return pl.pallas_call(
        functools.partial(_select_kernel, cap=cap),
        grid=(1,),
        in_specs=[full2((e, s)), full2((LANES, LANES))],
        out_specs=[full2((e, s)), full2((e, s)), full2((e, SUBLANES, s))],
        out_shape=[
            jax.ShapeDtypeStruct((e, s), F32),
            jax.ShapeDtypeStruct((e, s), F32),
            jax.ShapeDtypeStruct((e, SUBLANES, s), F32),
        ],
        compiler_params=_cparams(("arbitrary",)),
        name="select_tokens",
    )(aff, upper)


def _lists_kernel(off_ref, pos_ref, sel_ref, v_ref, o_ref, acc, *, nchunk, ntile):
    e = pl.program_id(0)
    acc[...] = jnp.zeros(acc.shape, F32)
    slot = lax.broadcasted_iota(I32, (LANES, LANES), 0).astype(F32)

    def body(c, carry):
        j0 = off_ref[e * nchunk + c] // LANES
        st = pl.multiple_of(c * LANES, LANES)
        p = pos_ref[0, :, pl.ds(st, LANES)]
        chosen = sel_ref[0, :, pl.ds(st, LANES)] > 0.5
        v = v_ref[0, :, pl.ds(st, LANES)].astype(BF16)
        for d in range(2):
            base = ((j0 + d) * LANES).astype(F32)
            onehot = jnp.where(((p - base) == slot) & chosen, 1.0, 0.0).astype(BF16)
            acc[j0 + d] += _dot_nt(v, onehot)
        return carry

    lax.fori_loop(0, nchunk, body, 0)
    for j in range(ntile):
        o_ref[0, :, j * LANES:(j + 1) * LANES] = acc[j]


def build_lists(pos, sel, vals, cap):
    e, s = pos.shape
    nchunk = s // LANES
    ntile = cap // LANES
    off = pos[:, ::LANES].astype(I32).reshape(-1)
    row = lambda nrow: pl.BlockSpec((1, nrow, s), lambda x, t: (x, 0, 0))
    lists = pl.pallas_call(
        functools.partial(_lists_kernel, nchunk=nchunk, ntile=ntile),
        grid_spec=pltpu.PrefetchScalarGridSpec(
            num_scalar_prefetch=1,
            grid=(e,),
            in_specs=[row(1), row(1), row(SUBLANES)],
            out_specs=pl.BlockSpec((1, SUBLANES, cap), lambda x, t: (x, 0, 0)),
            scratch_shapes=[pltpu.VMEM((ntile + 1, SUBLANES, LANES), F32)],
        ),
        out_shape=jax.ShapeDtypeStruct((e, SUBLANES, cap), F32),
        compiler_params=_cparams(("arbitrary",)),
        name="build_lists",
    )(off, pos.reshape(e, 1, s), sel.reshape(e, 1, s), vals)
    idx = (lists[:, 0] * LANES + lists[:, 1]).astype(I32)
    gate = (lists[:, 2] + lists[:, 3] + lists[:, 4])
    return idx, gate


def _ffn_kernel(idx_ref, h_hbm, gate_ref, wg_ref, wu_ref, wd_ref, y_ref, xbuf, xb, acc, sem_g, *, cap, nh, ne):
    e = pl.program_id(0)
    hs = pl.program_id(1)
    slot = e % 2
    rps = cap // nh

    def gather_rows(expert, first, count, s):
        def issue(k, carry):
            r = first + k
            t = idx_ref[expert * cap + r]
            pltpu.make_async_copy(h_hbm.at[t], xbuf.at[s, r], sem_g.at[s]).start()
            return carry

        lax.fori_loop(0, count, issue, 0, unroll=8)

    @pl.when((e == 0) & (hs == 0))
    def _():
        gather_rows(0, 0, cap, 0)

    @pl.when(hs == 0)
    def _():
        pltpu.make_async_copy(h_hbm.at[pl.ds(0, cap)], xbuf.at[slot], sem_g.at[slot]).wait()
        lo, hi = _unpack_bf16_pairs(_load_row_tiles(xbuf.at[slot]))
        xb[...] = jnp.concatenate([lo, hi], axis=1)

    @pl.when(e + 1 < ne)
    def _():
        gather_rows(e + 1, hs * rps, rps, 1 - slot)

    x = xb[...]
    gp = _dot(x, wg_ref[...].astype(BF16))
    up = _dot(x, wu_ref[...].astype(BF16))
    hid = (gp * (1.0 / (1.0 + jnp.exp(-gp))) * up).astype(BF16)
    contrib = _dot(hid, wd_ref[...].astype(BF16))

    @pl.when(hs == 0)
    def _():
        acc[...] = contrib

    @pl.when((hs > 0) & (hs < nh - 1))
    def _():
        acc[...] += contrib

    @pl.when(hs == nh - 1)
    def _():
        _store_row_tiles(y_ref, _pack_bf16_pairs((acc[...] + contrib) * gate_ref[0]))


def expert_ffn(h, idx, gate, w_gate, w_up, w_down, li, *, th):
    ne, cap = idx.shape
    d = w_gate.shape[2]
    hidden = w_gate.shape[3]
    nh = hidden // th
    assert nh >= 2 and d == 2 * SUBLANES * LANES
    return pl.pallas_call(
        functools.partial(_ffn_kernel, cap=cap, nh=nh, ne=ne),
        grid_spec=pltpu.PrefetchScalarGridSpec(
            num_scalar_prefetch=1,
            grid=(ne, nh),
            in_specs=[
                pl.BlockSpec(memory_space=pl.ANY),
                pl.BlockSpec((1, cap, 1), lambda x, j, a: (x, 0, 0)),
                pl.BlockSpec((None, None, d, th), lambda x, j, a: (li, x, 0, j)),
                pl.BlockSpec((None, None, d, th), lambda x, j, a: (li, x, 0, j)),
                pl.BlockSpec((None, None, th, d), lambda x, j, a: (li, x, j, 0)),
            ],
            out_specs=pl.BlockSpec((cap, SUBLANES, LANES), lambda x, j, a: (x, 0, 0),
                                   pipeline_mode=pl.Buffered(1)),
            scratch_shapes=[
                pltpu.VMEM((2, cap, SUBLANES, LANES), I32),
                pltpu.VMEM((cap, d), BF16),
                pltpu.VMEM((cap, d), F32),
                pltpu.SemaphoreType.DMA((2,)),
            ],
        ),
        out_shape=jax.ShapeDtypeStruct((ne * cap, SUBLANES, LANES), I32),
        compiler_params=_cparams(("arbitrary", "arbitrary")),
        name="expert_ffn",
    )(idx.reshape(-1), h, gate.reshape(ne, cap, 1), w_gate, w_up, w_down)


def _combine_kernel(off_ref, x_ref, pos_ref, sel_ref, g_ref, y_hbm, o_ref, buf, obuf, acc, sem, osem,
                    *, ne, cap, nchunk, final):
    c = pl.program_id(0)
    slot = c % 2
    win = COMBINE_WINDOW
    nrows = ne * cap
    half = acc.shape[1] // 2

    def chunk_off(e, cc):
        return off_ref[e * (nchunk + 1) + cc]

    def win_row(e, cc, p):
        return jnp.minimum(e * cap + chunk_off(e, cc) + p * win, nrows - win)

    def fetch(e, cc, s):
        return pltpu.make_async_copy(y_hbm.at[pl.ds(win_row(e, cc, 0), win)], buf.at[s, pl.ds(e * win, win)],
                                     sem.at[s])

    @pl.when(c == 0)
    def _():
        for e in range(ne):
            fetch(e, 0, 0).start()

    @pl.when(c + 1 < nchunk)
    def _():
        for e in range(ne):
            fetch(e, c + 1, 1 - slot).start()

    for e in range(ne):
        fetch(e, c, slot).wait()

    pos = pos_ref[...]
    sel = sel_ref[...]

    def membership(e, p, w):
        local = pos[:, e:e + 1] - chunk_off(e, c).astype(F32)
        col = pos[:, e:e + 1] + (e * cap - win_row(e, c, p)).astype(F32)
        ok = (sel[:, e:e + 1] > 0.5) & (local >= p * win) & (local < (p + 1) * win)
        return jnp.where((col == w) & ok, 1.0, 0.0)

    lane = lax.broadcasted_iota(I32, (1, LANES), 1)
    first = lane < win
    w = jnp.where(first, lane, lane - win).astype(F32)
    tiles = []
    for q in range(ne // 2):
        tiles.append(jnp.where(first, membership(2 * q, 0, w), membership(2 * q + 1, 0, w)).astype(BF16))
    member = jnp.concatenate(tiles, axis=1)
    a, b = _unpack_bf16_pairs(_load_row_tiles(buf.at[slot]))
    acc[:, :half] = _dot(member, a)
    acc[:, half:] = _dot(member, b)

    for e in range(ne):
        npieces = (chunk_off(e, c + 1) - chunk_off(e, c) + win - 1) // win

        def extra(p, carry, e=e):
            cp = pltpu.make_async_copy(y_hbm.at[pl.ds(win_row(e, c, p), win)], obuf, osem)
            cp.start()
            cp.wait()
            m = membership(e, p, lax.broadcasted_iota(I32, (1, win), 1).astype(F32)).astype(BF16)
            oa, ob = _unpack_bf16_pairs(_load_row_tiles(obuf))
            acc[:, :half] += _dot(m, oa)
            acc[:, half:] += _dot(m, ob)
            return carry

        lax.fori_loop(1, npieces, extra, 0)

    y = x_ref[...] + acc[...]
    if final:
        ms = jnp.mean(y * y, axis=-1, keepdims=True)
        y = y * lax.rsqrt(ms + RMS_EPS) * g_ref[...]
    o_ref[...] = y


def combine(x, y_rows, pos, sel, cap, final_g, *, tc, final):
    s, d = x.shape
    ne = pos.shape[0]
    nchunk = s // tc
    assert 2 * COMBINE_WINDOW == LANES and ne % 2 == 0 and ne * cap >= COMBINE_WINDOW
    off = jnp.concatenate([pos[:, ::tc].astype(I32), jnp.full((ne, 1), cap, I32)], axis=1).reshape(-1)
    return pl.pallas_call(
        functools.partial(_combine_kernel, ne=ne, cap=cap, nchunk=nchunk, final=final),
        grid_spec=pltpu.PrefetchScalarGridSpec(
            num_scalar_prefetch=1,
            grid=(nchunk,),
            in_specs=[
                pl.BlockSpec((tc, d), lambda c, t: (c, 0)),
                pl.BlockSpec((tc, ne), lambda c, t: (c, 0)),
                pl.BlockSpec((tc, ne), lambda c, t: (c, 0)),
                pl.BlockSpec((1, d), lambda c, t: (0, 0)),
                pl.BlockSpec(memory_space=pl.ANY),
            ],
            out_specs=pl.BlockSpec((tc, d), lambda c, t: (c, 0)),
            scratch_shapes=[
                pltpu.VMEM((2, ne * COMBINE_WINDOW, SUBLANES, LANES), I32),
                pltpu.VMEM((COMBINE_WINDOW, SUBLANES, LANES), I32),
                pltpu.VMEM((tc, d), F32),
                pltpu.SemaphoreType.DMA((2,)),
                pltpu.SemaphoreType.DMA(()),
            ],
        ),
        out_shape=jax.ShapeDtypeStruct((s, d), F32),
        compiler_params=_cparams(("arbitrary",)),
        name="combine",
    )(off, x, pos.T, sel.T, final_g.reshape(1, d), y_rows)


def ec_moe_block(x, g, w_router, w_gate, w_up, w_down, li, final_g, *, final, tm, th, tc):
    s = x.shape[0]
    cap = CAPACITY_FACTOR * s // N_EXPERTS
    h, aff = router(x, g, w_router, tm=tm)
    pos, sel, vals = select_tokens(aff, cap)
    idx, gate = build_lists(pos, sel, vals, cap)
    y_rows = expert_ffn(h, idx, gate, w_gate, w_up, w_down, li, th=th)
    return combine(x, y_rows, pos, sel, cap, final_g, tc=tc, final=final)


def _tiles(s):
    return dict(tm=min(1024, s), tn=512, tr=min(512, s), tt=min(512, s), th=256, tc=min(256, s))


def kernel(x, rel_bias, norm_mix_g, norm_ffn_g, final_norm_g, ev_w_in, ev_sink, ev_w_out, od_w_in, od_conv_w,
           od_pool_w, od_pool_scale, od_w_out, moe_w_router, moe_w_gate, moe_w_up, moe_w_down):
    b, s, d = x.shape
    assert b == 1
    t = _tiles(s)
    depth = norm_mix_g.shape[0]
    bias = window_bias(rel_bias)
    xs = x.reshape(s, d)
    for layer in range(depth):
        i = layer // 2
        if layer % 2 == 0:
            z = norm_matmul(xs, norm_mix_g[layer], ev_w_in, i, tm=t["tm"], tn=t["tn"])
            a_out = windowed_attention(z, ev_sink[i], bias)
            b_out = fourier_mix(z, Q_DIM + 2 * KV_DIM)
            xs = out_proj(xs, [a_out, b_out], ev_w_out, i, tm=t["tm"], tn=t["tn"])
        else:
            z = norm_matmul(xs, norm_mix_g[layer], od_w_in, i, tm=t["tm"], tn=t["tn"])
            mix = conv_pool(z, od_conv_w[i], od_pool_w[i], od_pool_scale[i], tt=t["tt"])
            xs = out_proj(xs, [mix], od_w_out, i, tm=t["tm"], tn=t["tn"])
        xs = ec_moe_block(xs, norm_ffn_g[layer], moe_w_router[layer], moe_w_gate, moe_w_up, moe_w_down, layer,
                          final_norm_g, final=(layer == depth - 1), tm=t["tr"], th=t["th"], tc=t["tc"])
    return xs.reshape(b, s, d)
```

```python
import functools
import math

import numpy as np
import jax
import jax.numpy as jnp
from jax import lax
from jax.experimental import pallas as pl
from jax.experimental.pallas import tpu as pltpu

F32 = jnp.float32
BF16 = jnp.bfloat16
I32 = jnp.int32

HEAD_DIM = 128
N_Q_HEADS = 12
N_KV_HEADS = 4
GQA_GROUP = N_Q_HEADS // N_KV_HEADS
WINDOW = 128
ATTN_BLOCK = 128
N_FOURIER_GROUPS = 4
FOURIER_GROUP_DIM = 128
Q_DIM = N_Q_HEADS * HEAD_DIM
KV_DIM = N_KV_HEADS * HEAD_DIM
FOURIER_DIM = N_FOURIER_GROUPS * FOURIER_GROUP_DIM
N_REL_BUCKETS = 32
REL_MAX_DISTANCE = 128
CONV_DIM = 1024
POOL_WINDOWS = (2, 4, 8, 16)
POOL_GROUPS = len(POOL_WINDOWS)
POOL_GROUP_DIM = 256
POOL_DIM = POOL_GROUPS * POOL_GROUP_DIM
N_EXPERTS = 16
CAPACITY_FACTOR = 2
RMS_EPS = 1e-6
NEG_INF = -1e30

LANES = 128
SUBLANES = 8
BF16_ROWS = 16
VMEM_LIMIT = 56 * 1024 * 1024

HALO = BF16_ROWS
COMBINE_WINDOW = 64


def _cparams(sem):
    return pltpu.CompilerParams(dimension_semantics=sem, vmem_limit_bytes=VMEM_LIMIT)


def _dot(a, b):
    return jnp.dot(a, b, preferred_element_type=F32)


def _dot_nt(a, b):
    return lax.dot_general(a, b, (((1,), (1,)), ((), ())), preferred_element_type=F32)


def _norm_matmul_kernel(x_ref, g_ref, w_ref, o_ref, h_ref):
    @pl.when(pl.program_id(1) == 0)
    def _():
        x = x_ref[...]
        ms = jnp.mean(x * x, axis=-1, keepdims=True)
        h_ref[...] = (x * lax.rsqrt(ms + RMS_EPS) * g_ref[...]).astype(BF16)

    o_ref[...] = _dot(h_ref[...], w_ref[...].astype(BF16)).astype(o_ref.dtype)


def norm_matmul(x, g, w, li, *, tm, tn):
    s, d = x.shape
    n = w.shape[2]
    return pl.pallas_call(
        _norm_matmul_kernel,
        grid=(s // tm, n // tn),
        in_specs=[
            pl.BlockSpec((tm, d), lambda i, j: (i, 0)),
            pl.BlockSpec((1, d), lambda i, j: (0, 0)),
            pl.BlockSpec((None, d, tn), lambda i, j: (li, 0, j)),
        ],
        out_specs=pl.BlockSpec((tm, tn), lambda i, j: (i, j)),
        out_shape=jax.ShapeDtypeStruct((s, n), BF16),
        scratch_shapes=[pltpu.VMEM((tm, d), BF16)],
        compiler_params=_cparams(("parallel", "arbitrary")),
        name="norm_matmul",
    )(x, g.reshape(1, d), w)


def _out_proj_kernel(*refs, nparts):
    x_ref = refs[0]
    p_refs = refs[1:1 + nparts]
    w_refs = refs[1 + nparts:1 + 2 * nparts]
    o_ref = refs[1 + 2 * nparts]
    acc = x_ref[...]
    for p_ref, w_ref in zip(p_refs, w_refs):
        acc = acc + _dot(p_ref[...].astype(BF16), w_ref[...].astype(BF16))
    o_ref[...] = acc


def out_proj(x, parts, w, li, *, tm, tn):
    s, d = x.shape
    nparts = len(parts)
    in_specs = [pl.BlockSpec((tm, tn), lambda i, j: (i, j))]
    for p in parts:
        in_specs.append(pl.BlockSpec((tm, p.shape[1]), lambda i, j: (i, 0)))
    off = 0
    for p in parts:
        width = p.shape[1]
        assert off % width == 0
        rb = off // width
        in_specs.append(pl.BlockSpec((None, width, tn), lambda i, j, rb=rb: (li, rb, j)))
        off += width
    assert off == w.shape[1]
    return pl.pallas_call(
        functools.partial(_out_proj_kernel, nparts=nparts),
        grid=(s // tm, d // tn),
        in_specs=in_specs,
        out_specs=pl.BlockSpec((tm, tn), lambda i, j: (i, j)),
        out_shape=jax.ShapeDtypeStruct((s, d), F32),
        compiler_params=_cparams(("parallel", "arbitrary")),
        name="out_proj",
    )(x, *parts, *([w] * nparts))


def _t5_bucket(rel):
    nb = N_REL_BUCKETS // 2
    max_exact = nb // 2
    ret = (rel > 0).astype(jnp.int32) * nb
    n = jnp.abs(rel)
    nf = jnp.maximum(n, 1).astype(jnp.float32)
    large = max_exact + (jnp.log(nf / max_exact) / math.log(REL_MAX_DISTANCE / max_exact)
                         * (nb - max_exact)).astype(jnp.int32)
    large = jnp.minimum(large, nb - 1)
    return ret + jnp.where(n < max_exact, n, large)


def _bias_kernel(tab_ref, bucket_ref, o_ref):
    h = pl.program_id(0)
    bucket = bucket_ref[...]
    acc = jnp.zeros(bucket.shape, F32)
    for b in range(N_REL_BUCKETS):
        acc = jnp.where(bucket == b, tab_ref[b * N_Q_HEADS + h], acc)
    i = lax.broadcasted_iota(I32, bucket.shape, 0)
    j = lax.broadcasted_iota(I32, bucket.shape, 1)
    valid = jnp.abs(j - WINDOW - i) <= WINDOW
    o_ref[0] = jnp.where(valid, acc, NEG_INF)


def window_bias(rel_bias):
    i = jnp.arange(ATTN_BLOCK, dtype=jnp.int32)[:, None]
    j = jnp.arange(3 * ATTN_BLOCK, dtype=jnp.int32)[None, :]
    bucket = _t5_bucket((j - WINDOW) - i)
    return pl.pallas_call(
        _bias_kernel,
        grid_spec=pltpu.PrefetchScalarGridSpec(
            num_scalar_prefetch=1,
            grid=(N_Q_HEADS,),
            in_specs=[pl.BlockSpec((ATTN_BLOCK, 3 * ATTN_BLOCK), lambda h, t: (0, 0))],
            out_specs=pl.BlockSpec((1, ATTN_BLOCK, 3 * ATTN_BLOCK), lambda h, t: (h, 0, 0)),
        ),
        out_shape=jax.ShapeDtypeStruct((N_Q_HEADS, ATTN_BLOCK, 3 * ATTN_BLOCK), F32),
        compiler_params=_cparams(("arbitrary",)),
        name="window_bias",
    )(rel_bias.reshape(-1), bucket)


def _attn_kernel(sink_ref, q_ref, kp_ref, kc_ref, kn_ref, vp_ref, vc_ref, vn_ref, bias_ref, o_ref, *, nb):
    n = pl.program_id(0)
    blk = ATTN_BLOCK
    rows = GQA_GROUP * blk
    col = lax.broadcasted_iota(I32, (rows, 3 * blk), 1)
    edge_ok = ((col >= blk) | (n > 0)) & ((col < 2 * blk) | (n < nb - 1))
    row = lax.broadcasted_iota(I32, (rows, 1), 0)
    scale = HEAD_DIM ** -0.5
    for kv in range(N_KV_HEADS):
        cs = slice(kv * HEAD_DIM, (kv + 1) * HEAD_DIM)
        k = jnp.concatenate([kp_ref[:, cs], kc_ref[:, cs], kn_ref[:, cs]], axis=0)
        v = jnp.concatenate([vp_ref[:, cs], vc_ref[:, cs], vn_ref[:, cs]], axis=0)
        h0 = kv * GQA_GROUP
        q = jnp.concatenate(
            [q_ref[:, (h0 + g) * HEAD_DIM:(h0 + g + 1) * HEAD_DIM] for g in range(GQA_GROUP)], axis=0)
        s = _dot_nt(q, k) * scale + bias_ref[kv]
        s = jnp.where(edge_ok, s, NEG_INF)
        sk = jnp.full((rows, 1), sink_ref[h0 + GQA_GROUP - 1], F32)
        for g in range(GQA_GROUP - 2, -1, -1):
            sk = jnp.where(row < (g + 1) * blk, sink_ref[h0 + g], sk)
        m = jnp.maximum(jnp.max(s, axis=-1, keepdims=True), sk)
        p = jnp.exp(s - m)
        denom = jnp.sum(p, axis=-1, keepdims=True) + jnp.exp(sk - m)
        o = _dot(p.astype(BF16), v) / denom
        for g in range(GQA_GROUP):
            o_ref[:, (h0 + g) * HEAD_DIM:(h0 + g + 1) * HEAD_DIM] = o[g * blk:(g + 1) * blk].astype(o_ref.dtype)


def windowed_attention(z, sink, bias):
    s = z.shape[0]
    blk = ATTN_BLOCK
    nb = s // blk
    kcol = Q_DIM // KV_DIM
    vcol = kcol + 1

    def prev(n, t):
        return jnp.maximum(n - 1, 0)

    def nxt(n, t):
        return jnp.minimum(n + 1, nb - 1)

    in_specs = [
        pl.BlockSpec((blk, Q_DIM), lambda n, t: (n, 0)),
        pl.BlockSpec((blk, KV_DIM), lambda n, t: (prev(n, t), kcol)),
        pl.BlockSpec((blk, KV_DIM), lambda n, t: (n, kcol)),
        pl.BlockSpec((blk, KV_DIM), lambda n, t: (nxt(n, t), kcol)),
        pl.BlockSpec((blk, KV_DIM), lambda n, t: (prev(n, t), vcol)),
        pl.BlockSpec((blk, KV_DIM), lambda n, t: (n, vcol)),
        pl.BlockSpec((blk, KV_DIM), lambda n, t: (nxt(n, t), vcol)),
        pl.BlockSpec((N_KV_HEADS, GQA_GROUP * blk, 3 * blk), lambda n, t: (0, 0, 0)),
    ]
    return pl.pallas_call(
        functools.partial(_attn_kernel, nb=nb),
        grid_spec=pltpu.PrefetchScalarGridSpec(
            num_scalar_prefetch=1,
            grid=(nb,),
            in_specs=in_specs,
            out_specs=pl.BlockSpec((blk, Q_DIM), lambda n, t: (n, 0)),
        ),
        out_shape=jax.ShapeDtypeStruct((s, Q_DIM), BF16),
        compiler_params=_cparams(("arbitrary",)),
        name="windowed_attention",
    )(sink, z, z, z, z, z, z, z, bias.reshape(N_KV_HEADS, GQA_GROUP * blk, 3 * blk))


def _fourier_tables(s):
    n2 = LANES
    n1 = s // n2
    k1 = np.arange(n1)[None, :, None]
    s1 = np.arange(n1)[None, None, :]
    s2 = np.arange(n2)[:, None, None]
    ang = 2.0 * np.pi * ((k1 * (s2 + n2 * s1)) % s) / s
    ma = np.concatenate([np.cos(ang), -np.sin(ang)], axis=1)
    a = 2.0 * np.pi * ((np.arange(n2)[:, None] * np.arange(n2)[None, :]) % n2) / n2
    c, sn = np.cos(a), np.sin(a)
    mc = np.block([[c, sn], [-sn, c]])
    scale = 1.0 / math.sqrt(s * FOURIER_GROUP_DIM)
    md = np.concatenate([c, sn], axis=0) * scale
    return (jnp.asarray(ma, BF16), jnp.asarray(mc, BF16), jnp.asarray(md, BF16))


def _fourier_kernel(x_ref, ma_ref, mc_ref, md_ref, o_ref, xf, yr, yi, *, n1):
    n2 = LANES
    xf[...] = x_ref[...].astype(F32)

    def stage_a(s2, carry):
        xs = xf[:, s2, :].astype(BF16)
        y = _dot(ma_ref[s2], xs)
        yr[:, s2, :] = y[:n1]
        yi[:, s2, :] = y[n1:]
        return carry

    lax.fori_loop(0, n2, stage_a, 0, unroll=8)

    def stage_c(k1, carry):
        y = jnp.concatenate([yr[k1], yi[k1]], axis=0).astype(BF16)
        xc = _dot(mc_ref[...], y)
        xx = jnp.concatenate([xc[:n2], xc[n2:]], axis=1).astype(BF16)
        o_ref[:, k1, :] = _dot(xx, md_ref[...])
        return carry

    lax.fori_loop(0, n1, stage_c, 0, unroll=4)


def fourier_mix(z, col0):
    s, w = z.shape
    n2 = LANES
    n1 = s // n2
    c = FOURIER_GROUP_DIM
    ma, mc, md = _fourier_tables(s)
    cb0 = col0 // c
    out = pl.pallas_call(
        functools.partial(_fourier_kernel, n1=n1),
        grid=(N_FOURIER_GROUPS,),
        in_specs=[
            pl.BlockSpec((n1, n2, c), lambda g: (0, 0, cb0 + g)),
            pl.BlockSpec((n2, 2 * n1, n1), lambda g: (0, 0, 0)),
            pl.BlockSpec((2 * n2, 2 * n2), lambda g: (0, 0)),
            pl.BlockSpec((2 * c, c), lambda g: (0, 0)),
        ],
        out_specs=pl.BlockSpec((n2, n1, c), lambda g: (0, 0, g)),
        out_shape=jax.ShapeDtypeStruct((n2, n1, FOURIER_DIM), F32),
        scratch_shapes=[pltpu.VMEM((n1, n2, c), F32)] * 3,
        compiler_params=_cparams(("arbitrary",)),
        name="fourier_mix",
    )(z.reshape(n1, n2, w), ma, mc, md)
    return out.reshape(s, FOURIER_DIM)


def _convpool_kernel(zp_ref, zc_ref, zn_ref, cw_ref, pw_ref, ps_ref, o_ref, *, seq, tt):
    i = pl.program_id(0)
    ext_rows = tt + 2 * HALO
    grow = i * tt - HALO + lax.broadcasted_iota(I32, (ext_rows, 1), 0)
    row_ok = (grow >= 0) & (grow < seq)

    def ext(c0, c1):
        e = jnp.concatenate([zp_ref[:, c0:c1], zc_ref[:, c0:c1], zn_ref[:, c0:c1]], axis=0).astype(F32)
        return jnp.where(row_ok, e, 0.0)

    def shifted(e, d):
        return e[HALO + d:HALO + d + tt]

    prod = ext(CONV_DIM, 2 * CONV_DIM) * ext(2 * CONV_DIM, 3 * CONV_DIM)
    cw = cw_ref[...]
    conv = shifted(prod, -1) * cw[0:1] + shifted(prod, 0) * cw[1:2] + shifted(prod, 1) * cw[2:3]
    o_ref[:, :CONV_DIM] = (zc_ref[:, :CONV_DIM].astype(F32) * conv).astype(o_ref.dtype)

    t = grow[HALO:HALO + tt]
    for g, win in enumerate(POOL_WINDOWS):
        lo = win // 2
        hi = win - 1 - lo
        c0 = 3 * CONV_DIM + g * POOL_GROUP_DIM
        e = ext(c0, c0 + POOL_GROUP_DIM)
        total = shifted(e, -lo)
        for d in range(-lo + 1, hi + 1):
            total = total + shifted(e, d)
        count = (jnp.minimum(t + hi, seq - 1) - jnp.maximum(t - lo, 0) + 1).astype(F32)
        pooled = total / count - shifted(e, 0)
        dg = _dot(pooled.astype(BF16), pw_ref[g].astype(BF16)) * ps_ref[:, g * POOL_GROUP_DIM:(g + 1) * POOL_GROUP_DIM]
        o_ref[:, CONV_DIM + g * POOL_GROUP_DIM:CONV_DIM + (g + 1) * POOL_GROUP_DIM] = dg.astype(o_ref.dtype)


def conv_pool(z, conv_w, pool_w, pool_scale, *, tt):
    s, w = z.shape
    nh = tt // HALO
    last_h = s // HALO - 1
    return pl.pallas_call(
        functools.partial(_convpool_kernel, seq=s, tt=tt),
        grid=(s // tt,),
        in_specs=[
            pl.BlockSpec((HALO, w), lambda i: (jnp.maximum(i * nh - 1, 0), 0)),
            pl.BlockSpec((tt, w), lambda i: (i, 0)),
            pl.BlockSpec((HALO, w), lambda i: (jnp.minimum((i + 1) * nh, last_h), 0)),
            pl.BlockSpec((3, CONV_DIM), lambda i: (0, 0)),
            pl.BlockSpec((POOL_GROUPS, POOL_GROUP_DIM, POOL_GROUP_DIM), lambda i: (0, 0, 0)),
            pl.BlockSpec((1, POOL_DIM), lambda i: (0, 0)),
        ],
        out_specs=pl.BlockSpec((tt, CONV_DIM + POOL_DIM), lambda i: (i, 0)),
        out_shape=jax.ShapeDtypeStruct((s, CONV_DIM + POOL_DIM), BF16),
        compiler_params=_cparams(("arbitrary",)),
        name="conv_pool",
    )(z, z, z, conv_w, pool_w, pool_scale.reshape(1, POOL_DIM))


def _pack_bf16_pairs(y):
    n = y.shape[1] // 2
    lo = lax.bitcast_convert_type(y[:, :n].astype(BF16).astype(F32), I32)
    hi = lax.bitcast_convert_type(y[:, n:].astype(BF16).astype(F32), I32)
    return lax.shift_right_logical(lo, 16) | hi


def _unpack_bf16_pairs(p):
    lo = lax.bitcast_convert_type(lax.shift_left(p, 16), F32).astype(BF16)
    hi = lax.bitcast_convert_type(p & (-65536), F32).astype(BF16)
    return lo, hi


def _store_row_tiles(ref, packed):
    rows = packed.shape[0]
    for j in range(SUBLANES):
        ref[pl.ds(j, rows, stride=SUBLANES), :] = packed[:, j * LANES:(j + 1) * LANES]


def _load_row_tiles(ref):
    rows = ref.shape[0] // SUBLANES
    return jnp.concatenate([ref[pl.ds(j, rows, stride=SUBLANES), :] for j in range(SUBLANES)], axis=1)


def _router_kernel(x_ref, g_ref, wr_ref, h_ref, aff_ref):
    x = x_ref[...]
    ms = jnp.mean(x * x, axis=-1, keepdims=True)
    h = x * lax.rsqrt(ms + RMS_EPS) * g_ref[...]
    _store_row_tiles(h_ref, _pack_bf16_pairs(h))
    logits = lax.dot_general(wr_ref[...], h, (((1,), (1,)), ((), ())),
                             preferred_element_type=F32, precision=lax.Precision.HIGHEST)
    m = jnp.max(logits, axis=0, keepdims=True)
    p = jnp.exp(logits - m)
    aff_ref[...] = p / jnp.sum(p, axis=0, keepdims=True)


def router(x, g, w_router, *, tm):
    s, d = x.shape
    assert d == 2 * SUBLANES * LANES
    e = w_router.shape[1]
    return pl.pallas_call(
        _router_kernel,
        grid=(s // tm,),
        in_specs=[
            pl.BlockSpec((tm, d), lambda i: (i, 0)),
            pl.BlockSpec((1, d), lambda i: (0, 0)),
            pl.BlockSpec((e, d), lambda i: (0, 0)),
        ],
        out_specs=[pl.BlockSpec((tm * SUBLANES, LANES), lambda i: (i, 0)), pl.BlockSpec((e, tm), lambda i: (0, i))],
        out_shape=[jax.ShapeDtypeStruct((s * SUBLANES, LANES), I32), jax.ShapeDtypeStruct((e, s), F32)],
        compiler_params=_cparams(("parallel",)),
        name="router",
    )(x, g.reshape(1, d), w_router.T)


def _excl_cumsum_lanes(m, upper):
    r, s = m.shape
    off = jnp.zeros((r, 1), F32)
    pieces = []
    for c in range(s // LANES):
        mc = m[:, c * LANES:(c + 1) * LANES]
        pieces.append(_dot(mc.astype(BF16), upper) + off)
        off = off + jnp.sum(mc, axis=1, keepdims=True)
    return jnp.concatenate(pieces, axis=1)


def _select_kernel(aff_ref, upper_ref, pos_ref, sel_ref, v_ref, *, cap):
    a = aff_ref[...]
    e, s = a.shape
    bits = pltpu.bitcast(a, I32)
    thr = jnp.zeros((e, 1), I32)
    for bit in range(30, -1, -1):
        cand = thr | (1 << bit)
        cnt = jnp.sum((bits >= cand).astype(F32), axis=1, keepdims=True)
        thr = jnp.where(cnt >= cap, cand, thr)
    gt = bits > thr
    eq = (bits == thr).astype(F32)
    need = cap - jnp.sum(gt.astype(F32), axis=1, keepdims=True)
    upper = upper_ref[...]
    eq_rank = _excl_cumsum_lanes(eq, upper)
    sel = jnp.where(gt | ((eq > 0.5) & (eq_rank < need)), 1.0, 0.0)
    pos = _excl_cumsum_lanes(sel, upper)
    pos_ref[...] = pos
    sel_ref[...] = sel
    tok = lax.broadcasted_iota(I32, (1, s), 1).astype(F32)
    tok_hi = jnp.floor(tok * (1.0 / LANES))
    tok_lo = tok - tok_hi * LANES
    g1 = a.astype(BF16).astype(F32)
    g2 = (a - g1).astype(BF16).astype(F32)
    g3 = a - g1 - g2
    zero = jnp.zeros((SUBLANES - 5, s), F32)
    for x in range(e):
        v_ref[x] = jnp.concatenate([tok_hi, tok_lo, g1[x:x + 1], g2[x:x + 1], g3[x:x + 1], zero], axis=0)


def select_tokens(aff, cap):
    e, s = aff.shape
    upper = jnp.asarray(np.triu(np.ones((LANES, LANES), np.float32), k=1), BF16)
    full2 = lambda shp: pl.BlockSpec(shp, lambda i: (0,) * len(shp))
    return pl.pallas_call(
        functools.partial(_select_kernel, cap=cap),
        grid=(1,),
        in_specs=[full2((e, s)), full2((LANES, LANES))],
        out_specs=[full2((e, s)), full2((e, s)), full2((e, SUBLANES, s))],
        out_shape=[
            jax.ShapeDtypeStruct((e, s), F32),
            jax.ShapeDtypeStruct((e, s), F32),
            jax.ShapeDtypeStruct((e, SUBLANES, s), F32),
        ],
        compiler_params=_cparams(("arbitrary",)),
        name="select_tokens",
    )(aff, upper)


def _lists_kernel(off_ref, pos_ref, sel_ref, v_ref, o_ref, acc, *, nchunk, ntile):
    e = pl.program_id(0)
    acc[...] = jnp.zeros(acc.shape, F32)
    slot = lax.broadcasted_iota(I32, (LANES, LANES), 0).astype(F32)

    def body(c, carry):
        j0 = jnp.minimum(off_ref[e * nchunk + c] // LANES, ntile - 1)
        st = pl.multiple_of(c * LANES, LANES)
        p = pos_ref[0, :, pl.ds(st, LANES)]
        chosen = sel_ref[0, :, pl.ds(st, LANES)] > 0.5
        v = v_ref[0, :, pl.ds(st, LANES)].astype(BF16)
        for d in range(2):
            base = ((j0 + d) * LANES).astype(F32)
            onehot = jnp.where(((p - base) == slot) & chosen, 1.0, 0.0).astype(BF16)
            acc[j0 + d] += _dot_nt(v, onehot)
        return carry

    lax.fori_loop(0, nchunk, body, 0, unroll=8)
    for j in range(ntile):
        o_ref[0, :, j * LANES:(j + 1) * LANES] = acc[j]


def build_lists(pos, sel, vals, cap):
    e, s = pos.shape
    nchunk = s // LANES
    ntile = cap // LANES
    off = pos[:, ::LANES].astype(I32).reshape(-1)
    row = lambda nrow: pl.BlockSpec((1, nrow, s), lambda x, t: (x, 0, 0))
    lists = pl.pallas_call(
        functools.partial(_lists_kernel, nchunk=nchunk, ntile=ntile),
        grid_spec=pltpu.PrefetchScalarGridSpec(
            num_scalar_prefetch=1,
            grid=(e,),
            in_specs=[row(1), row(1), row(SUBLANES)],
            out_specs=pl.BlockSpec((1, SUBLANES, cap), lambda x, t: (x, 0, 0)),
            scratch_shapes=[pltpu.VMEM((ntile + 1, SUBLANES, LANES), F32)],
        ),
        out_shape=jax.ShapeDtypeStruct((e, SUBLANES, cap), F32),
        compiler_params=_cparams(("arbitrary",)),
        name="build_lists",
    )(off, pos.reshape(e, 1, s), sel.reshape(e, 1, s), vals)
    idx = (lists[:, 0] * LANES + lists[:, 1]).astype(I32)
    gate = (lists[:, 2] + lists[:, 3] + lists[:, 4])
    return idx, gate


def _ffn_kernel(idx_ref, h_hbm, gate_ref, wg_ref, wu_ref, wd_ref, y_ref, xbuf, xb, acc, sem_g, *, cap, nh, ne):
    e = pl.program_id(0)
    hs = pl.program_id(1)
    slot = e % 2
    rps = cap // nh

    def gather_rows(expert, first, count, s):
        def issue(k, carry):
            r = first + k
            t = idx_ref[expert * cap + r]
            pltpu.make_async_copy(h_hbm.at[pl.ds(pl.multiple_of(t * SUBLANES, SUBLANES), SUBLANES)],
                                  xbuf.at[s, pl.ds(pl.multiple_of(r * SUBLANES, SUBLANES), SUBLANES)],
                                  sem_g.at[s]).start()
            return carry

        lax.fori_loop(0, count, issue, 0, unroll=8)

    @pl.when((e == 0) & (hs == 0))
    def _():
        gather_rows(0, 0, cap, 0)

    @pl.when(hs == 0)
    def _():
        pltpu.make_async_copy(h_hbm.at[pl.ds(0, cap * SUBLANES)], xbuf.at[slot], sem_g.at[slot]).wait()
        lo, hi = _unpack_bf16_pairs(_load_row_tiles(xbuf.at[slot]))
        xb[...] = jnp.concatenate([lo, hi], axis=1)
        acc[...] = jnp.zeros(acc.shape, F32)

    @pl.when(e + 1 < ne)
    def _():
        gather_rows(e + 1, hs * rps, rps, 1 - slot)

    th = wg_ref.shape[1]
    w_gu = jnp.concatenate([wg_ref[...].astype(BF16), wu_ref[...].astype(BF16)], axis=1)
    gu = _dot(xb[...], w_gu)
    gp = gu[:, :th]
    hid = (gp * (1.0 / (1.0 + jnp.exp(-gp))) * gu[:, th:]).astype(BF16)
    acc[...] += _dot(hid, wd_ref[...].astype(BF16))

    @pl.when(hs == nh - 1)
    def _():
        _store_row_tiles(y_ref, _pack_bf16_pairs(acc[...] * gate_ref[0]))


def expert_ffn(h, idx, gate, w_gate, w_up, w_down, li, *, th):
    ne, cap = idx.shape
    d = w_gate.shape[2]
    hidden = w_gate.shape[3]
    nh = hidden // th
    assert d == 2 * SUBLANES * LANES
    return pl.pallas_call(
        functools.partial(_ffn_kernel, cap=cap, nh=nh, ne=ne),
        grid_spec=pltpu.PrefetchScalarGridSpec(
            num_scalar_prefetch=1,
            grid=(ne, nh),
            in_specs=[
                pl.BlockSpec(memory_space=pl.ANY),
                pl.BlockSpec((1, cap, 1), lambda x, j, a: (x, 0, 0)),
                pl.BlockSpec((None, None, d, th), lambda x, j, a: (li, x, 0, j)),
                pl.BlockSpec((None, None, d, th), lambda x, j, a: (li, x, 0, j)),
                pl.BlockSpec((None, None, th, d), lambda x, j, a: (li, x, j, 0)),
            ],
            out_specs=pl.BlockSpec((cap * SUBLANES, LANES), lambda x, j, a: (x, 0),
                                   pipeline_mode=pl.Buffered(1)),
            scratch_shapes=[
                pltpu.VMEM((2, cap * SUBLANES, LANES), I32),
                pltpu.VMEM((cap, d), BF16),
                pltpu.VMEM((cap, d), F32),
                pltpu.SemaphoreType.DMA((2,)),
            ],
        ),
        out_shape=jax.ShapeDtypeStruct((ne * cap * SUBLANES, LANES), I32),
        compiler_params=_cparams(("arbitrary", "arbitrary")),
        name="expert_ffn",
    )(idx.reshape(-1), h, gate.reshape(ne, cap, 1), w_gate, w_up, w_down)


def _combine_kernel(off_ref, x_ref, pos_ref, sel_ref, g_ref, y_hbm, o_ref, buf, obuf, acc, sem, osem,
                    *, ne, cap, nchunk, final):
    c = pl.program_id(0)
    slot = c % 2
    win = COMBINE_WINDOW
    nrows = ne * cap
    half = acc.shape[1] // 2

    def chunk_off(e, cc):
        return off_ref[e * (nchunk + 1) + cc]

    def win_row(e, cc, p):
        return jnp.minimum(e * cap + chunk_off(e, cc) + p * win, nrows - win)

    def tiles(row, count):
        return pl.ds(pl.multiple_of(row * SUBLANES, SUBLANES), count * SUBLANES)

    def fetch(e, cc, s):
        return pltpu.make_async_copy(y_hbm.at[tiles(win_row(e, cc, 0), win)], buf.at[s, tiles(e * win, win)],
                                     sem.at[s])

    @pl.when(c == 0)
    def _():
        for e in range(ne):
            fetch(e, 0, 0).start()

    @pl.when(c + 1 < nchunk)
    def _():
        for e in range(ne):
            fetch(e, c + 1, 1 - slot).start()

    for e in range(ne):
        fetch(e, c, slot).wait()

    pos = pos_ref[...]
    sel = sel_ref[...]

    def membership(e, p, w):
        local = pos[:, e:e + 1] - chunk_off(e, c).astype(F32)
        col = pos[:, e:e + 1] + (e * cap - win_row(e, c, p)).astype(F32)
        ok = (sel[:, e:e + 1] > 0.5) & (local >= p * win) & (local < (p + 1) * win)
        return jnp.where((col == w) & ok, 1.0, 0.0)

    lane = lax.broadcasted_iota(I32, (1, LANES), 1)
    first = lane < win
    w = jnp.where(first, lane, lane - win).astype(F32)
    member = jnp.concatenate(
        [jnp.where(first, membership(2 * q, 0, w), membership(2 * q + 1, 0, w)).astype(BF16)
         for q in range(ne // 2)], axis=1)
    a, b = _unpack_bf16_pairs(_load_row_tiles(buf.at[slot]))
    acc[:, :half] = _dot(member, a)
    acc[:, half:] = _dot(member, b)

    for e in range(ne):
        npieces = (chunk_off(e, c + 1) - chunk_off(e, c) + win - 1) // win

        def extra(p, carry, e=e):
            cp = pltpu.make_async_copy(y_hbm.at[tiles(win_row(e, c, p), win)], obuf, osem)
            cp.start()
            cp.wait()
            m = membership(e, p, lax.broadcasted_iota(I32, (1, win), 1).astype(F32)).astype(BF16)
            oa, ob = _unpack_bf16_pairs(_load_row_tiles(obuf))
            acc[:, :half] += _dot(m, oa)
            acc[:, half:] += _dot(m, ob)
            return carry

        lax.fori_loop(1, npieces, extra, 0)

    y = x_ref[...] + acc[...]
    if final:
        ms = jnp.mean(y * y, axis=-1, keepdims=True)
        y = y * lax.rsqrt(ms + RMS_EPS) * g_ref[...]
    o_ref[...] = y


def combine(x, y_rows, pos, sel, cap, final_g, *, tc, final):
    s, d = x.shape
    ne = pos.shape[0]
    nchunk = s // tc
    assert 2 * COMBINE_WINDOW == LANES and ne % 2 == 0 and ne * cap >= COMBINE_WINDOW
    off = jnp.concatenate([pos[:, ::tc].astype(I32), jnp.full((ne, 1), cap, I32)], axis=1).reshape(-1)
    return pl.pallas_call(
        functools.partial(_combine_kernel, ne=ne, cap=cap, nchunk=nchunk, final=final),
        grid_spec=pltpu.PrefetchScalarGridSpec(
            num_scalar_prefetch=1,
            grid=(nchunk,),
            in_specs=[
                pl.BlockSpec((tc, d), lambda c, t: (c, 0)),
                pl.BlockSpec((tc, ne), lambda c, t: (c, 0)),
                pl.BlockSpec((tc, ne), lambda c, t: (c, 0)),
                pl.BlockSpec((1, d), lambda c, t: (0, 0)),
                pl.BlockSpec(memory_space=pl.ANY),
            ],
            out_specs=pl.BlockSpec((tc, d), lambda c, t: (c, 0)),
            scratch_shapes=[
                pltpu.VMEM((2, ne * COMBINE_WINDOW * SUBLANES, LANES), I32),
                pltpu.VMEM((COMBINE_WINDOW * SUBLANES, LANES), I32),
                pltpu.VMEM((tc, d), F32),
                pltpu.SemaphoreType.DMA((2,)),
                pltpu.SemaphoreType.DMA(()),
            ],
        ),
        out_shape=jax.ShapeDtypeStruct((s, d), F32),
        compiler_params=_cparams(("arbitrary",)),
        name="combine",
    )(off, x, pos.T, sel.T, final_g.reshape(1, d), y_rows)


def ec_moe_block(x, g, w_router, w_gate, w_up, w_down, li, final_g, *, final, tm, th, tc):
    s = x.shape[0]
    cap = CAPACITY_FACTOR * s // N_EXPERTS
    h, aff = router(x, g, w_router, tm=tm)
    pos, sel, vals = select_tokens(aff, cap)
    idx, gate = build_lists(pos, sel, vals, cap)
    y_rows = expert_ffn(h, idx, gate, w_gate, w_up, w_down, li, th=th)
    return combine(x, y_rows, pos, sel, cap, final_g, tc=tc, final=final)


def _tiles(s):
    return dict(tm=min(1024, s), tn=512, tr=min(512, s), tt=min(512, s), th=256, tc=min(256, s))


def kernel(x, rel_bias, norm_mix_g, norm_ffn_g, final_norm_g, ev_w_in, ev_sink, ev_w_out, od_w_in, od_conv_w,
           od_pool_w, od_pool_scale, od_w_out, moe_w_router, moe_w_gate, moe_w_up, moe_w_down):
    b, s, d = x.shape
    assert b == 1
    t = _tiles(s)
    depth = norm_mix_g.shape[0]
    bias = window_bias(rel_bias)
    xs = x.reshape(s, d)
    for layer in range(depth):
        i = layer // 2
        if layer % 2 == 0:
            z = norm_matmul(xs, norm_mix_g[layer], ev_w_in, i, tm=t["tm"], tn=t["tn"])
            a_out = windowed_attention(z, ev_sink[i], bias)
            b_out = fourier_mix(z, Q_DIM + 2 * KV_DIM)
            xs = out_proj(xs, [a_out, b_out], ev_w_out, i, tm=t["tm"], tn=t["tn"])
        else:
            z = norm_matmul(xs, norm_mix_g[layer], od_w_in, i, tm=t["tm"], tn=t["tn"])
            mix = conv_pool(z, od_conv_w[i], od_pool_w[i], od_pool_scale[i], tt=t["tt"])
            xs = out_proj(xs, [mix], od_w_out, i, tm=t["tm"], tn=t["tn"])
        xs = ec_moe_block(xs, norm_ffn_g[layer], moe_w_router[layer], moe_w_gate, moe_w_up, moe_w_down, layer,
                          final_norm_g, final=(layer == depth - 1), tm=t["tr"], th=t["th"], tc=t["tc"])
    return xs.reshape(b, s, d)
```

```python
import functools
import math

import numpy as np
import jax
import jax.numpy as jnp
from jax import lax
from jax.experimental import pallas as pl
from jax.experimental.pallas import tpu as pltpu

F32 = jnp.float32
BF16 = jnp.bfloat16
I32 = jnp.int32

HEAD_DIM = 128
N_Q_HEADS = 12
N_KV_HEADS = 4
GQA_GROUP = N_Q_HEADS // N_KV_HEADS
WINDOW = 128
ATTN_BLOCK = 128
N_FOURIER_GROUPS = 4
FOURIER_GROUP_DIM = 128
Q_DIM = N_Q_HEADS * HEAD_DIM
KV_DIM = N_KV_HEADS * HEAD_DIM
FOURIER_DIM = N_FOURIER_GROUPS * FOURIER_GROUP_DIM
N_REL_BUCKETS = 32
REL_MAX_DISTANCE = 128
CONV_DIM = 1024
POOL_WINDOWS = (2, 4, 8, 16)
POOL_GROUPS = len(POOL_WINDOWS)
POOL_GROUP_DIM = 256
POOL_DIM = POOL_GROUPS * POOL_GROUP_DIM
N_EXPERTS = 16
CAPACITY_FACTOR = 2
RMS_EPS = 1e-6
NEG_INF = -1e30

LANES = 128
SUBLANES = 8
BF16_ROWS = 16
VMEM_LIMIT = 56 * 1024 * 1024

HALO = BF16_ROWS
COMBINE_WINDOW = 64


def _cparams(sem):
    return pltpu.CompilerParams(dimension_semantics=sem, vmem_limit_bytes=VMEM_LIMIT)


def _dot(a, b):
    return jnp.dot(a, b, preferred_element_type=F32)


def _dot_nt(a, b):
    return lax.dot_general(a, b, (((1,), (1,)), ((), ())), preferred_element_type=F32)


def _rmsnorm(x, g):
    ms = jnp.mean(x * x, axis=-1, keepdims=True)
    return x * lax.rsqrt(ms + RMS_EPS) * g


def _pack_bf16_pairs(y):
    n = y.shape[1] // 2
    lo = lax.bitcast_convert_type(y[:, :n].astype(BF16).astype(F32), I32)
    hi = lax.bitcast_convert_type(y[:, n:].astype(BF16).astype(F32), I32)
    return lax.shift_right_logical(lo, 16) | hi


def _unpack_bf16_pairs(p):
    lo = lax.bitcast_convert_type(lax.shift_left(p, 16), F32).astype(BF16)
    hi = lax.bitcast_convert_type(p & (-65536), F32).astype(BF16)
    return lo, hi


def _store_row_tiles(ref, packed):
    rows = packed.shape[0]
    for j in range(SUBLANES):
        ref[pl.ds(j, rows, stride=SUBLANES), :] = packed[:, j * LANES:(j + 1) * LANES]


def _load_row_tiles(ref):
    rows = ref.shape[0] // SUBLANES
    return jnp.concatenate([ref[pl.ds(j, rows, stride=SUBLANES), :] for j in range(SUBLANES)], axis=1)


def _prenorm_kernel(x_ref, g_ref, o_ref):
    o_ref[...] = _rmsnorm(x_ref[...], g_ref[...]).astype(o_ref.dtype)


def prenorm(x, g, *, tm):
    s, d = x.shape
    return pl.pallas_call(
        _prenorm_kernel,
        grid=(s // tm,),
        in_specs=[pl.BlockSpec((tm, d), lambda i: (i, 0)), pl.BlockSpec((1, d), lambda i: (0, 0))],
        out_specs=pl.BlockSpec((tm, d), lambda i: (i, 0)),
        out_shape=jax.ShapeDtypeStruct((s, d), BF16),
        compiler_params=_cparams(("parallel",)),
        name="prenorm",
    )(x, g.reshape(1, d))


def _in_proj_kernel(h_ref, w_ref, o_ref, wb_ref):
    @pl.when(pl.program_id(1) == 0)
    def _():
        wb_ref[...] = w_ref[...].astype(BF16)

    o_ref[...] = _dot(h_ref[...], wb_ref[...]).astype(o_ref.dtype)


def in_proj(h, w, li, *, tm, tn):
    s, d = h.shape
    n = w.shape[2]
    return pl.pallas_call(
        _in_proj_kernel,
        grid=(n // tn, s // tm),
        in_specs=[
            pl.BlockSpec((tm, d), lambda j, i: (i, 0)),
            pl.BlockSpec((None, d, tn), lambda j, i: (li, 0, j)),
        ],
        out_specs=pl.BlockSpec((tm, tn), lambda j, i: (i, j)),
        out_shape=jax.ShapeDtypeStruct((s, n), BF16),
        scratch_shapes=[pltpu.VMEM((d, tn), BF16)],
        compiler_params=_cparams(("parallel", "arbitrary")),
        name="in_proj",
    )(h, w)


def _out_proj_kernel(*refs, widths):
    nparts = len(widths)
    x_ref = refs[0]
    p_refs = refs[1:1 + nparts]
    w_ref, o_ref, wb_ref = refs[1 + nparts:]

    @pl.when(pl.program_id(1) == 0)
    def _():
        wb_ref[...] = w_ref[...].astype(BF16)

    acc = x_ref[...]
    off = 0
    for p_ref, width in zip(p_refs, widths):
        acc = acc + _dot(p_ref[...].astype(BF16), wb_ref[off:off + width, :])
        off += width
    o_ref[...] = acc


def out_proj(x, parts, w, li, *, tm, tn):
    s, d = x.shape
    k = w.shape[1]
    widths = tuple(p.shape[1] for p in parts)
    assert sum(widths) == k
    in_specs = [pl.BlockSpec((tm, tn), lambda j, i: (i, j))]
    in_specs += [pl.BlockSpec((tm, wd), lambda j, i: (i, 0)) for wd in widths]
    in_specs += [pl.BlockSpec((None, k, tn), lambda j, i: (li, 0, j))]
    return pl.pallas_call(
        functools.partial(_out_proj_kernel, widths=widths),
        grid=(d // tn, s // tm),
        in_specs=in_specs,
        out_specs=pl.BlockSpec((tm, tn), lambda j, i: (i, j)),
        out_shape=jax.ShapeDtypeStruct((s, d), F32),
        scratch_shapes=[pltpu.VMEM((k, tn), BF16)],
        compiler_params=_cparams(("parallel", "arbitrary")),
        name="out_proj",
    )(x, *parts, w)


def _split_bf16(v):
    hi = v.astype(BF16)
    return hi, (v - hi.astype(F32)).astype(BF16)


def _router_kernel(x_ref, g_ref, wr_ref, h_ref, aff_ref):
    h = _rmsnorm(x_ref[...], g_ref[...])
    _store_row_tiles(h_ref, _pack_bf16_pairs(h))
    h_hi, h_lo = _split_bf16(h)
    w_hi, w_lo = _split_bf16(wr_ref[...])
    logits = _dot(h_hi, w_hi) + (_dot(h_hi, w_lo) + _dot(h_lo, w_hi))
    m = jnp.max(logits, axis=1, keepdims=True)
    p = jnp.exp(logits - m)
    aff_ref[...] = p / jnp.sum(p, axis=1, keepdims=True)


def router(x, g, w_router, *, tm):
    s, d = x.shape
    assert d == 2 * SUBLANES * LANES
    e = w_router.shape[1]
    return pl.pallas_call(
        _router_kernel,
        grid=(s // tm,),
        in_specs=[
            pl.BlockSpec((tm, d), lambda i: (i, 0)),
            pl.BlockSpec((1, d), lambda i: (0, 0)),
            pl.BlockSpec((d, e), lambda i: (0, 0)),
        ],
        out_specs=[pl.BlockSpec((tm * SUBLANES, LANES), lambda i: (i, 0)), pl.BlockSpec((tm, e), lambda i: (i, 0))],
        out_shape=[jax.ShapeDtypeStruct((s * SUBLANES, LANES), I32), jax.ShapeDtypeStruct((s, e), F32)],
        compiler_params=_cparams(("parallel",)),
        name="router",
    )(x, g.reshape(1, d), w_router)


def _t5_bucket(rel):
    nb = N_REL_BUCKETS // 2
    max_exact = nb // 2
    ret = (rel > 0).astype(jnp.int32) * nb
    n = jnp.abs(rel)
    nf = jnp.maximum(n, 1).astype(jnp.float32)
    large = max_exact + (jnp.log(nf / max_exact) / math.log(REL_MAX_DISTANCE / max_exact)
                         * (nb - max_exact)).astype(jnp.int32)
    large = jnp.minimum(large, nb - 1)
    return ret + jnp.where(n < max_exact, n, large)


def _bias_kernel(tab_ref, bucket_ref, o_ref):
    h = pl.program_id(0)
    bucket = bucket_ref[...]
    acc = jnp.zeros(bucket.shape, F32)
    for b in range(N_REL_BUCKETS):
        acc = jnp.where(bucket == b, tab_ref[b * N_Q_HEADS + h], acc)
    i = lax.broadcasted_iota(I32, bucket.shape, 0)
    j = lax.broadcasted_iota(I32, bucket.shape, 1)
    valid = jnp.abs(j - WINDOW - i) <= WINDOW
    o_ref[0] = jnp.where(valid, acc, NEG_INF)


def window_bias(rel_bias):
    i = jnp.arange(ATTN_BLOCK, dtype=jnp.int32)[:, None]
    j = jnp.arange(3 * ATTN_BLOCK, dtype=jnp.int32)[None, :]
    bucket = _t5_bucket((j - WINDOW) - i)
    return pl.pallas_call(
        _bias_kernel,
        grid_spec=pltpu.PrefetchScalarGridSpec(
            num_scalar_prefetch=1,
            grid=(N_Q_HEADS,),
            in_specs=[pl.BlockSpec((ATTN_BLOCK, 3 * ATTN_BLOCK), lambda h, t: (0, 0))],
            out_specs=pl.BlockSpec((1, ATTN_BLOCK, 3 * ATTN_BLOCK), lambda h, t: (h, 0, 0)),
        ),
        out_shape=jax.ShapeDtypeStruct((N_Q_HEADS, ATTN_BLOCK, 3 * ATTN_BLOCK), F32),
        compiler_params=_cparams(("arbitrary",)),
        name="window_bias",
    )(rel_bias.reshape(-1), bucket)


def _attn_kernel(sink_ref, q_ref, kp_ref, kc_ref, kn_ref, vp_ref, vc_ref, vn_ref, bias_ref, o_ref, *, nb):
    n = pl.program_id(0)
    blk = ATTN_BLOCK
    rows = GQA_GROUP * blk
    col = lax.broadcasted_iota(I32, (rows, 3 * blk), 1)
    edge_ok = ((col >= blk) | (n > 0)) & ((col < 2 * blk) | (n < nb - 1))
    row = lax.broadcasted_iota(I32, (rows, 1), 0)
    scale = HEAD_DIM ** -0.5
    for kv in range(N_KV_HEADS):
        cs = slice(kv * HEAD_DIM, (kv + 1) * HEAD_DIM)
        k = jnp.concatenate([kp_ref[:, cs], kc_ref[:, cs], kn_ref[:, cs]], axis=0)
        v = jnp.concatenate([vp_ref[:, cs], vc_ref[:, cs], vn_ref[:, cs]], axis=0)
        h0 = kv * GQA_GROUP
        q = jnp.concatenate(
            [q_ref[:, (h0 + g) * HEAD_DIM:(h0 + g + 1) * HEAD_DIM] for g in range(GQA_GROUP)], axis=0)
        s = _dot_nt(q, k) * scale + bias_ref[kv]
        s = jnp.where(edge_ok, s, NEG_INF)
        sk = jnp.full((rows, 1), sink_ref[h0 + GQA_GROUP - 1], F32)
        for g in range(GQA_GROUP - 2, -1, -1):
            sk = jnp.where(row < (g + 1) * blk, sink_ref[h0 + g], sk)
        m = jnp.maximum(jnp.max(s, axis=-1, keepdims=True), sk)
        p = jnp.exp(s - m)
        denom = jnp.sum(p, axis=-1, keepdims=True) + jnp.exp(sk - m)
        o = _dot(p.astype(BF16), v) / denom
        for g in range(GQA_GROUP):
            o_ref[:, (h0 + g) * HEAD_DIM:(h0 + g + 1) * HEAD_DIM] = o[g * blk:(g + 1) * blk].astype(o_ref.dtype)


def windowed_attention(z, sink, bias):
    s = z.shape[0]
    blk = ATTN_BLOCK
    nb = s // blk
    kcol = Q_DIM // KV_DIM
    vcol = kcol + 1

    def prev(n, t):
        return jnp.maximum(n - 1, 0)

    def nxt(n, t):
        return jnp.minimum(n + 1, nb - 1)

    in_specs = [
        pl.BlockSpec((blk, Q_DIM), lambda n, t: (n, 0)),
        pl.BlockSpec((blk, KV_DIM), lambda n, t: (prev(n, t), kcol)),
        pl.BlockSpec((blk, KV_DIM), lambda n, t: (n, kcol)),
        pl.BlockSpec((blk, KV_DIM), lambda n, t: (nxt(n, t), kcol)),
        pl.BlockSpec((blk, KV_DIM), lambda n, t: (prev(n, t), vcol)),
        pl.BlockSpec((blk, KV_DIM), lambda n, t: (n, vcol)),
        pl.BlockSpec((blk, KV_DIM), lambda n, t: (nxt(n, t), vcol)),
        pl.BlockSpec((N_KV_HEADS, GQA_GROUP * blk, 3 * blk), lambda n, t: (0, 0, 0)),
    ]
    return pl.pallas_call(
        functools.partial(_attn_kernel, nb=nb),
        grid_spec=pltpu.PrefetchScalarGridSpec(
            num_scalar_prefetch=1,
            grid=(nb,),
            in_specs=in_specs,
            out_specs=pl.BlockSpec((blk, Q_DIM), lambda n, t: (n, 0)),
        ),
        out_shape=jax.ShapeDtypeStruct((s, Q_DIM), BF16),
        compiler_params=_cparams(("arbitrary",)),
        name="windowed_attention",
    )(sink, z, z, z, z, z, z, z, bias.reshape(N_KV_HEADS, GQA_GROUP * blk, 3 * blk))


def _fourier_tables(s):
    n2 = LANES
    n1 = s // n2
    k1 = np.arange(n1)[None, :, None]
    s1 = np.arange(n1)[None, None, :]
    s2 = np.arange(n2)[:, None, None]
    ang = 2.0 * np.pi * ((k1 * (s2 + n2 * s1)) % s) / s
    ma = np.concatenate([np.cos(ang), -np.sin(ang)], axis=1)
    a = 2.0 * np.pi * ((np.arange(n2)[:, None] * np.arange(n2)[None, :]) % n2) / n2
    c, sn = np.cos(a), np.sin(a)
    mc = np.block([[c, sn], [-sn, c]])
    scale = 1.0 / math.sqrt(s * FOURIER_GROUP_DIM)
    md = np.concatenate([c, sn], axis=0) * scale
    return (jnp.asarray(ma, BF16), jnp.asarray(mc, BF16), jnp.asarray(md, BF16))


def _fourier_kernel(x_ref, ma_ref, mc_ref, md_ref, o_ref, xf, yr, yi, *, n1):
    n2 = LANES
    xf[...] = x_ref[...].astype(F32)

    def stage_a(s2, carry):
        xs = xf[:, s2, :].astype(BF16)
        y = _dot(ma_ref[s2], xs)
        yr[:, s2, :] = y[:n1]
        yi[:, s2, :] = y[n1:]
        return carry

    lax.fori_loop(0, n2, stage_a, 0, unroll=8)

    def stage_c(k1, carry):
        y = jnp.concatenate([yr[k1], yi[k1]], axis=0).astype(BF16)
        xc = _dot(mc_ref[...], y)
        xx = jnp.concatenate([xc[:n2], xc[n2:]], axis=1).astype(BF16)
        o_ref[:, k1, :] = _dot(xx, md_ref[...])
        return carry

    lax.fori_loop(0, n1, stage_c, 0, unroll=4)


def fourier_mix(z, col0):
    s, w = z.shape
    n2 = LANES
    n1 = s // n2
    c = FOURIER_GROUP_DIM
    ma, mc, md = _fourier_tables(s)
    cb0 = col0 // c
    out = pl.pallas_call(
        functools.partial(_fourier_kernel, n1=n1),
        grid=(N_FOURIER_GROUPS,),
        in_specs=[
            pl.BlockSpec((n1, n2, c), lambda g: (0, 0, cb0 + g)),
            pl.BlockSpec((n2, 2 * n1, n1), lambda g: (0, 0, 0)),
            pl.BlockSpec((2 * n2, 2 * n2), lambda g: (0, 0)),
            pl.BlockSpec((2 * c, c), lambda g: (0, 0)),
        ],
        out_specs=pl.BlockSpec((n2, n1, c), lambda g: (0, 0, g)),
        out_shape=jax.ShapeDtypeStruct((n2, n1, FOURIER_DIM), F32),
        scratch_shapes=[pltpu.VMEM((n1, n2, c), F32)] * 3,
        compiler_params=_cparams(("arbitrary",)),
        name="fourier_mix",
    )(z.reshape(n1, n2, w), ma, mc, md)
    return out.reshape(s, FOURIER_DIM)


def _convpool_kernel(zp_ref, zc_ref, zn_ref, cw_ref, pw_ref, ps_ref, o_ref, *, seq, tt):
    i = pl.program_id(0)
    ext_rows = tt + 2 * HALO
    grow = i * tt - HALO + lax.broadcasted_iota(I32, (ext_rows, 1), 0)
    row_ok = (grow >= 0) & (grow < seq)

    def ext(c0, c1):
        e = jnp.concatenate([zp_ref[:, c0:c1], zc_ref[:, c0:c1], zn_ref[:, c0:c1]], axis=0).astype(F32)
        return jnp.where(row_ok, e, 0.0)

    def shifted(e, d):
        return e[HALO + d:HALO + d + tt]

    prod = ext(CONV_DIM, 2 * CONV_DIM) * ext(2 * CONV_DIM, 3 * CONV_DIM)
    cw = cw_ref[...]
    conv = shifted(prod, -1) * cw[0:1] + shifted(prod, 0) * cw[1:2] + shifted(prod, 1) * cw[2:3]
    o_ref[:, :CONV_DIM] = (zc_ref[:, :CONV_DIM].astype(F32) * conv).astype(o_ref.dtype)

    t = grow[HALO:HALO + tt]
    for g, win in enumerate(POOL_WINDOWS):
        lo = win // 2
        hi = win - 1 - lo
        c0 = 3 * CONV_DIM + g * POOL_GROUP_DIM
        e = ext(c0, c0 + POOL_GROUP_DIM)
        total = shifted(e, -lo)
        for d in range(-lo + 1, hi + 1):
            total = total + shifted(e, d)
        count = (jnp.minimum(t + hi, seq - 1) - jnp.maximum(t - lo, 0) + 1).astype(F32)
        pooled = total / count - shifted(e, 0)
        dg = _dot(pooled.astype(BF16), pw_ref[g].astype(BF16)) * ps_ref[:, g * POOL_GROUP_DIM:(g + 1) * POOL_GROUP_DIM]
        o_ref[:, CONV_DIM + g * POOL_GROUP_DIM:CONV_DIM + (g + 1) * POOL_GROUP_DIM] = dg.astype(o_ref.dtype)


def conv_pool(z, conv_w, pool_w, pool_scale, *, tt):
    s, w = z.shape
    nh = tt // HALO
    last_h = s // HALO - 1
    return pl.pallas_call(
        functools.partial(_convpool_kernel, seq=s, tt=tt),
        grid=(s // tt,),
        in_specs=[
            pl.BlockSpec((HALO, w), lambda i: (jnp.maximum(i * nh - 1, 0), 0)),
            pl.BlockSpec((tt, w), lambda i: (i, 0)),
            pl.BlockSpec((HALO, w), lambda i: (jnp.minimum((i + 1) * nh, last_h), 0)),
            pl.BlockSpec((3, CONV_DIM), lambda i: (0, 0)),
            pl.BlockSpec((POOL_GROUPS, POOL_GROUP_DIM, POOL_GROUP_DIM), lambda i: (0, 0, 0)),
            pl.BlockSpec((1, POOL_DIM), lambda i: (0, 0)),
        ],
        out_specs=pl.BlockSpec((tt, CONV_DIM + POOL_DIM), lambda i: (i, 0)),
        out_shape=jax.ShapeDtypeStruct((s, CONV_DIM + POOL_DIM), BF16),
        compiler_params=_cparams(("arbitrary",)),
        name="conv_pool",
    )(z, z, z, conv_w, pool_w, pool_scale.reshape(1, POOL_DIM))


def _excl_cumsum_lanes(m, upper):
    r, s = m.shape
    off = jnp.zeros((r, 1), F32)
    pieces = []
    for c in range(s // LANES):
        mc = m[:, c * LANES:(c + 1) * LANES]
        pieces.append(_dot(mc.astype(BF16), upper) + off)
        off = off + jnp.sum(mc, axis=1, keepdims=True)
    return jnp.concatenate(pieces, axis=1)


def _select_kernel(aff_ref, upper_ref, pos_ref, sel_ref, v_ref, *, cap):
    a = aff_ref[...]
    e, s = a.shape
    bits = pltpu.bitcast(a, I32)
    thr = jnp.zeros((e, 1), I32)
    for bit in range(30, -1, -1):
        cand = thr | (1 << bit)
        cnt = jnp.sum((bits >= cand).astype(F32), axis=1, keepdims=True)
        thr = jnp.where(cnt >= cap, cand, thr)
    gt = bits > thr
    eq = (bits == thr).astype(F32)
    need = cap - jnp.sum(gt.astype(F32), axis=1, keepdims=True)
    upper = upper_ref[...]
    eq_rank = _excl_cumsum_lanes(eq, upper)
    sel = jnp.where(gt | ((eq > 0.5) & (eq_rank < need)), 1.0, 0.0)
    pos = _excl_cumsum_lanes(sel, upper)
    pos_ref[...] = pos
    sel_ref[...] = sel
    tok = lax.broadcasted_iota(I32, (1, s), 1).astype(F32)
    tok_hi = jnp.floor(tok * (1.0 / LANES))
    tok_lo = tok - tok_hi * LANES
    g1 = a.astype(BF16).astype(F32)
    g2 = (a - g1).astype(BF16).astype(F32)
    g3 = a - g1 - g2
    zero = jnp.zeros((SUBLANES - 5, s), F32)
    for x in range(e):
        v_ref[x] = jnp.concatenate([tok_hi, tok_lo, g1[x:x + 1], g2[x:x + 1], g3[x:x + 1], zero], axis=0)


def select_tokens(aff, cap):
    e, s = aff.shape
    upper = jnp.asarray(np.triu(np.ones((LANES, LANES), np.float32), k=1), BF16)
    full2 = lambda shp: pl.BlockSpec(shp, lambda i: (0,) * len(shp))
    return pl.pallas_call(
        functools.partial(_select_kernel, cap=cap),
        grid=(1,),
        in_specs=[full2((e, s)), full2((LANES, LANES))],
        out_specs=[full2((e, s)), full2((e, s)), full2((e, SUBLANES, s))],
        out_shape=[
            jax.ShapeDtypeStruct((e, s), F32),
            jax.ShapeDtypeStruct((e, s), F32),
            jax.ShapeDtypeStruct((e, SUBLANES, s), F32),
        ],
        compiler_params=_cparams(("arbitrary",)),
        name="select_tokens",
    )(aff, upper)


def _lists_kernel(off_ref, pos_ref, sel_ref, v_ref, o_ref, acc, *, nchunk, ntile):
    e = pl.program_id(0)
    acc[...] = jnp.zeros(acc.shape, F32)
    slot = lax.broadcasted_iota(I32, (LANES, LANES), 0).astype(F32)

    def body(c, carry):
        j0 = jnp.minimum(off_ref[e * nchunk + c] // LANES, ntile - 1)
        st = pl.multiple_of(c * LANES, LANES)
        p = pos_ref[0, :, pl.ds(st, LANES)]
        chosen = sel_ref[0, :, pl.ds(st, LANES)] > 0.5
        v = v_ref[0, :, pl.ds(st, LANES)].astype(BF16)
        for d in range(2):
            base = ((j0 + d) * LANES).astype(F32)
            onehot = jnp.where(((p - base) == slot) & chosen, 1.0, 0.0).astype(BF16)
            acc[j0 + d] += _dot_nt(v, onehot)
        return carry

    lax.fori_loop(0, nchunk, body, 0, unroll=8)
    for j in range(ntile):
        o_ref[0, :, j * LANES:(j + 1) * LANES] = acc[j]


def build_lists(pos, sel, vals, cap):
    e, s = pos.shape
    nchunk = s // LANES
    ntile = cap // LANES
    off = pos[:, ::LANES].astype(I32).reshape(-1)
    row = lambda nrow: pl.BlockSpec((1, nrow, s), lambda x, t: (x, 0, 0))
    lists = pl.pallas_call(
        functools.partial(_lists_kernel, nchunk=nchunk, ntile=ntile),
        grid_spec=pltpu.PrefetchScalarGridSpec(
            num_scalar_prefetch=1,
            grid=(e,),
            in_specs=[row(1), row(1), row(SUBLANES)],
            out_specs=pl.BlockSpec((1, SUBLANES, cap), lambda x, t: (x, 0, 0)),
            scratch_shapes=[pltpu.VMEM((ntile + 1, SUBLANES, LANES), F32)],
        ),
        out_shape=jax.ShapeDtypeStruct((e, SUBLANES, cap), F32),
        compiler_params=_cparams(("arbitrary",)),
        name="build_lists",
    )(off, pos.reshape(e, 1, s), sel.reshape(e, 1, s), vals)
    idx = (lists[:, 0] * LANES + lists[:, 1]).astype(I32)
    gate = (lists[:, 2] + lists[:, 3] + lists[:, 4])
    return idx, gate


def _ffn_kernel(idx_ref, h_hbm, gate_ref, wg_ref, wu_ref, wd_ref, y_ref, xbuf, xb, acc, sem_g, *, cap, nh, ne):
    e = pl.program_id(0)
    hs = pl.program_id(1)
    slot = e % 2
    rps = cap // nh

    def gather_rows(expert, first, count, s):
        def issue(k, carry):
            r = first + k
            t = idx_ref[expert * cap + r]
            pltpu.make_async_copy(h_hbm.at[pl.ds(pl.multiple_of(t * SUBLANES, SUBLANES), SUBLANES)],
                                  xbuf.at[s, pl.ds(pl.multiple_of(r * SUBLANES, SUBLANES), SUBLANES)],
                                  sem_g.at[s]).start()
            return carry

        lax.fori_loop(0, count, issue, 0, unroll=8)

    @pl.when((e == 0) & (hs == 0))
    def _():
        gather_rows(0, 0, cap, 0)

    @pl.when(hs == 0)
    def _():
        pltpu.make_async_copy(h_hbm.at[pl.ds(0, cap * SUBLANES)], xbuf.at[slot], sem_g.at[slot]).wait()
        lo, hi = _unpack_bf16_pairs(_load_row_tiles(xbuf.at[slot]))
        xb[...] = jnp.concatenate([lo, hi], axis=1)
        acc[...] = jnp.zeros(acc.shape, F32)

    @pl.when(e + 1 < ne)
    def _():
        gather_rows(e + 1, hs * rps, rps, 1 - slot)

    th = wg_ref.shape[1]
    w_gu = jnp.concatenate([wg_ref[...].astype(BF16), wu_ref[...].astype(BF16)], axis=1)
    gu = _dot(xb[...], w_gu)
    gp = gu[:, :th]
    hid = (gp * (1.0 / (1.0 + jnp.exp(-gp))) * gu[:, th:]).astype(BF16)
    acc[...] += _dot(hid, wd_ref[...].astype(BF16))

    @pl.when(hs == nh - 1)
    def _():
        _store_row_tiles(y_ref, _pack_bf16_pairs(acc[...] * gate_ref[0]))


def expert_ffn(h, idx, gate, w_gate, w_up, w_down, li, *, th):
    ne, cap = idx.shape
    d = w_gate.shape[2]
    hidden = w_gate.shape[3]
    nh = hidden // th
    assert d == 2 * SUBLANES * LANES
    return pl.pallas_call(
        functools.partial(_ffn_kernel, cap=cap, nh=nh, ne=ne),
        grid_spec=pltpu.PrefetchScalarGridSpec(
            num_scalar_prefetch=1,
            grid=(ne, nh),
            in_specs=[
                pl.BlockSpec(memory_space=pl.ANY),
                pl.BlockSpec((1, cap, 1), lambda x, j, a: (x, 0, 0)),
                pl.BlockSpec((None, None, d, th), lambda x, j, a: (li, x, 0, j)),
                pl.BlockSpec((None, None, d, th), lambda x, j, a: (li, x, 0, j)),
                pl.BlockSpec((None, None, th, d), lambda x, j, a: (li, x, j, 0)),
            ],
            out_specs=pl.BlockSpec((cap * SUBLANES, LANES), lambda x, j, a: (x, 0),
                                   pipeline_mode=pl.Buffered(1)),
            scratch_shapes=[
                pltpu.VMEM((2, cap * SUBLANES, LANES), I32),
                pltpu.VMEM((cap, d), BF16),
                pltpu.VMEM((cap, d), F32),
                pltpu.SemaphoreType.DMA((2,)),
            ],
        ),
        out_shape=jax.ShapeDtypeStruct((ne * cap * SUBLANES, LANES), I32),
        compiler_params=_cparams(("arbitrary", "arbitrary")),
        name="expert_ffn",
    )(idx.reshape(-1), h, gate.reshape(ne, cap, 1), w_gate, w_up, w_down)


def _combine_kernel(off_ref, x_ref, pos_ref, sel_ref, g_ref, y_hbm, *rest, ne, cap, nchunk, final):
    nout = 1 if final else 2
    o_refs = rest[:nout]
    buf, obuf, acc, sem, osem = rest[nout:]
    c = pl.program_id(0)
    slot = c % 2
    win = COMBINE_WINDOW
    nrows = ne * cap
    half = acc.shape[1] // 2

    def chunk_off(e, cc):
        return off_ref[e * (nchunk + 1) + cc]

    def win_row(e, cc, p):
        return jnp.minimum(e * cap + chunk_off(e, cc) + p * win, nrows - win)

    def tiles(row, count):
        return pl.ds(pl.multiple_of(row * SUBLANES, SUBLANES), count * SUBLANES)

    def fetch(e, cc, s):
        return pltpu.make_async_copy(y_hbm.at[tiles(win_row(e, cc, 0), win)], buf.at[s, tiles(e * win, win)],
                                     sem.at[s])

    @pl.when(c == 0)
    def _():
        for e in range(ne):
            fetch(e, 0, 0).start()

    @pl.when(c + 1 < nchunk)
    def _():
        for e in range(ne):
            fetch(e, c + 1, 1 - slot).start()

    for e in range(ne):
        fetch(e, c, slot).wait()

    pos = pos_ref[...]
    sel = sel_ref[...]

    def membership(e, p, w):
        local = pos[:, e:e + 1] - chunk_off(e, c).astype(F32)
        col = pos[:, e:e + 1] + (e * cap - win_row(e, c, p)).astype(F32)
        ok = (sel[:, e:e + 1] > 0.5) & (local >= p * win) & (local < (p + 1) * win)
        return jnp.where((col == w) & ok, 1.0, 0.0)

    lane = lax.broadcasted_iota(I32, (1, LANES), 1)
    first = lane < win
    w = jnp.where(first, lane, lane - win).astype(F32)
    member = jnp.concatenate(
        [jnp.where(first, membership(2 * q, 0, w), membership(2 * q + 1, 0, w)).astype(BF16)
         for q in range(ne // 2)], axis=1)
    a, b = _unpack_bf16_pairs(_load_row_tiles(buf.at[slot]))
    acc[:, :half] = _dot(member, a)
    acc[:, half:] = _dot(member, b)

    for e in range(ne):
        npieces = (chunk_off(e, c + 1) - chunk_off(e, c) + win - 1) // win

        def extra(p, carry, e=e):
            cp = pltpu.make_async_copy(y_hbm.at[tiles(win_row(e, c, p), win)], obuf, osem)
            cp.start()
            cp.wait()
            m = membership(e, p, lax.broadcasted_iota(I32, (1, win), 1).astype(F32)).astype(BF16)
            oa, ob = _unpack_bf16_pairs(_load_row_tiles(obuf))
            acc[:, :half] += _dot(m, oa)
            acc[:, half:] += _dot(m, ob)
            return carry

        lax.fori_loop(1, npieces, extra, 0)

    y = x_ref[...] + acc[...]
    if final:
        o_refs[0][...] = _rmsnorm(y, g_ref[...])
    else:
        o_refs[0][...] = y
        o_refs[1][...] = _rmsnorm(y, g_ref[...]).astype(BF16)


def combine(x, y_rows, pos, sel, cap, g, *, tc, final):
    s, d = x.shape
    ne = pos.shape[0]
    nchunk = s // tc
    assert 2 * COMBINE_WINDOW == LANES and ne % 2 == 0 and ne * cap >= COMBINE_WINDOW
    off = jnp.concatenate([pos[:, ::tc].astype(I32), jnp.full((ne, 1), cap, I32)], axis=1).reshape(-1)
    row_block = pl.BlockSpec((tc, d), lambda c, t: (c, 0))
    if final:
        out_specs, out_shape = row_block, jax.ShapeDtypeStruct((s, d), F32)
    else:
        out_specs = [row_block, row_block]
        out_shape = [jax.ShapeDtypeStruct((s, d), F32), jax.ShapeDtypeStruct((s, d), BF16)]
    return pl.pallas_call(
        functools.partial(_combine_kernel, ne=ne, cap=cap, nchunk=nchunk, final=final),
        grid_spec=pltpu.PrefetchScalarGridSpec(
            num_scalar_prefetch=1,
            grid=(nchunk,),
            in_specs=[
                pl.BlockSpec((tc, d), lambda c, t: (c, 0)),
                pl.BlockSpec((tc, ne), lambda c, t: (c, 0)),
                pl.BlockSpec((tc, ne), lambda c, t: (c, 0)),
                pl.BlockSpec((1, d), lambda c, t: (0, 0)),
                pl.BlockSpec(memory_space=pl.ANY),
            ],
            out_specs=out_specs,
            scratch_shapes=[
                pltpu.VMEM((2, ne * COMBINE_WINDOW * SUBLANES, LANES), I32),
                pltpu.VMEM((COMBINE_WINDOW * SUBLANES, LANES), I32),
                pltpu.VMEM((tc, d), F32),
                pltpu.SemaphoreType.DMA((2,)),
                pltpu.SemaphoreType.DMA(()),
            ],
        ),
        out_shape=out_shape,
        compiler_params=_cparams(("arbitrary",)),
        name="combine",
    )(off, x, pos.T, sel.T, g.reshape(1, d), y_rows)


def ec_moe_block(x1, h, aff, w_gate, w_up, w_down, li, g_next, *, final, th, tc):
    s = x1.shape[0]
    cap = CAPACITY_FACTOR * s // N_EXPERTS
    pos, sel, vals = select_tokens(aff, cap)
    idx, gate = build_lists(pos, sel, vals, cap)
    y_rows = expert_ffn(h, idx, gate, w_gate, w_up, w_down, li, th=th)
    return combine(x1, y_rows, pos, sel, cap, g_next, tc=tc, final=final)


def _tiles(s):
    return dict(tm=min(1024, s), tn=1024, to=min(512, s), tt=min(512, s), th=256, tc=min(256, s))


def kernel(x, rel_bias, norm_mix_g, norm_ffn_g, final_norm_g, ev_w_in, ev_sink, ev_w_out, od_w_in, od_conv_w,
           od_pool_w, od_pool_scale, od_w_out, moe_w_router, moe_w_gate, moe_w_up, moe_w_down):
    b, s, d = x.shape
    assert b == 1
    t = _tiles(s)
    depth = norm_mix_g.shape[0]
    bias = window_bias(rel_bias)
    xs = x.reshape(s, d)
    hn = prenorm(xs, norm_mix_g[0], tm=t["tm"])
    for layer in range(depth):
        i = layer // 2
        if layer % 2 == 0:
            z = in_proj(hn, ev_w_in, i, tm=t["tm"], tn=t["tn"])
            parts = [windowed_attention(z, ev_sink[i], bias), fourier_mix(z, Q_DIM + 2 * KV_DIM)]
            w_out = ev_w_out
        else:
            z = in_proj(hn, od_w_in, i, tm=t["tm"], tn=t["tn"])
            parts = [conv_pool(z, od_conv_w[i], od_pool_w[i], od_pool_scale[i], tt=t["tt"])]
            w_out = od_w_out
        x1 = out_proj(xs, parts, w_out, i, tm=t["tm"], tn=t["tn"])
        h, aff = router(x1, norm_ffn_g[layer], moe_w_router[layer], tm=t["to"])
        final = layer == depth - 1
        g_next = final_norm_g if final else norm_mix_g[layer + 1]
        res = ec_moe_block(x1, h, aff.T, moe_w_gate, moe_w_up, moe_w_down, layer, g_next,
                           final=final, th=t["th"], tc=t["tc"])
        if final:
            xs = res
        else:
            xs, hn = res
    return xs.reshape(b, s, d)
```

```python
import functools
import math

import numpy as np
import jax
import jax.numpy as jnp
from jax import lax
from jax.experimental import pallas as pl
from jax.experimental.pallas import tpu as pltpu

F32 = jnp.float32
BF16 = jnp.bfloat16
I32 = jnp.int32

HEAD_DIM = 128
N_Q_HEADS = 12
N_KV_HEADS = 4
GQA_GROUP = N_Q_HEADS // N_KV_HEADS
WINDOW = 128
ATTN_BLOCK = 128
N_FOURIER_GROUPS = 4
FOURIER_GROUP_DIM = 128
Q_DIM = N_Q_HEADS * HEAD_DIM
KV_DIM = N_KV_HEADS * HEAD_DIM
FOURIER_DIM = N_FOURIER_GROUPS * FOURIER_GROUP_DIM
N_REL_BUCKETS = 32
REL_MAX_DISTANCE = 128
CONV_DIM = 1024
POOL_WINDOWS = (2, 4, 8, 16)
POOL_GROUPS = len(POOL_WINDOWS)
POOL_GROUP_DIM = 256
POOL_DIM = POOL_GROUPS * POOL_GROUP_DIM
N_EXPERTS = 16
CAPACITY_FACTOR = 2
RMS_EPS = 1e-6
NEG_INF = -1e30
LOG2E = math.log2(math.e)

LANES = 128
SUBLANES = 8
BF16_ROWS = 16
VMEM_LIMIT = 56 * 1024 * 1024

HALO = BF16_ROWS
COMBINE_WINDOW = 64


def _cparams(sem):
    return pltpu.CompilerParams(dimension_semantics=sem, vmem_limit_bytes=VMEM_LIMIT)


def _dot(a, b):
    return jnp.dot(a, b, preferred_element_type=F32)


def _dot_nt(a, b):
    return lax.dot_general(a, b, (((1,), (1,)), ((), ())), preferred_element_type=F32)


def _rmsnorm(x, g):
    ms = jnp.mean(x * x, axis=-1, keepdims=True)
    return x * lax.rsqrt(ms + RMS_EPS) * g


def _pack_bf16_pairs(y):
    n = y.shape[1] // 2
    lo = lax.bitcast_convert_type(y[:, :n].astype(BF16).astype(F32), I32)
    hi = lax.bitcast_convert_type(y[:, n:].astype(BF16).astype(F32), I32)
    return lax.shift_right_logical(lo, 16) | hi


def _unpack_bf16_pairs(p):
    lo = lax.bitcast_convert_type(lax.shift_left(p, 16), F32).astype(BF16)
    hi = lax.bitcast_convert_type(p & (-65536), F32).astype(BF16)
    return lo, hi


def _store_row_tiles(ref, packed):
    rows = packed.shape[0]
    for j in range(SUBLANES):
        ref[pl.ds(j, rows, stride=SUBLANES), :] = packed[:, j * LANES:(j + 1) * LANES]


def _load_row_tiles(ref):
    rows = ref.shape[0] // SUBLANES
    return jnp.concatenate([ref[pl.ds(j, rows, stride=SUBLANES), :] for j in range(SUBLANES)], axis=1)


def _prenorm_kernel(x_ref, g_ref, o_ref):
    o_ref[...] = _rmsnorm(x_ref[...], g_ref[...]).astype(o_ref.dtype)


def prenorm(x, g, *, tm):
    s, d = x.shape
    return pl.pallas_call(
        _prenorm_kernel,
        grid=(s // tm,),
        in_specs=[pl.BlockSpec((tm, d), lambda i: (i, 0)), pl.BlockSpec((1, d), lambda i: (0, 0))],
        out_specs=pl.BlockSpec((tm, d), lambda i: (i, 0)),
        out_shape=jax.ShapeDtypeStruct((s, d), BF16),
        compiler_params=_cparams(("parallel",)),
        name="prenorm",
    )(x, g.reshape(1, d))


def _in_proj_kernel(h_ref, w_ref, o_ref, wb_ref):
    @pl.when(pl.program_id(1) == 0)
    def _():
        wb_ref[...] = w_ref[...].astype(BF16)

    o_ref[...] = _dot(h_ref[...], wb_ref[...]).astype(o_ref.dtype)


def in_proj(h, w, li, *, tm, tn):
    s, d = h.shape
    n = w.shape[2]
    return pl.pallas_call(
        _in_proj_kernel,
        grid=(n // tn, s // tm),
        in_specs=[
            pl.BlockSpec((tm, d), lambda j, i: (i, 0)),
            pl.BlockSpec((None, d, tn), lambda j, i: (li, 0, j)),
        ],
        out_specs=pl.BlockSpec((tm, tn), lambda j, i: (i, j)),
        out_shape=jax.ShapeDtypeStruct((s, n), BF16),
        scratch_shapes=[pltpu.VMEM((d, tn), BF16)],
        compiler_params=_cparams(("parallel", "arbitrary")),
        name="in_proj",
    )(h, w)


def _out_proj_kernel(*refs, widths):
    nparts = len(widths)
    x_ref = refs[0]
    p_refs = refs[1:1 + nparts]
    w_ref, o_ref, wb_ref = refs[1 + nparts:]

    @pl.when(pl.program_id(1) == 0)
    def _():
        wb_ref[...] = w_ref[...].astype(BF16)

    acc = x_ref[...]
    off = 0
    for p_ref, width in zip(p_refs, widths):
        acc = acc + _dot(p_ref[...].astype(BF16), wb_ref[off:off + width, :])
        off += width
    o_ref[...] = acc


def out_proj(x, parts, w, li, *, tm, tn):
    s, d = x.shape
    k = w.shape[1]
    widths = tuple(p.shape[1] for p in parts)
    assert sum(widths) == k
    in_specs = [pl.BlockSpec((tm, tn), lambda j, i: (i, j))]
    in_specs += [pl.BlockSpec((tm, wd), lambda j, i: (i, 0)) for wd in widths]
    in_specs += [pl.BlockSpec((None, k, tn), lambda j, i: (li, 0, j))]
    return pl.pallas_call(
        functools.partial(_out_proj_kernel, widths=widths),
        grid=(d // tn, s // tm),
        in_specs=in_specs,
        out_specs=pl.BlockSpec((tm, tn), lambda j, i: (i, j)),
        out_shape=jax.ShapeDtypeStruct((s, d), F32),
        scratch_shapes=[pltpu.VMEM((k, tn), BF16)],
        compiler_params=_cparams(("parallel", "arbitrary")),
        name="out_proj",
    )(x, *parts, w)


def _split_bf16(v):
    hi = v.astype(BF16)
    return hi, (v - hi.astype(F32)).astype(BF16)


def _router_kernel(x_ref, g_ref, wr_ref, h_ref, aff_ref):
    h = _rmsnorm(x_ref[...], g_ref[...])
    _store_row_tiles(h_ref, _pack_bf16_pairs(h))
    h_hi, h_lo = _split_bf16(h)
    w_hi, w_lo = _split_bf16(wr_ref[...])
    logits = _dot(h_hi, w_hi) + (_dot(h_hi, w_lo) + _dot(h_lo, w_hi))
    m = jnp.max(logits, axis=1, keepdims=True)
    p = jnp.exp(logits - m)
    aff_ref[...] = p / jnp.sum(p, axis=1, keepdims=True)


def router(x, g, w_router, *, tm):
    s, d = x.shape
    assert d == 2 * SUBLANES * LANES
    e = w_router.shape[1]
    return pl.pallas_call(
        _router_kernel,
        grid=(s // tm,),
        in_specs=[
            pl.BlockSpec((tm, d), lambda i: (i, 0)),
            pl.BlockSpec((1, d), lambda i: (0, 0)),
            pl.BlockSpec((d, e), lambda i: (0, 0)),
        ],
        out_specs=[pl.BlockSpec((tm * SUBLANES, LANES), lambda i: (i, 0)), pl.BlockSpec((tm, e), lambda i: (i, 0))],
        out_shape=[jax.ShapeDtypeStruct((s * SUBLANES, LANES), I32), jax.ShapeDtypeStruct((s, e), F32)],
        compiler_params=_cparams(("parallel",)),
        name="router",
    )(x, g.reshape(1, d), w_router)


def _t5_bucket(rel):
    nb = N_REL_BUCKETS // 2
    max_exact = nb // 2
    ret = (rel > 0).astype(jnp.int32) * nb
    n = jnp.abs(rel)
    nf = jnp.maximum(n, 1).astype(jnp.float32)
    large = max_exact + (jnp.log(nf / max_exact) / math.log(REL_MAX_DISTANCE / max_exact)
                         * (nb - max_exact)).astype(jnp.int32)
    large = jnp.minimum(large, nb - 1)
    return ret + jnp.where(n < max_exact, n, large)


def _bias_kernel(tab_ref, bucket_ref, o_ref):
    h = pl.program_id(0)
    bucket = bucket_ref[...]
    acc = jnp.zeros(bucket.shape, F32)
    for b in range(N_REL_BUCKETS):
        acc = jnp.where(bucket == b, tab_ref[b * N_Q_HEADS + h], acc)
    i = lax.broadcasted_iota(I32, bucket.shape, 0)
    j = lax.broadcasted_iota(I32, bucket.shape, 1)
    valid = jnp.abs(j - WINDOW - i) <= WINDOW
    o_ref[0] = jnp.where(valid, acc * LOG2E, NEG_INF)


def window_bias(rel_bias):
    i = jnp.arange(ATTN_BLOCK, dtype=jnp.int32)[:, None]
    j = jnp.arange(3 * ATTN_BLOCK, dtype=jnp.int32)[None, :]
    bucket = _t5_bucket((j - WINDOW) - i)
    return pl.pallas_call(
        _bias_kernel,
        grid_spec=pltpu.PrefetchScalarGridSpec(
            num_scalar_prefetch=1,
            grid=(N_Q_HEADS,),
            in_specs=[pl.BlockSpec((ATTN_BLOCK, 3 * ATTN_BLOCK), lambda h, t: (0, 0))],
            out_specs=pl.BlockSpec((1, ATTN_BLOCK, 3 * ATTN_BLOCK), lambda h, t: (h, 0, 0)),
        ),
        out_shape=jax.ShapeDtypeStruct((N_Q_HEADS, ATTN_BLOCK, 3 * ATTN_BLOCK), F32),
        compiler_params=_cparams(("arbitrary",)),
        name="window_bias",
    )(rel_bias.reshape(-1), bucket)


def _attn_kernel(sink_ref, q_ref, kp_ref, kc_ref, kn_ref, vp_ref, vc_ref, vn_ref, bias_ref, o_ref, *, nb):
    n = pl.program_id(0)
    blk = ATTN_BLOCK
    ones = jnp.ones((3 * blk, HEAD_DIM), BF16)
    scale2 = HEAD_DIM ** -0.5 * LOG2E

    def block(at_edge):
        if at_edge:
            col = lax.broadcasted_iota(I32, (1, 3 * blk), 1)
            in_seq = ((col >= blk) | (n > 0)) & ((col < 2 * blk) | (n < nb - 1))
            edge_bias = jnp.where(in_seq, 0.0, NEG_INF)
        for kv in range(N_KV_HEADS):
            cs = slice(kv * HEAD_DIM, (kv + 1) * HEAD_DIM)
            k = jnp.concatenate([kp_ref[:, cs], kc_ref[:, cs], kn_ref[:, cs]], axis=0)
            v = jnp.concatenate([vp_ref[:, cs], vc_ref[:, cs], vn_ref[:, cs]], axis=0)
            v1 = jnp.concatenate([v, ones], axis=1)
            for g in range(GQA_GROUP):
                hq = kv * GQA_GROUP + g
                hs = slice(hq * HEAD_DIM, (hq + 1) * HEAD_DIM)
                s2 = _dot_nt(q_ref[:, hs], k) * scale2 + bias_ref[kv, g * blk:(g + 1) * blk, :]
                if at_edge:
                    s2 = s2 + edge_bias
                sk2 = sink_ref[hq] * LOG2E
                m2 = jnp.maximum(jnp.max(s2, axis=-1, keepdims=True), sk2)
                p = jnp.exp2(s2 - m2).astype(BF16)
                ov = _dot(p, v1)
                denom = ov[:, HEAD_DIM:HEAD_DIM + 1] + jnp.exp2(sk2 - m2)
                o_ref[:, hs] = (ov[:, :HEAD_DIM] / denom).astype(o_ref.dtype)

    @pl.when((n > 0) & (n < nb - 1))
    def _():
        block(False)

    @pl.when((n == 0) | (n == nb - 1))
    def _():
        block(True)


def windowed_attention(z, sink, bias):
    s = z.shape[0]
    blk = ATTN_BLOCK
    nb = s // blk
    kcol = Q_DIM // KV_DIM
    vcol = kcol + 1

    def prev(n, t):
        return jnp.maximum(n - 1, 0)

    def nxt(n, t):
        return jnp.minimum(n + 1, nb - 1)

    in_specs = [
        pl.BlockSpec((blk, Q_DIM), lambda n, t: (n, 0)),
        pl.BlockSpec((blk, KV_DIM), lambda n, t: (prev(n, t), kcol)),
        pl.BlockSpec((blk, KV_DIM), lambda n, t: (n, kcol)),
        pl.BlockSpec((blk, KV_DIM), lambda n, t: (nxt(n, t), kcol)),
        pl.BlockSpec((blk, KV_DIM), lambda n, t: (prev(n, t), vcol)),
        pl.BlockSpec((blk, KV_DIM), lambda n, t: (n, vcol)),
        pl.BlockSpec((blk, KV_DIM), lambda n, t: (nxt(n, t), vcol)),
        pl.BlockSpec((N_KV_HEADS, GQA_GROUP * blk, 3 * blk), lambda n, t: (0, 0, 0)),
    ]
    return pl.pallas_call(
        functools.partial(_attn_kernel, nb=nb),
        grid_spec=pltpu.PrefetchScalarGridSpec(
            num_scalar_prefetch=1,
            grid=(nb,),
            in_specs=in_specs,
            out_specs=pl.BlockSpec((blk, Q_DIM), lambda n, t: (n, 0)),
        ),
        out_shape=jax.ShapeDtypeStruct((s, Q_DIM), BF16),
        compiler_params=_cparams(("arbitrary",)),
        name="windowed_attention",
    )(sink, z, z, z, z, z, z, z, bias.reshape(N_KV_HEADS, GQA_GROUP * blk, 3 * blk))


def _fourier_tables(s):
    n2 = LANES
    n1 = s // n2
    k1 = np.arange(n1)[None, :, None]
    s1 = np.arange(n1)[None, None, :]
    s2 = np.arange(n2)[:, None, None]
    ang = 2.0 * np.pi * ((k1 * (s2 + n2 * s1)) % s) / s
    ma = np.concatenate([np.cos(ang), -np.sin(ang)], axis=1)
    a = 2.0 * np.pi * ((np.arange(n2)[:, None] * np.arange(n2)[None, :]) % n2) / n2
    c, sn = np.cos(a), np.sin(a)
    mc = np.block([[c, sn], [-sn, c]])
    scale = 1.0 / math.sqrt(s * FOURIER_GROUP_DIM)
    md = np.concatenate([c, sn], axis=0) * scale
    return (jnp.asarray(ma, BF16), jnp.asarray(mc, BF16), jnp.asarray(md, BF16))


FOURIER_K1_BATCH = 4


def _fourier_kernel(x_ref, ma_ref, mc_ref, md_ref, o_ref, xf, yr, yi, ot, *, n1):
    n2 = LANES
    pitch = n2 + SUBLANES
    pitch_o = n1 + SUBLANES
    kb = FOURIER_K1_BATCH
    for s1 in range(n1):
        xf[s1 * pitch:s1 * pitch + n2, :] = x_ref[s1 * n2:(s1 + 1) * n2, :].astype(F32)

    def stage_a(s2, carry):
        xs = xf[pl.ds(s2, n1, stride=pitch), :].astype(BF16)
        y = _dot(ma_ref[s2], xs)
        yr[pl.ds(s2, n1, stride=pitch), :] = y[:n1]
        yi[pl.ds(s2, n1, stride=pitch), :] = y[n1:]
        return carry

    lax.fori_loop(0, n2, stage_a, 0, unroll=8)

    def stage_c(b, carry):
        def slab(ref, j):
            return ref[pl.ds(pl.multiple_of((b * kb + j) * pitch, SUBLANES), n2), :]

        y = jnp.concatenate(
            [jnp.concatenate([slab(yr, j), slab(yi, j)], axis=0) for j in range(kb)], axis=1).astype(BF16)
        xc = _dot(mc_ref[...], y)
        xx = jnp.concatenate(
            [jnp.concatenate([xc[:n2, j * LANES:(j + 1) * LANES], xc[n2:, j * LANES:(j + 1) * LANES]], axis=1)
             for j in range(kb)], axis=0).astype(BF16)
        out = _dot(xx, md_ref[...])
        for j in range(kb):
            ot[pl.ds(b * kb + j, n2, stride=pitch_o), :] = out[j * n2:(j + 1) * n2]
        return carry

    lax.fori_loop(0, n1 // kb, stage_c, 0, unroll=2)
    for k2 in range(n2):
        o_ref[k2 * n1:(k2 + 1) * n1, :] = ot[k2 * pitch_o:k2 * pitch_o + n1, :]


def fourier_mix(z, col0):
    s, w = z.shape
    n2 = LANES
    n1 = s // n2
    assert n1 % FOURIER_K1_BATCH == 0 and n1 % SUBLANES == 0
    c = FOURIER_GROUP_DIM
    ma, mc, md = _fourier_tables(s)
    cb0 = col0 // c
    slabs = pltpu.VMEM((n1 * (n2 + SUBLANES), c), F32)
    return pl.pallas_call(
        functools.partial(_fourier_kernel, n1=n1),
        grid=(N_FOURIER_GROUPS,),
        in_specs=[
            pl.BlockSpec((s, c), lambda g: (0, cb0 + g)),
            pl.BlockSpec((n2, 2 * n1, n1), lambda g: (0, 0, 0)),
            pl.BlockSpec((2 * n2, 2 * n2), lambda g: (0, 0)),
            pl.BlockSpec((2 * c, c), lambda g: (0, 0)),
        ],
        out_specs=pl.BlockSpec((s, c), lambda g: (0, g)),
        out_shape=jax.ShapeDtypeStruct((s, FOURIER_DIM), F32),
        scratch_shapes=[slabs, slabs, slabs, pltpu.VMEM((n2 * (n1 + SUBLANES), c), F32)],
        compiler_params=_cparams(("arbitrary",)),
        name="fourier_mix",
    )(z, ma, mc, md)


def _convpool_kernel(zp_ref, zc_ref, zn_ref, cw_ref, pw_ref, ps_ref, o_ref, *, seq, tt):
    i = pl.program_id(0)
    ext_rows = tt + 2 * HALO
    grow = i * tt - HALO + lax.broadcasted_iota(I32, (ext_rows, 1), 0)
    row_ok = (grow >= 0) & (grow < seq)

    def ext(c0, c1):
        e = jnp.concatenate([zp_ref[:, c0:c1], zc_ref[:, c0:c1], zn_ref[:, c0:c1]], axis=0).astype(F32)
        return jnp.where(row_ok, e, 0.0)

    def shifted(e, d):
        return e[HALO + d:HALO + d + tt]

    prod = ext(CONV_DIM, 2 * CONV_DIM) * ext(2 * CONV_DIM, 3 * CONV_DIM)
    cw = cw_ref[...]
    conv = shifted(prod, -1) * cw[0:1] + shifted(prod, 0) * cw[1:2] + shifted(prod, 1) * cw[2:3]
    o_ref[:, :CONV_DIM] = (zc_ref[:, :CONV_DIM].astype(F32) * conv).astype(o_ref.dtype)

    t = grow[HALO:HALO + tt]
    for g, win in enumerate(POOL_WINDOWS):
        lo = win // 2
        hi = win - 1 - lo
        c0 = 3 * CONV_DIM + g * POOL_GROUP_DIM
        e = ext(c0, c0 + POOL_GROUP_DIM)
        assert win & (win - 1) == 0 and win <= HALO
        run, step = e, 1
        while step < win:
            run = run + pltpu.roll(run, ext_rows - step, axis=0)
            step *= 2
        total = shifted(run, -lo)
        count = (jnp.minimum(t + hi, seq - 1) - jnp.maximum(t - lo, 0) + 1).astype(F32)
        pooled = total / count - shifted(e, 0)
        dg = _dot(pooled.astype(BF16), pw_ref[g].astype(BF16)) * ps_ref[:, g * POOL_GROUP_DIM:(g + 1) * POOL_GROUP_DIM]
        o_ref[:, CONV_DIM + g * POOL_GROUP_DIM:CONV_DIM + (g + 1) * POOL_GROUP_DIM] = dg.astype(o_ref.dtype)


def conv_pool(z, conv_w, pool_w, pool_scale, *, tt):
    s, w = z.shape
    nh = tt // HALO
    last_h = s // HALO - 1
    return pl.pallas_call(
        functools.partial(_convpool_kernel, seq=s, tt=tt),
        grid=(s // tt,),
        in_specs=[
            pl.BlockSpec((HALO, w), lambda i: (jnp.maximum(i * nh - 1, 0), 0)),
            pl.BlockSpec((tt, w), lambda i: (i, 0)),
            pl.BlockSpec((HALO, w), lambda i: (jnp.minimum((i + 1) * nh, last_h), 0)),
            pl.BlockSpec((3, CONV_DIM), lambda i: (0, 0)),
            pl.BlockSpec((POOL_GROUPS, POOL_GROUP_DIM, POOL_GROUP_DIM), lambda i: (0, 0, 0)),
            pl.BlockSpec((1, POOL_DIM), lambda i: (0, 0)),
        ],
        out_specs=pl.BlockSpec((tt, CONV_DIM + POOL_DIM), lambda i: (i, 0)),
        out_shape=jax.ShapeDtypeStruct((s, CONV_DIM + POOL_DIM), BF16),
        compiler_params=_cparams(("arbitrary",)),
        name="conv_pool",
    )(z, z, z, conv_w, pool_w, pool_scale.reshape(1, POOL_DIM))


def _excl_cumsum_lanes(m, upper):
    r, s = m.shape
    off = jnp.zeros((r, 1), F32)
    pieces = []
    for c in range(s // LANES):
        mc = m[:, c * LANES:(c + 1) * LANES]
        pieces.append(_dot(mc.astype(BF16), upper) + off)
        off = off + jnp.sum(mc, axis=1, keepdims=True)
    return jnp.concatenate(pieces, axis=1)


def _select_kernel(aff_ref, upper_ref, pos_ref, sel_ref, v_ref, *, cap):
    a = aff_ref[...]
    e, s = a.shape
    bits = pltpu.bitcast(a, I32)
    thr = jnp.zeros((e, 1), I32)
    for bit in range(30, -1, -1):
        cand = thr | (1 << bit)
        cnt = jnp.sum((bits >= cand).astype(F32), axis=1, keepdims=True)
        thr = jnp.where(cnt >= cap, cand, thr)
    gt = bits > thr
    eq = (bits == thr).astype(F32)
    need = cap - jnp.sum(gt.astype(F32), axis=1, keepdims=True)
    upper = upper_ref[...]
    eq_rank = _excl_cumsum_lanes(eq, upper)
    sel = jnp.where(gt | ((eq > 0.5) & (eq_rank < need)), 1.0, 0.0)
    pos = _excl_cumsum_lanes(sel, upper)
    pos_ref[...] = pos
    sel_ref[...] = sel
    tok = lax.broadcasted_iota(I32, (1, s), 1).astype(F32)
    tok_hi = jnp.floor(tok * (1.0 / LANES))
    tok_lo = tok - tok_hi * LANES
    g1 = a.astype(BF16).astype(F32)
    g2 = (a - g1).astype(BF16).astype(F32)
    g3 = a - g1 - g2
    zero = jnp.zeros((SUBLANES - 5, s), F32)
    for x in range(e):
        v_ref[x] = jnp.concatenate([tok_hi, tok_lo, g1[x:x + 1], g2[x:x + 1], g3[x:x + 1], zero], axis=0)


def select_tokens(aff, cap):
    e, s = aff.shape
    upper = jnp.asarray(np.triu(np.ones((LANES, LANES), np.float32), k=1), BF16)
    full2 = lambda shp: pl.BlockSpec(shp, lambda i: (0,) * len(shp))
    return pl.pallas_call(
        functools.partial(_select_kernel, cap=cap),
        grid=(1,),
        in_specs=[full2((e, s)), full2((LANES, LANES))],
        out_specs=[full2((e, s)), full2((e, s)), full2((e, SUBLANES, s))],
        out_shape=[
            jax.ShapeDtypeStruct((e, s), F32),
            jax.ShapeDtypeStruct((e, s), F32),
            jax.ShapeDtypeStruct((e, SUBLANES, s), F32),
        ],
        compiler_params=_cparams(("arbitrary",)),
        name="select_tokens",
    )(aff, upper)


def _lists_kernel(off_ref, pos_ref, sel_ref, v_ref, o_ref, acc, *, nchunk, ntile):
    e = pl.program_id(0)
    acc[...] = jnp.zeros(acc.shape, F32)
    slot = lax.broadcasted_iota(I32, (LANES, LANES), 0).astype(F32)

    def body(c, carry):
        j0 = jnp.minimum(off_ref[e * nchunk + c] // LANES, ntile - 1)
        st = pl.multiple_of(c * LANES, LANES)
        p = pos_ref[0, :, pl.ds(st, LANES)]
        chosen = sel_ref[0, :, pl.ds(st, LANES)] > 0.5
        v = v_ref[0, :, pl.ds(st, LANES)].astype(BF16)
        for d in range(2):
            base = ((j0 + d) * LANES).astype(F32)
            onehot = jnp.where(((p - base) == slot) & chosen, 1.0, 0.0).astype(BF16)
            acc[j0 + d] += _dot_nt(v, onehot)
        return carry

    lax.fori_loop(0, nchunk, body, 0, unroll=8)
    for j in range(ntile):
        o_ref[0, :, j * LANES:(j + 1) * LANES] = acc[j]


def build_lists(pos, sel, vals, cap):
    e, s = pos.shape
    nchunk = s // LANES
    ntile = cap // LANES
    off = pos[:, ::LANES].astype(I32).reshape(-1)
    row = lambda nrow: pl.BlockSpec((1, nrow, s), lambda x, t: (x, 0, 0))
    lists = pl.pallas_call(
        functools.partial(_lists_kernel, nchunk=nchunk, ntile=ntile),
        grid_spec=pltpu.PrefetchScalarGridSpec(
            num_scalar_prefetch=1,
            grid=(e,),
            in_specs=[row(1), row(1), row(SUBLANES)],
            out_specs=pl.BlockSpec((1, SUBLANES, cap), lambda x, t: (x, 0, 0)),
            scratch_shapes=[pltpu.VMEM((ntile + 1, SUBLANES, LANES), F32)],
        ),
        out_shape=jax.ShapeDtypeStruct((e, SUBLANES, cap), F32),
        compiler_params=_cparams(("arbitrary",)),
        name="build_lists",
    )(off, pos.reshape(e, 1, s), sel.reshape(e, 1, s), vals)
    idx = (lists[:, 0] * LANES + lists[:, 1]).astype(I32)
    gate = (lists[:, 2] + lists[:, 3] + lists[:, 4])
    return idx, gate


def _ffn_kernel(idx_ref, h_hbm, gate_ref, wg_ref, wu_ref, wd_ref, y_ref, xbuf, xb, acc, sem_g, *, cap, nh, ne):
    e = pl.program_id(0)
    hs = pl.program_id(1)
    slot = e % 2
    rps = cap // nh

    def gather_row(expert, r, s):
        t = idx_ref[expert * cap + r]
        return pltpu.make_async_copy(h_hbm.at[pl.ds(pl.multiple_of(t * SUBLANES, SUBLANES), SUBLANES)],
                                     xbuf.at[s, pl.ds(pl.multiple_of(r * SUBLANES, SUBLANES), SUBLANES)],
                                     sem_g.at[s])

    def wait_gather(s):
        pltpu.make_async_copy(h_hbm.at[pl.ds(0, cap * SUBLANES)], xbuf.at[s], sem_g.at[s]).wait()

    @pl.when((e == 0) & (hs == 0))
    def _():
        def issue(r, carry):
            gather_row(0, r, 0).start()
            return carry

        lax.fori_loop(0, cap, issue, 0, unroll=8)

    @pl.when(hs == 0)
    def _():
        wait_gather(slot)
        lo, hi = _unpack_bf16_pairs(_load_row_tiles(xbuf.at[slot]))
        xb[...] = jnp.concatenate([lo, hi], axis=1)
        acc[...] = jnp.zeros(acc.shape, F32)

    th = wg_ref.shape[1]
    w_gu = jnp.concatenate([wg_ref[...].astype(BF16), wu_ref[...].astype(BF16)], axis=1)
    gu = _dot(xb[...], w_gu)
    gp = gu[:, :th]
    hid = (gp * (1.0 / (1.0 + jnp.exp(-gp))) * gu[:, th:]).astype(BF16)
    nxt = lax.rem(e + 1, ne)
    for k in range(rps):
        gather_row(nxt, hs * rps + k, 1 - slot).start()
    acc[...] += _dot(hid, wd_ref[...].astype(BF16))

    @pl.when(hs == nh - 1)
    def _():
        _store_row_tiles(y_ref, _pack_bf16_pairs(acc[...] * gate_ref[0]))

    @pl.when((e == ne - 1) & (hs == nh - 1))
    def _():
        wait_gather(1 - slot)


def expert_ffn(h, idx, gate, w_gate, w_up, w_down, li, *, th):
    ne, cap = idx.shape
    d = w_gate.shape[2]
    hidden = w_gate.shape[3]
    nh = hidden // th
    assert d == 2 * SUBLANES * LANES
    return pl.pallas_call(
        functools.partial(_ffn_kernel, cap=cap, nh=nh, ne=ne),
        grid_spec=pltpu.PrefetchScalarGridSpec(
            num_scalar_prefetch=1,
            grid=(ne, nh),
            in_specs=[
                pl.BlockSpec(memory_space=pl.ANY),
                pl.BlockSpec((1, cap, 1), lambda x, j, a: (x, 0, 0)),
                pl.BlockSpec((None, None, d, th), lambda x, j, a: (li, x, 0, j)),
                pl.BlockSpec((None, None, d, th), lambda x, j, a: (li, x, 0, j)),
                pl.BlockSpec((None, None, th, d), lambda x, j, a: (li, x, j, 0)),
            ],
            out_specs=pl.BlockSpec((cap * SUBLANES, LANES), lambda x, j, a: (x, 0),
                                   pipeline_mode=pl.Buffered(1)),
            scratch_shapes=[
                pltpu.VMEM((2, cap * SUBLANES, LANES), I32),
                pltpu.VMEM((cap, d), BF16),
                pltpu.VMEM((cap, d), F32),
                pltpu.SemaphoreType.DMA((2,)),
            ],
        ),
        out_shape=jax.ShapeDtypeStruct((ne * cap * SUBLANES, LANES), I32),
        compiler_params=_cparams(("arbitrary", "arbitrary")),
        name="expert_ffn",
    )(idx.reshape(-1), h, gate.reshape(ne, cap, 1), w_gate, w_up, w_down)


def _combine_kernel(off_ref, x_ref, pos_ref, sel_ref, g_ref, y_hbm, *rest, ne, cap, nchunk, final):
    nout = 1 if final else 2
    o_refs = rest[:nout]
    buf, obuf, acc, sem, osem = rest[nout:]
    c = pl.program_id(0)
    slot = c % 2
    win = COMBINE_WINDOW
    nrows = ne * cap
    half = acc.shape[1] // 2

    def chunk_off(e, cc):
        return off_ref[e * (nchunk + 1) + cc]

    def win_row(e, cc, p):
        return jnp.minimum(e * cap + chunk_off(e, cc) + p * win, nrows - win)

    def tiles(row, count):
        return pl.ds(pl.multiple_of(row * SUBLANES, SUBLANES), count * SUBLANES)

    def fetch(e, cc, s):
        return pltpu.make_async_copy(y_hbm.at[tiles(win_row(e, cc, 0), win)], buf.at[s, tiles(e * win, win)],
                                     sem.at[s])

    @pl.when(c == 0)
    def _():
        for e in range(ne):
            fetch(e, 0, 0).start()

    @pl.when(c + 1 < nchunk)
    def _():
        for e in range(ne):
            fetch(e, c + 1, 1 - slot).start()

    for e in range(ne):
        fetch(e, c, slot).wait()

    pos = pos_ref[...]
    sel = sel_ref[...]

    def membership(e, p, w):
        local = pos[:, e:e + 1] - chunk_off(e, c).astype(F32)
        col = pos[:, e:e + 1] + (e * cap - win_row(e, c, p)).astype(F32)
        ok = (sel[:, e:e + 1] > 0.5) & (local >= p * win) & (local < (p + 1) * win)
        return jnp.where((col == w) & ok, 1.0, 0.0)

    lane = lax.broadcasted_iota(I32, (1, LANES), 1)
    first = lane < win
    w = jnp.where(first, lane, lane - win).astype(F32)
    member = jnp.concatenate(
        [jnp.where(first, membership(2 * q, 0, w), membership(2 * q + 1, 0, w)).astype(BF16)
         for q in range(ne // 2)], axis=1)
    a, b = _unpack_bf16_pairs(_load_row_tiles(buf.at[slot]))
    acc[:, :half] = _dot(member, a)
    acc[:, half:] = _dot(member, b)

    for e in range(ne):
        npieces = (chunk_off(e, c + 1) - chunk_off(e, c) + win - 1) // win

        def extra(p, carry, e=e):
            cp = pltpu.make_async_copy(y_hbm.at[tiles(win_row(e, c, p), win)], obuf, osem)
            cp.start()
            cp.wait()
            m = membership(e, p, lax.broadcasted_iota(I32, (1, win), 1).astype(F32)).astype(BF16)
            oa, ob = _unpack_bf16_pairs(_load_row_tiles(obuf))
            acc[:, :half] += _dot(m, oa)
            acc[:, half:] += _dot(m, ob)
            return carry

        lax.fori_loop(1, npieces, extra, 0)

    y = x_ref[...] + acc[...]
    if final:
        o_refs[0][...] = _rmsnorm(y, g_ref[...])
    else:
        o_refs[0][...] = y
        o_refs[1][...] = _rmsnorm(y, g_ref[...]).astype(BF16)


def combine(x, y_rows, pos, sel, cap, g, *, tc, final):
    s, d = x.shape
    ne = pos.shape[0]
    nchunk = s // tc
    assert 2 * COMBINE_WINDOW == LANES and ne % 2 == 0 and ne * cap >= COMBINE_WINDOW
    off = jnp.concatenate([pos[:, ::tc].astype(I32), jnp.full((ne, 1), cap, I32)], axis=1).reshape(-1)
    row_block = pl.BlockSpec((tc, d), lambda c, t: (c, 0))
    if final:
        out_specs, out_shape = row_block, jax.ShapeDtypeStruct((s, d), F32)
    else:
        out_specs = [row_block, row_block]
        out_shape = [jax.ShapeDtypeStruct((s, d), F32), jax.ShapeDtypeStruct((s, d), BF16)]
    return pl.pallas_call(
        functools.partial(_combine_kernel, ne=ne, cap=cap, nchunk=nchunk, final=final),
        grid_spec=pltpu.PrefetchScalarGridSpec(
            num_scalar_prefetch=1,
            grid=(nchunk,),
            in_specs=[
                pl.BlockSpec((tc, d), lambda c, t: (c, 0)),
                pl.BlockSpec((tc, ne), lambda c, t: (c, 0)),
                pl.BlockSpec((tc, ne), lambda c, t: (c, 0)),
                pl.BlockSpec((1, d), lambda c, t: (0, 0)),
                pl.BlockSpec(memory_space=pl.ANY),
            ],
            out_specs=out_specs,
            scratch_shapes=[
                pltpu.VMEM((2, ne * COMBINE_WINDOW * SUBLANES, LANES), I32),
                pltpu.VMEM((COMBINE_WINDOW * SUBLANES, LANES), I32),
                pltpu.VMEM((tc, d), F32),
                pltpu.SemaphoreType.DMA((2,)),
                pltpu.SemaphoreType.DMA(()),
            ],
        ),
        out_shape=out_shape,
        compiler_params=_cparams(("arbitrary",)),
        name="combine",
    )(off, x, pos.T, sel.T, g.reshape(1, d), y_rows)


def ec_moe_block(x1, h, aff, w_gate, w_up, w_down, li, g_next, *, final, th, tc):
    s = x1.shape[0]
    cap = CAPACITY_FACTOR * s // N_EXPERTS
    pos, sel, vals = select_tokens(aff, cap)
    idx, gate = build_lists(pos, sel, vals, cap)
    y_rows = expert_ffn(h, idx, gate, w_gate, w_up, w_down, li, th=th)
    return combine(x1, y_rows, pos, sel, cap, g_next, tc=tc, final=final)


def _tiles(s):
    return dict(tm=min(1024, s), tn=1024, to=min(512, s), tt=min(512, s), th=256, tc=min(256, s))


def kernel(x, rel_bias, norm_mix_g, norm_ffn_g, final_norm_g, ev_w_in, ev_sink, ev_w_out, od_w_in, od_conv_w,
           od_pool_w, od_pool_scale, od_w_out, moe_w_router, moe_w_gate, moe_w_up, moe_w_down):
    b, s, d = x.shape
    assert b == 1
    t = _tiles(s)
    depth = norm_mix_g.shape[0]
    bias = window_bias(rel_bias)
    xs = x.reshape(s, d)
    hn = prenorm(xs, norm_mix_g[0], tm=t["tm"])
    for layer in range(depth):
        i = layer // 2
        if layer % 2 == 0:
            z = in_proj(hn, ev_w_in, i, tm=t["tm"], tn=t["tn"])
            parts = [windowed_attention(z, ev_sink[i], bias), fourier_mix(z, Q_DIM + 2 * KV_DIM)]
            w_out = ev_w_out
        else:
            z = in_proj(hn, od_w_in, i, tm=t["tm"], tn=t["tn"])
            parts = [conv_pool(z, od_conv_w[i], od_pool_w[i], od_pool_scale[i], tt=t["tt"])]
            w_out = od_w_out
        x1 = out_proj(xs, parts, w_out, i, tm=t["tm"], tn=t["tn"])
        h, aff = router(x1, norm_ffn_g[layer], moe_w_router[layer], tm=t["to"])
        final = layer == depth - 1
        g_next = final_norm_g if final else norm_mix_g[layer + 1]
        res = ec_moe_block(x1, h, aff.T, moe_w_gate, moe_w_up, moe_w_down, layer, g_next,
                           final=final, th=t["th"], tc=t["tc"])
        if final:
            xs = res
        else:
            xs, hn = res
    return xs.reshape(b, s, d)
```

```python
import functools
import math

import numpy as np
import jax
import jax.numpy as jnp
from jax import lax
from jax.experimental import pallas as pl
from jax.experimental.pallas import tpu as pltpu

F32 = jnp.float32
BF16 = jnp.bfloat16
I32 = jnp.int32

HEAD_DIM = 128
N_Q_HEADS = 12
N_KV_HEADS = 4
GQA_GROUP = N_Q_HEADS // N_KV_HEADS
WINDOW = 128
ATTN_BLOCK = 128
N_FOURIER_GROUPS = 4
FOURIER_GROUP_DIM = 128
Q_DIM = N_Q_HEADS * HEAD_DIM
KV_DIM = N_KV_HEADS * HEAD_DIM
FOURIER_DIM = N_FOURIER_GROUPS * FOURIER_GROUP_DIM
N_REL_BUCKETS = 32
REL_MAX_DISTANCE = 128
CONV_DIM = 1024
POOL_WINDOWS = (2, 4, 8, 16)
POOL_GROUPS = len(POOL_WINDOWS)
POOL_GROUP_DIM = 256
POOL_DIM = POOL_GROUPS * POOL_GROUP_DIM
N_EXPERTS = 16
CAPACITY_FACTOR = 2
RMS_EPS = 1e-6
NEG_INF = -1e30
LOG2E = math.log2(math.e)

LANES = 128
SUBLANES = 8
BF16_ROWS = 16
VMEM_LIMIT = 56 * 1024 * 1024

HALO = BF16_ROWS
COMBINE_WINDOW = 64


def _cparams(sem):
    return pltpu.CompilerParams(dimension_semantics=sem, vmem_limit_bytes=VMEM_LIMIT)


def _dot(a, b):
    return jnp.dot(a, b, preferred_element_type=F32)


def _dot_nt(a, b):
    return lax.dot_general(a, b, (((1,), (1,)), ((), ())), preferred_element_type=F32)


def _rmsnorm(x, g):
    ms = jnp.mean(x * x, axis=-1, keepdims=True)
    return x * lax.rsqrt(ms + RMS_EPS) * g


def _pack_bf16_pairs(y):
    n = y.shape[1] // 2
    lo = lax.bitcast_convert_type(y[:, :n].astype(BF16).astype(F32), I32)
    hi = lax.bitcast_convert_type(y[:, n:].astype(BF16).astype(F32), I32)
    return lax.shift_right_logical(lo, 16) | hi


def _unpack_bf16_pairs(p):
    lo = lax.bitcast_convert_type(lax.shift_left(p, 16), F32).astype(BF16)
    hi = lax.bitcast_convert_type(p & (-65536), F32).astype(BF16)
    return lo, hi


def _store_row_tiles(ref, packed, row0=0):
    rows = packed.shape[0]
    for j in range(SUBLANES):
        ref[pl.ds(row0 * SUBLANES + j, rows, stride=SUBLANES), :] = packed[:, j * LANES:(j + 1) * LANES]


def _load_row_tiles(ref):
    rows = ref.shape[0] // SUBLANES
    return jnp.concatenate([ref[pl.ds(j, rows, stride=SUBLANES), :] for j in range(SUBLANES)], axis=1)


def _prenorm_kernel(x_ref, g_ref, o_ref):
    o_ref[...] = _rmsnorm(x_ref[...], g_ref[...]).astype(o_ref.dtype)


def prenorm(x, g, *, tm):
    s, d = x.shape
    return pl.pallas_call(
        _prenorm_kernel,
        grid=(s // tm,),
        in_specs=[pl.BlockSpec((tm, d), lambda i: (i, 0)), pl.BlockSpec((1, d), lambda i: (0, 0))],
        out_specs=pl.BlockSpec((tm, d), lambda i: (i, 0)),
        out_shape=jax.ShapeDtypeStruct((s, d), BF16),
        compiler_params=_cparams(("parallel",)),
        name="prenorm",
    )(x, g.reshape(1, d))


def _in_proj_kernel(h_ref, w_ref, o_ref, wb_ref):
    @pl.when(pl.program_id(1) == 0)
    def _():
        wb_ref[...] = w_ref[...].astype(BF16)

    o_ref[...] = _dot(h_ref[...], wb_ref[...]).astype(o_ref.dtype)


def in_proj(h, w, li, *, tm, tn):
    s, d = h.shape
    n = w.shape[2]
    return pl.pallas_call(
        _in_proj_kernel,
        grid=(n // tn, s // tm),
        in_specs=[
            pl.BlockSpec((tm, d), lambda j, i: (i, 0)),
            pl.BlockSpec((None, d, tn), lambda j, i: (li, 0, j)),
        ],
        out_specs=pl.BlockSpec((tm, tn), lambda j, i: (i, j)),
        out_shape=jax.ShapeDtypeStruct((s, n), BF16),
        scratch_shapes=[pltpu.VMEM((d, tn), BF16)],
        compiler_params=_cparams(("parallel", "arbitrary")),
        name="in_proj",
    )(h, w)


def _out_proj_kernel(*refs, widths):
    nparts = len(widths)
    x_ref = refs[0]
    p_refs = refs[1:1 + nparts]
    w_ref, o_ref, wb_ref = refs[1 + nparts:]

    @pl.when(pl.program_id(1) == 0)
    def _():
        wb_ref[...] = w_ref[...].astype(BF16)

    acc = x_ref[...]
    off = 0
    for p_ref, width in zip(p_refs, widths):
        acc = acc + _dot(p_ref[...].astype(BF16), wb_ref[off:off + width, :])
        off += width
    o_ref[...] = acc


def out_proj(x, parts, w, li, *, tm, tn):
    s, d = x.shape
    k = w.shape[1]
    widths = tuple(p.shape[1] for p in parts)
    assert sum(widths) == k
    in_specs = [pl.BlockSpec((tm, tn), lambda j, i: (i, j))]
    in_specs += [pl.BlockSpec((tm, wd), lambda j, i: (i, 0)) for wd in widths]
    in_specs += [pl.BlockSpec((None, k, tn), lambda j, i: (li, 0, j))]
    return pl.pallas_call(
        functools.partial(_out_proj_kernel, widths=widths),
        grid=(d // tn, s // tm),
        in_specs=in_specs,
        out_specs=pl.BlockSpec((tm, tn), lambda j, i: (i, j)),
        out_shape=jax.ShapeDtypeStruct((s, d), F32),
        scratch_shapes=[pltpu.VMEM((k, tn), BF16)],
        compiler_params=_cparams(("parallel", "arbitrary")),
        name="out_proj",
    )(x, *parts, w)


def _split_bf16(v):
    hi = v.astype(BF16)
    return hi, (v - hi.astype(F32)).astype(BF16)


def _router_kernel(x_ref, g_ref, wr_ref, h_ref, aff_ref):
    h = _rmsnorm(x_ref[...], g_ref[...])
    _store_row_tiles(h_ref, _pack_bf16_pairs(h))
    h_hi, h_lo = _split_bf16(h)
    w_hi, w_lo = _split_bf16(wr_ref[...])
    logits = _dot(h_hi, w_hi) + (_dot(h_hi, w_lo) + _dot(h_lo, w_hi))
    m = jnp.max(logits, axis=1, keepdims=True)
    p = jnp.exp(logits - m)
    aff_ref[...] = p / jnp.sum(p, axis=1, keepdims=True)


def router(x, g, w_router, *, tm):
    s, d = x.shape
    assert d == 2 * SUBLANES * LANES
    e = w_router.shape[1]
    return pl.pallas_call(
        _router_kernel,
        grid=(s // tm,),
        in_specs=[
            pl.BlockSpec((tm, d), lambda i: (i, 0)),
            pl.BlockSpec((1, d), lambda i: (0, 0)),
            pl.BlockSpec((d, e), lambda i: (0, 0)),
        ],
        out_specs=[pl.BlockSpec((tm * SUBLANES, LANES), lambda i: (i, 0)), pl.BlockSpec((tm, e), lambda i: (i, 0))],
        out_shape=[jax.ShapeDtypeStruct((s * SUBLANES, LANES), I32), jax.ShapeDtypeStruct((s, e), F32)],
        compiler_params=_cparams(("parallel",)),
        name="router",
    )(x, g.reshape(1, d), w_router)


def _t5_bucket(rel):
    nb = N_REL_BUCKETS // 2
    max_exact = nb // 2
    ret = (rel > 0).astype(jnp.int32) * nb
    n = jnp.abs(rel)
    nf = jnp.maximum(n, 1).astype(jnp.float32)
    large = max_exact + (jnp.log(nf / max_exact) / math.log(REL_MAX_DISTANCE / max_exact)
                         * (nb - max_exact)).astype(jnp.int32)
    large = jnp.minimum(large, nb - 1)
    return ret + jnp.where(n < max_exact, n, large)


def _bias_kernel(tab_ref, bucket_ref, o_ref):
    h = pl.program_id(0)
    bucket = bucket_ref[...]
    acc = jnp.zeros(bucket.shape, F32)
    for b in range(N_REL_BUCKETS):
        acc = jnp.where(bucket == b, tab_ref[b * N_Q_HEADS + h], acc)
    i = lax.broadcasted_iota(I32, bucket.shape, 0)
    j = lax.broadcasted_iota(I32, bucket.shape, 1)
    valid = jnp.abs(j - WINDOW - i) <= WINDOW
    o_ref[0] = jnp.where(valid, acc * LOG2E, NEG_INF)


def window_bias(rel_bias):
    i = jnp.arange(ATTN_BLOCK, dtype=jnp.int32)[:, None]
    j = jnp.arange(3 * ATTN_BLOCK, dtype=jnp.int32)[None, :]
    bucket = _t5_bucket((j - WINDOW) - i)
    return pl.pallas_call(
        _bias_kernel,
        grid_spec=pltpu.PrefetchScalarGridSpec(
            num_scalar_prefetch=1,
            grid=(N_Q_HEADS,),
            in_specs=[pl.BlockSpec((ATTN_BLOCK, 3 * ATTN_BLOCK), lambda h, t: (0, 0))],
            out_specs=pl.BlockSpec((1, ATTN_BLOCK, 3 * ATTN_BLOCK), lambda h, t: (h, 0, 0)),
        ),
        out_shape=jax.ShapeDtypeStruct((N_Q_HEADS, ATTN_BLOCK, 3 * ATTN_BLOCK), F32),
        compiler_params=_cparams(("arbitrary",)),
        name="window_bias",
    )(rel_bias.reshape(-1), bucket)


def _attn_kernel(sink_ref, q_ref, kp_ref, kc_ref, kn_ref, vp_ref, vc_ref, vn_ref, bias_ref, o_ref, *, nb):
    n = pl.program_id(0)
    blk = ATTN_BLOCK
    ones = jnp.ones((3 * blk, HEAD_DIM), BF16)
    scale2 = HEAD_DIM ** -0.5 * LOG2E

    def block(at_edge):
        if at_edge:
            col = lax.broadcasted_iota(I32, (1, 3 * blk), 1)
            in_seq = ((col >= blk) | (n > 0)) & ((col < 2 * blk) | (n < nb - 1))
            edge_bias = jnp.where(in_seq, 0.0, NEG_INF)
        for kv in range(N_KV_HEADS):
            cs = slice(kv * HEAD_DIM, (kv + 1) * HEAD_DIM)
            k = jnp.concatenate([kp_ref[:, cs], kc_ref[:, cs], kn_ref[:, cs]], axis=0)
            v = jnp.concatenate([vp_ref[:, cs], vc_ref[:, cs], vn_ref[:, cs]], axis=0)
            v1 = jnp.concatenate([v, ones], axis=1)
            for g in range(GQA_GROUP):
                hq = kv * GQA_GROUP + g
                hs = slice(hq * HEAD_DIM, (hq + 1) * HEAD_DIM)
                s2 = _dot_nt(q_ref[:, hs], k) * scale2 + bias_ref[kv, g * blk:(g + 1) * blk, :]
                if at_edge:
                    s2 = s2 + edge_bias
                sk2 = sink_ref[hq] * LOG2E
                m2 = jnp.maximum(jnp.max(s2, axis=-1, keepdims=True), sk2)
                p = jnp.exp2(s2 - m2).astype(BF16)
                ov = _dot(p, v1)
                denom = ov[:, HEAD_DIM:HEAD_DIM + 1] + jnp.exp2(sk2 - m2)
                o_ref[:, hs] = (ov[:, :HEAD_DIM] / denom).astype(o_ref.dtype)

    @pl.when((n > 0) & (n < nb - 1))
    def _():
        block(False)

    @pl.when((n == 0) | (n == nb - 1))
    def _():
        block(True)


def windowed_attention(z, sink, bias):
    s = z.shape[0]
    blk = ATTN_BLOCK
    nb = s // blk
    kcol = Q_DIM // KV_DIM
    vcol = kcol + 1

    def prev(n, t):
        return jnp.maximum(n - 1, 0)

    def nxt(n, t):
        return jnp.minimum(n + 1, nb - 1)

    in_specs = [
        pl.BlockSpec((blk, Q_DIM), lambda n, t: (n, 0)),
        pl.BlockSpec((blk, KV_DIM), lambda n, t: (prev(n, t), kcol)),
        pl.BlockSpec((blk, KV_DIM), lambda n, t: (n, kcol)),
        pl.BlockSpec((blk, KV_DIM), lambda n, t: (nxt(n, t), kcol)),
        pl.BlockSpec((blk, KV_DIM), lambda n, t: (prev(n, t), vcol)),
        pl.BlockSpec((blk, KV_DIM), lambda n, t: (n, vcol)),
        pl.BlockSpec((blk, KV_DIM), lambda n, t: (nxt(n, t), vcol)),
        pl.BlockSpec((N_KV_HEADS, GQA_GROUP * blk, 3 * blk), lambda n, t: (0, 0, 0)),
    ]
    return pl.pallas_call(
        functools.partial(_attn_kernel, nb=nb),
        grid_spec=pltpu.PrefetchScalarGridSpec(
            num_scalar_prefetch=1,
            grid=(nb,),
            in_specs=in_specs,
            out_specs=pl.BlockSpec((blk, Q_DIM), lambda n, t: (n, 0)),
        ),
        out_shape=jax.ShapeDtypeStruct((s, Q_DIM), BF16),
        compiler_params=_cparams(("arbitrary",)),
        name="windowed_attention",
    )(sink, z, z, z, z, z, z, z, bias.reshape(N_KV_HEADS, GQA_GROUP * blk, 3 * blk))


def _fourier_tables(s):
    n2 = LANES
    n1 = s // n2
    k1 = np.arange(n1)[None, :, None]
    s1 = np.arange(n1)[None, None, :]
    s2 = np.arange(n2)[:, None, None]
    ang = 2.0 * np.pi * ((k1 * (s2 + n2 * s1)) % s) / s
    ma = np.concatenate([np.cos(ang), -np.sin(ang)], axis=1)
    a = 2.0 * np.pi * ((np.arange(n2)[:, None] * np.arange(n2)[None, :]) % n2) / n2
    c, sn = np.cos(a), np.sin(a)
    mc = np.block([[c, sn], [-sn, c]])
    scale = 1.0 / math.sqrt(s * FOURIER_GROUP_DIM)
    md = np.concatenate([c, sn], axis=0) * scale
    return (jnp.asarray(ma, BF16), jnp.asarray(mc, BF16), jnp.asarray(md, BF16))


FOURIER_K1_BATCH = 4


def _fourier_kernel(x_ref, ma_ref, mc_ref, md_ref, o_ref, xf, yr, yi, ot, *, n1):
    n2 = LANES
    pitch = n2 + SUBLANES
    pitch_o = n1 + SUBLANES
    kb = FOURIER_K1_BATCH
    for s1 in range(n1):
        xf[s1 * pitch:s1 * pitch + n2, :] = x_ref[s1 * n2:(s1 + 1) * n2, :].astype(F32)

    def stage_a(s2, carry):
        xs = xf[pl.ds(s2, n1, stride=pitch), :].astype(BF16)
        y = _dot(ma_ref[s2], xs)
        yr[pl.ds(s2, n1, stride=pitch), :] = y[:n1]
        yi[pl.ds(s2, n1, stride=pitch), :] = y[n1:]
        return carry

    lax.fori_loop(0, n2, stage_a, 0, unroll=8)

    def stage_c(b, carry):
        def slab(ref, j):
            return ref[pl.ds(pl.multiple_of((b * kb + j) * pitch, SUBLANES), n2), :]

        y = jnp.concatenate(
            [jnp.concatenate([slab(yr, j), slab(yi, j)], axis=0) for j in range(kb)], axis=1).astype(BF16)
        xc = _dot(mc_ref[...], y)
        xx = jnp.concatenate(
            [jnp.concatenate([xc[:n2, j * LANES:(j + 1) * LANES], xc[n2:, j * LANES:(j + 1) * LANES]], axis=1)
             for j in range(kb)], axis=0).astype(BF16)
        out = _dot(xx, md_ref[...])
        for j in range(kb):
            ot[pl.ds(b * kb + j, n2, stride=pitch_o), :] = out[j * n2:(j + 1) * n2]
        return carry

    lax.fori_loop(0, n1 // kb, stage_c, 0, unroll=2)
    for k2 in range(n2):
        o_ref[k2 * n1:(k2 + 1) * n1, :] = ot[k2 * pitch_o:k2 * pitch_o + n1, :]


def fourier_mix(z, col0):
    s, w = z.shape
    n2 = LANES
    n1 = s // n2
    assert n1 % FOURIER_K1_BATCH == 0 and n1 % SUBLANES == 0
    c = FOURIER_GROUP_DIM
    ma, mc, md = _fourier_tables(s)
    cb0 = col0 // c
    slabs = pltpu.VMEM((n1 * (n2 + SUBLANES), c), F32)
    return pl.pallas_call(
        functools.partial(_fourier_kernel, n1=n1),
        grid=(N_FOURIER_GROUPS,),
        in_specs=[
            pl.BlockSpec((s, c), lambda g: (0, cb0 + g)),
            pl.BlockSpec((n2, 2 * n1, n1), lambda g: (0, 0, 0)),
            pl.BlockSpec((2 * n2, 2 * n2), lambda g: (0, 0)),
            pl.BlockSpec((2 * c, c), lambda g: (0, 0)),
        ],
        out_specs=pl.BlockSpec((s, c), lambda g: (0, g)),
        out_shape=jax.ShapeDtypeStruct((s, FOURIER_DIM), F32),
        scratch_shapes=[slabs, slabs, slabs, pltpu.VMEM((n2 * (n1 + SUBLANES), c), F32)],
        compiler_params=_cparams(("arbitrary",)),
        name="fourier_mix",
    )(z, ma, mc, md)


def _convpool_kernel(zp_ref, zc_ref, zn_ref, cw_ref, pw_ref, ps_ref, o_ref, *, seq, tt):
    i = pl.program_id(0)
    ext_rows = tt + 2 * HALO
    grow = i * tt - HALO + lax.broadcasted_iota(I32, (ext_rows, 1), 0)
    row_ok = (grow >= 0) & (grow < seq)

    def ext(c0, c1):
        e = jnp.concatenate([zp_ref[:, c0:c1], zc_ref[:, c0:c1], zn_ref[:, c0:c1]], axis=0).astype(F32)
        return jnp.where(row_ok, e, 0.0)

    def shifted(e, d):
        return e[HALO + d:HALO + d + tt]

    prod = ext(CONV_DIM, 2 * CONV_DIM) * ext(2 * CONV_DIM, 3 * CONV_DIM)
    cw = cw_ref[...]
    conv = shifted(prod, -1) * cw[0:1] + shifted(prod, 0) * cw[1:2] + shifted(prod, 1) * cw[2:3]
    o_ref[:, :CONV_DIM] = (zc_ref[:, :CONV_DIM].astype(F32) * conv).astype(o_ref.dtype)

    t = grow[HALO:HALO + tt]
    for g, win in enumerate(POOL_WINDOWS):
        lo = win // 2
        hi = win - 1 - lo
        c0 = 3 * CONV_DIM + g * POOL_GROUP_DIM
        e = ext(c0, c0 + POOL_GROUP_DIM)
        assert win & (win - 1) == 0 and win <= HALO
        run, step = e, 1
        while step < win:
            run = run + pltpu.roll(run, ext_rows - step, axis=0)
            step *= 2
        total = shifted(run, -lo)
        count = (jnp.minimum(t + hi, seq - 1) - jnp.maximum(t - lo, 0) + 1).astype(F32)
        pooled = total / count - shifted(e, 0)
        dg = _dot(pooled.astype(BF16), pw_ref[g].astype(BF16)) * ps_ref[:, g * POOL_GROUP_DIM:(g + 1) * POOL_GROUP_DIM]
        o_ref[:, CONV_DIM + g * POOL_GROUP_DIM:CONV_DIM + (g + 1) * POOL_GROUP_DIM] = dg.astype(o_ref.dtype)


def conv_pool(z, conv_w, pool_w, pool_scale, *, tt):
    s, w = z.shape
    nh = tt // HALO
    last_h = s // HALO - 1
    return pl.pallas_call(
        functools.partial(_convpool_kernel, seq=s, tt=tt),
        grid=(s // tt,),
        in_specs=[
            pl.BlockSpec((HALO, w), lambda i: (jnp.maximum(i * nh - 1, 0), 0)),
            pl.BlockSpec((tt, w), lambda i: (i, 0)),
            pl.BlockSpec((HALO, w), lambda i: (jnp.minimum((i + 1) * nh, last_h), 0)),
            pl.BlockSpec((3, CONV_DIM), lambda i: (0, 0)),
            pl.BlockSpec((POOL_GROUPS, POOL_GROUP_DIM, POOL_GROUP_DIM), lambda i: (0, 0, 0)),
            pl.BlockSpec((1, POOL_DIM), lambda i: (0, 0)),
        ],
        out_specs=pl.BlockSpec((tt, CONV_DIM + POOL_DIM), lambda i: (i, 0)),
        out_shape=jax.ShapeDtypeStruct((s, CONV_DIM + POOL_DIM), BF16),
        compiler_params=_cparams(("arbitrary",)),
        name="conv_pool",
    )(z, z, z, conv_w, pool_w, pool_scale.reshape(1, POOL_DIM))


def _excl_cumsum_lanes(m, upper):
    r, s = m.shape
    off = jnp.zeros((r, 1), F32)
    pieces = []
    for c in range(s // LANES):
        mc = m[:, c * LANES:(c + 1) * LANES]
        pieces.append(_dot(mc.astype(BF16), upper) + off)
        off = off + jnp.sum(mc, axis=1, keepdims=True)
    return jnp.concatenate(pieces, axis=1)


def _select_kernel(aff_ref, upper_ref, pos_ref, sel_ref, v_ref, *, cap):
    a = aff_ref[...]
    e, s = a.shape
    bits = pltpu.bitcast(a, I32)
    thr = jnp.zeros((e, 1), I32)
    for bit in range(30, -1, -1):
        cand = thr | (1 << bit)
        cnt = jnp.sum((bits >= cand).astype(F32), axis=1, keepdims=True)
        thr = jnp.where(cnt >= cap, cand, thr)
    gt = bits > thr
    eq = (bits == thr).astype(F32)
    need = cap - jnp.sum(gt.astype(F32), axis=1, keepdims=True)
    upper = upper_ref[...]
    eq_rank = _excl_cumsum_lanes(eq, upper)
    sel = jnp.where(gt | ((eq > 0.5) & (eq_rank < need)), 1.0, 0.0)
    pos = _excl_cumsum_lanes(sel, upper)
    pos_ref[...] = pos
    sel_ref[...] = sel
    tok = lax.broadcasted_iota(I32, (1, s), 1).astype(F32)
    tok_hi = jnp.floor(tok * (1.0 / LANES))
    tok_lo = tok - tok_hi * LANES
    g1 = a.astype(BF16).astype(F32)
    g2 = (a - g1).astype(BF16).astype(F32)
    g3 = a - g1 - g2
    zero = jnp.zeros((SUBLANES - 5, s), F32)
    for x in range(e):
        v_ref[x] = jnp.concatenate([tok_hi, tok_lo, g1[x:x + 1], g2[x:x + 1], g3[x:x + 1], zero], axis=0)


def select_tokens(aff, cap):
    e, s = aff.shape
    upper = jnp.asarray(np.triu(np.ones((LANES, LANES), np.float32), k=1), BF16)
    full2 = lambda shp: pl.BlockSpec(shp, lambda i: (0,) * len(shp))
    return pl.pallas_call(
        functools.partial(_select_kernel, cap=cap),
        grid=(1,),
        in_specs=[full2((e, s)), full2((LANES, LANES))],
        out_specs=[full2((e, s)), full2((e, s)), full2((e, SUBLANES, s))],
        out_shape=[
            jax.ShapeDtypeStruct((e, s), F32),
            jax.ShapeDtypeStruct((e, s), F32),
            jax.ShapeDtypeStruct((e, SUBLANES, s), F32),
        ],
        compiler_params=_cparams(("arbitrary",)),
        name="select_tokens",
    )(aff, upper)


def _lists_kernel(off_ref, cross_ref, pos_ref, sel_ref, v_ref, o_ref, acc, *, nchunk, ntile):
    e = pl.program_id(0)
    acc[...] = jnp.zeros(acc.shape, F32)
    slot = lax.broadcasted_iota(I32, (LANES, LANES), 0).astype(F32)

    def add_chunk(c, tile):
        st = pl.multiple_of(c * LANES, LANES)
        p = pos_ref[0, :, pl.ds(st, LANES)]
        chosen = sel_ref[0, :, pl.ds(st, LANES)] > 0.5
        v = v_ref[0, :, pl.ds(st, LANES)].astype(BF16)
        base = (tile * LANES).astype(F32)
        onehot = jnp.where(((p - base) == slot) & chosen, 1.0, 0.0).astype(BF16)
        acc[tile] += _dot_nt(v, onehot)

    def body(c, carry):
        add_chunk(c, jnp.minimum(off_ref[e * nchunk + c] // LANES, ntile - 1))
        return carry

    lax.fori_loop(0, nchunk, body, 0, unroll=8)
    for j in range(1, ntile):
        add_chunk(cross_ref[e * ntile + j], jnp.int32(j))
    for j in range(ntile):
        o_ref[0, :, j * LANES:(j + 1) * LANES] = acc[j]


def build_lists(pos, sel, vals, cap):
    e, s = pos.shape
    nchunk = s // LANES
    ntile = cap // LANES
    off = pos[:, ::LANES].astype(I32)
    bounds = jnp.arange(ntile, dtype=I32) * LANES
    cross = jnp.maximum(jnp.sum((off[:, :, None] < bounds[None, None, :]).astype(I32), axis=1) - 1, 0)
    row = lambda nrow: pl.BlockSpec((1, nrow, s), lambda x, t, u: (x, 0, 0))
    lists = pl.pallas_call(
        functools.partial(_lists_kernel, nchunk=nchunk, ntile=ntile),
        grid_spec=pltpu.PrefetchScalarGridSpec(
            num_scalar_prefetch=2,
            grid=(e,),
            in_specs=[row(1), row(1), row(SUBLANES)],
            out_specs=pl.BlockSpec((1, SUBLANES, cap), lambda x, t, u: (x, 0, 0)),
            scratch_shapes=[pltpu.VMEM((ntile, SUBLANES, LANES), F32)],
        ),
        out_shape=jax.ShapeDtypeStruct((e, SUBLANES, cap), F32),
        compiler_params=_cparams(("arbitrary",)),
        name="build_lists",
    )(off.reshape(-1), cross.reshape(-1), pos.reshape(e, 1, s), sel.reshape(e, 1, s), vals)
    idx = (lists[:, 0] * LANES + lists[:, 1]).astype(I32)
    gate = (lists[:, 2] + lists[:, 3] + lists[:, 4])
    return idx, gate


def _ffn_kernel(idx_ref, h_hbm, gate_ref, wg_ref, wu_ref, wd_ref, y_ref, xbuf, xb, acc, sem_g, *, cap, nh, ne):
    e = pl.program_id(0)
    hs = pl.program_id(1)
    slot = e % 2
    rps = cap // nh

    def gather_row(expert, r, s):
        t = idx_ref[expert * cap + r]
        return pltpu.make_async_copy(h_hbm.at[pl.ds(pl.multiple_of(t * SUBLANES, SUBLANES), SUBLANES)],
                                     xbuf.at[s, pl.ds(pl.multiple_of(r * SUBLANES, SUBLANES), SUBLANES)],
                                     sem_g.at[s])

    def wait_gather(s):
        pltpu.make_async_copy(h_hbm.at[pl.ds(0, cap * SUBLANES)], xbuf.at[s], sem_g.at[s]).wait()

    @pl.when((e == 0) & (hs == 0))
    def _():
        def issue(r, carry):
            gather_row(0, r, 0).start()
            return carry

        lax.fori_loop(0, cap, issue, 0, unroll=8)

    @pl.when(hs == 0)
    def _():
        wait_gather(slot)
        lo, hi = _unpack_bf16_pairs(_load_row_tiles(xbuf.at[slot]))
        xb[...] = jnp.concatenate([lo, hi], axis=1)
        acc[...] = jnp.zeros(acc.shape, F32)

    th = wg_ref.shape[1]
    w_gu = jnp.concatenate([wg_ref[...].astype(BF16), wu_ref[...].astype(BF16)], axis=1)
    gu = _dot(xb[...], w_gu)
    gp = gu[:, :th]
    hid = (gp * (1.0 / (1.0 + jnp.exp(-gp))) * gu[:, th:]).astype(BF16)
    nxt = lax.rem(e + 1, ne)
    for k in range(rps):
        gather_row(nxt, hs * rps + k, 1 - slot).start()
    acc[...] += _dot(hid, wd_ref[...].astype(BF16))

    @pl.when(hs == nh - 1)
    def _():
        _store_row_tiles(y_ref, _pack_bf16_pairs(acc[...] * gate_ref[0]))

    @pl.when((e == ne - 1) & (hs == nh - 1))
    def _():
        wait_gather(1 - slot)


def expert_ffn(h, idx, gate, w_gate, w_up, w_down, li, *, th):
    ne, cap = idx.shape
    d = w_gate.shape[2]
    hidden = w_gate.shape[3]
    nh = hidden // th
    assert d == 2 * SUBLANES * LANES
    return pl.pallas_call(
        functools.partial(_ffn_kernel, cap=cap, nh=nh, ne=ne),
        grid_spec=pltpu.PrefetchScalarGridSpec(
            num_scalar_prefetch=1,
            grid=(ne, nh),
            in_specs=[
                pl.BlockSpec(memory_space=pl.ANY),
                pl.BlockSpec((1, cap, 1), lambda x, j, a: (x, 0, 0)),
                pl.BlockSpec((None, None, d, th), lambda x, j, a: (li, x, 0, j)),
                pl.BlockSpec((None, None, d, th), lambda x, j, a: (li, x, 0, j)),
                pl.BlockSpec((None, None, th, d), lambda x, j, a: (li, x, j, 0)),
            ],
            out_specs=pl.BlockSpec((cap * SUBLANES, LANES), lambda x, j, a: (x, 0),
                                   pipeline_mode=pl.Buffered(1)),
            scratch_shapes=[
                pltpu.VMEM((2, cap * SUBLANES, LANES), I32),
                pltpu.VMEM((cap, d), BF16),
                pltpu.VMEM((cap, d), F32),
                pltpu.SemaphoreType.DMA((2,)),
            ],
        ),
        out_shape=jax.ShapeDtypeStruct((ne * cap * SUBLANES, LANES), I32),
        compiler_params=_cparams(("arbitrary", "arbitrary")),
        name="expert_ffn",
    )(idx.reshape(-1), h, gate.reshape(ne, cap, 1), w_gate, w_up, w_down)


def _combine_kernel(off_ref, x_ref, pos_ref, sel_ref, g_ref, y_hbm, *rest, ne, cap, nchunk, final):
    nout = 1 if final else 2
    o_refs = rest[:nout]
    buf, obuf, acc, member, sem, osem = rest[nout:]
    c = pl.program_id(0)
    slot = c % 2
    win = COMBINE_WINDOW
    nrows = ne * cap
    half = acc.shape[1] // 2

    def chunk_off(e, cc):
        return off_ref[e * (nchunk + 1) + cc]

    def win_row(e, cc, p):
        return jnp.minimum(e * cap + chunk_off(e, cc) + p * win, nrows - win)

    def tiles(row, count):
        return pl.ds(pl.multiple_of(row * SUBLANES, SUBLANES), count * SUBLANES)

    def fetch(e, cc, s):
        return pltpu.make_async_copy(y_hbm.at[tiles(win_row(e, cc, 0), win)], buf.at[s, tiles(e * win, win)],
                                     sem.at[s])

    @pl.when(c == 0)
    def _():
        for e in range(ne):
            fetch(e, 0, 0).start()

    @pl.when(c + 1 < nchunk)
    def _():
        for e in range(ne):
            fetch(e, c + 1, 1 - slot).start()

    for e in range(ne):
        fetch(e, c, slot).wait()

    pos = pos_ref[...]
    sel = sel_ref[...]

    def window_col(e, p):
        local = pos[:, e:e + 1] - chunk_off(e, c).astype(F32)
        col = pos[:, e:e + 1] + (e * cap - win_row(e, c, p)).astype(F32)
        ok = (sel[:, e:e + 1] > 0.5) & (local >= p * win) & (local < (p + 1) * win)
        return jnp.where(ok, col, -1.0)

    lane = lax.broadcasted_iota(I32, (1, LANES), 1)
    first = lane < win
    w = jnp.where(first, lane, lane - win).astype(F32)
    for q in range(ne // 2):
        col = jnp.where(first, window_col(2 * q, 0), window_col(2 * q + 1, 0))
        member[:, q * LANES:(q + 1) * LANES] = jnp.where(col == w, 1.0, 0.0).astype(BF16)
    a, b = _unpack_bf16_pairs(_load_row_tiles(buf.at[slot]))
    acc[:, :half] = _dot(member[...], a)
    acc[:, half:] = _dot(member[...], b)

    for e in range(ne):
        npieces = (chunk_off(e, c + 1) - chunk_off(e, c) + win - 1) // win

        def extra(p, carry, e=e):
            cp = pltpu.make_async_copy(y_hbm.at[tiles(win_row(e, c, p), win)], obuf, osem)
            cp.start()
            cp.wait()
            wi = lax.broadcasted_iota(I32, (1, win), 1).astype(F32)
            m = jnp.where(window_col(e, p) == wi, 1.0, 0.0).astype(BF16)
            oa, ob = _unpack_bf16_pairs(_load_row_tiles(obuf))
            acc[:, :half] += _dot(m, oa)
            acc[:, half:] += _dot(m, ob)
            return carry

        lax.fori_loop(1, npieces, extra, 0)

    y = x_ref[...] + acc[...]
    if final:
        o_refs[0][...] = _rmsnorm(y, g_ref[...])
    else:
        o_refs[0][...] = y
        o_refs[1][...] = _rmsnorm(y, g_ref[...]).astype(BF16)


def combine(x, y_rows, pos, sel, cap, g, *, tc, final):
    s, d = x.shape
    ne = pos.shape[0]
    nchunk = s // tc
    assert 2 * COMBINE_WINDOW == LANES and ne % 2 == 0 and ne * cap >= COMBINE_WINDOW
    off = jnp.concatenate([pos[:, ::tc].astype(I32), jnp.full((ne, 1), cap, I32)], axis=1).reshape(-1)
    row_block = pl.BlockSpec((tc, d), lambda c, t: (c, 0))
    if final:
        out_specs, out_shape = row_block, jax.ShapeDtypeStruct((s, d), F32)
    else:
        out_specs = [row_block, row_block]
        out_shape = [jax.ShapeDtypeStruct((s, d), F32), jax.ShapeDtypeStruct((s, d), BF16)]
    return pl.pallas_call(
        functools.partial(_combine_kernel, ne=ne, cap=cap, nchunk=nchunk, final=final),
        grid_spec=pltpu.PrefetchScalarGridSpec(
            num_scalar_prefetch=1,
            grid=(nchunk,),
            in_specs=[
                pl.BlockSpec((tc, d), lambda c, t: (c, 0)),
                pl.BlockSpec((tc, ne), lambda c, t: (c, 0)),
                pl.BlockSpec((tc, ne), lambda c, t: (c, 0)),
                pl.BlockSpec((1, d), lambda c, t: (0, 0)),
                pl.BlockSpec(memory_space=pl.ANY),
            ],
            out_specs=out_specs,
            scratch_shapes=[
                pltpu.VMEM((2, ne * COMBINE_WINDOW * SUBLANES, LANES), I32),
                pltpu.VMEM((COMBINE_WINDOW * SUBLANES, LANES), I32),
                pltpu.VMEM((tc, d), F32),
                pltpu.VMEM((tc, ne * COMBINE_WINDOW), BF16),
                pltpu.SemaphoreType.DMA((2,)),
                pltpu.SemaphoreType.DMA(()),
            ],
        ),
        out_shape=out_shape,
        compiler_params=_cparams(("arbitrary",)),
        name="combine",
    )(off, x, pos.T, sel.T, g.reshape(1, d), y_rows)


def ec_moe_block(x1, h, aff, w_gate, w_up, w_down, li, g_next, *, final, th, tc):
    s = x1.shape[0]
    cap = CAPACITY_FACTOR * s // N_EXPERTS
    pos, sel, vals = select_tokens(aff, cap)
    idx, gate = build_lists(pos, sel, vals, cap)
    y_rows = expert_ffn(h, idx, gate, w_gate, w_up, w_down, li, th=th)
    return combine(x1, y_rows, pos, sel, cap, g_next, tc=tc, final=final)


def _tiles(s):
    return dict(tm=min(1024, s), tn=1024, to=min(512, s), tt=min(512, s), th=256, tc=min(256, s))


def kernel(x, rel_bias, norm_mix_g, norm_ffn_g, final_norm_g, ev_w_in, ev_sink, ev_w_out, od_w_in, od_conv_w,
           od_pool_w, od_pool_scale, od_w_out, moe_w_router, moe_w_gate, moe_w_up, moe_w_down):
    b, s, d = x.shape
    assert b == 1
    t = _tiles(s)
    depth = norm_mix_g.shape[0]
    bias = window_bias(rel_bias)
    xs = x.reshape(s, d)
    hn = prenorm(xs, norm_mix_g[0], tm=t["tm"])
    for layer in range(depth):
        i = layer // 2
        if layer % 2 == 0:
            z = in_proj(hn, ev_w_in, i, tm=t["tm"], tn=t["tn"])
            parts = [windowed_attention(z, ev_sink[i], bias), fourier_mix(z, Q_DIM + 2 * KV_DIM)]
            w_out = ev_w_out
        else:
            z = in_proj(hn, od_w_in, i, tm=t["tm"], tn=t["tn"])
            parts = [conv_pool(z, od_conv_w[i], od_pool_w[i], od_pool_scale[i], tt=t["tt"])]
            w_out = od_w_out
        x1 = out_proj(xs, parts, w_out, i, tm=t["tm"], tn=t["tn"])
        h, aff = router(x1, norm_ffn_g[layer], moe_w_router[layer], tm=t["to"])
        final = layer == depth - 1
        g_next = final_norm_g if final else norm_mix_g[layer + 1]
        res = ec_moe_block(x1, h, aff.T, moe_w_gate, moe_w_up, moe_w_down, layer, g_next,
                           final=final, th=t["th"], tc=t["tc"])
        if final:
            xs = res
        else:
            xs, hn = res
    return xs.reshape(b, s, d)
```

```python
import functools
import math

import numpy as np
import jax
import jax.numpy as jnp
from jax import lax
from jax.experimental import pallas as pl
from jax.experimental.pallas import tpu as pltpu

F32 = jnp.float32
BF16 = jnp.bfloat16
I32 = jnp.int32

HEAD_DIM = 128
N_Q_HEADS = 12
N_KV_HEADS = 4
GQA_GROUP = N_Q_HEADS // N_KV_HEADS
WINDOW = 128
ATTN_BLOCK = 128
N_FOURIER_GROUPS = 4
FOURIER_GROUP_DIM = 128
Q_DIM = N_Q_HEADS * HEAD_DIM
KV_DIM = N_KV_HEADS * HEAD_DIM
FOURIER_DIM = N_FOURIER_GROUPS * FOURIER_GROUP_DIM
N_REL_BUCKETS = 32
REL_MAX_DISTANCE = 128
CONV_DIM = 1024
POOL_WINDOWS = (2, 4, 8, 16)
POOL_GROUPS = len(POOL_WINDOWS)
POOL_GROUP_DIM = 256
POOL_DIM = POOL_GROUPS * POOL_GROUP_DIM
N_EXPERTS = 16
CAPACITY_FACTOR = 2
RMS_EPS = 1e-6
NEG_INF = -1e30
LOG2E = math.log2(math.e)

LANES = 128
SUBLANES = 8
BF16_ROWS = 16
VMEM_LIMIT = 56 * 1024 * 1024

HALO = BF16_ROWS
COMBINE_WINDOW = 64


def _cparams(sem):
    return pltpu.CompilerParams(dimension_semantics=sem, vmem_limit_bytes=VMEM_LIMIT)


def _dot(a, b):
    return jnp.dot(a, b, preferred_element_type=F32)


def _dot_nt(a, b):
    return lax.dot_general(a, b, (((1,), (1,)), ((), ())), preferred_element_type=F32)


def _rmsnorm(x, g):
    ms = jnp.mean(x * x, axis=-1, keepdims=True)
    return x * lax.rsqrt(ms + RMS_EPS) * g


def _pack_bf16_pairs(y):
    n = y.shape[1] // 2
    lo = lax.bitcast_convert_type(y[:, :n].astype(BF16).astype(F32), I32)
    hi = lax.bitcast_convert_type(y[:, n:].astype(BF16).astype(F32), I32)
    return lax.shift_right_logical(lo, 16) | hi


def _unpack_bf16_pairs(p):
    lo = lax.bitcast_convert_type(lax.shift_left(p, 16), F32).astype(BF16)
    hi = lax.bitcast_convert_type(p & (-65536), F32).astype(BF16)
    return lo, hi


def _store_row_tiles(ref, packed, row0=0):
    rows = packed.shape[0]
    for j in range(SUBLANES):
        ref[pl.ds(row0 * SUBLANES + j, rows, stride=SUBLANES), :] = packed[:, j * LANES:(j + 1) * LANES]


def _load_row_tiles(ref):
    rows = ref.shape[0] // SUBLANES
    return jnp.concatenate([ref[pl.ds(j, rows, stride=SUBLANES), :] for j in range(SUBLANES)], axis=1)


def _prenorm_kernel(x_ref, g_ref, o_ref):
    o_ref[...] = _rmsnorm(x_ref[...], g_ref[...]).astype(o_ref.dtype)


def prenorm(x, g, *, tm):
    s, d = x.shape
    return pl.pallas_call(
        _prenorm_kernel,
        grid=(s // tm,),
        in_specs=[pl.BlockSpec((tm, d), lambda i: (i, 0)), pl.BlockSpec((1, d), lambda i: (0, 0))],
        out_specs=pl.BlockSpec((tm, d), lambda i: (i, 0)),
        out_shape=jax.ShapeDtypeStruct((s, d), BF16),
        compiler_params=_cparams(("parallel",)),
        name="prenorm",
    )(x, g.reshape(1, d))


def _in_proj_kernel(h_ref, w_ref, o_ref, wb_ref):
    @pl.when(pl.program_id(1) == 0)
    def _():
        wb_ref[...] = w_ref[...].astype(BF16)

    o_ref[...] = _dot(h_ref[...], wb_ref[...]).astype(o_ref.dtype)


def in_proj(h, w, li, *, tm, tn):
    s, d = h.shape
    n = w.shape[2]
    return pl.pallas_call(
        _in_proj_kernel,
        grid=(n // tn, s // tm),
        in_specs=[
            pl.BlockSpec((tm, d), lambda j, i: (i, 0)),
            pl.BlockSpec((None, d, tn), lambda j, i: (li, 0, j)),
        ],
        out_specs=pl.BlockSpec((tm, tn), lambda j, i: (i, j)),
        out_shape=jax.ShapeDtypeStruct((s, n), BF16),
        scratch_shapes=[pltpu.VMEM((d, tn), BF16)],
        compiler_params=_cparams(("parallel", "arbitrary")),
        name="in_proj",
    )(h, w)


def _out_proj_kernel(*refs, widths):
    nparts = len(widths)
    x_ref = refs[0]
    p_refs = refs[1:1 + nparts]
    w_ref, o_ref, wb_ref = refs[1 + nparts:]

    @pl.when(pl.program_id(1) == 0)
    def _():
        wb_ref[...] = w_ref[...].astype(BF16)

    acc = x_ref[...]
    off = 0
    for p_ref, width in zip(p_refs, widths):
        acc = acc + _dot(p_ref[...].astype(BF16), wb_ref[off:off + width, :])
        off += width
    o_ref[...] = acc


def out_proj(x, parts, w, li, *, tm, tn):
    s, d = x.shape
    k = w.shape[1]
    widths = tuple(p.shape[1] for p in parts)
    assert sum(widths) == k
    in_specs = [pl.BlockSpec((tm, tn), lambda j, i: (i, j))]
    in_specs += [pl.BlockSpec((tm, wd), lambda j, i: (i, 0)) for wd in widths]
    in_specs += [pl.BlockSpec((None, k, tn), lambda j, i: (li, 0, j))]
    return pl.pallas_call(
        functools.partial(_out_proj_kernel, widths=widths),
        grid=(d // tn, s // tm),
        in_specs=in_specs,
        out_specs=pl.BlockSpec((tm, tn), lambda j, i: (i, j)),
        out_shape=jax.ShapeDtypeStruct((s, d), F32),
        scratch_shapes=[pltpu.VMEM((k, tn), BF16)],
        compiler_params=_cparams(("parallel", "arbitrary")),
        name="out_proj",
    )(x, *parts, w)


def _split_bf16(v):
    hi = v.astype(BF16)
    return hi, (v - hi.astype(F32)).astype(BF16)


def _router_kernel(x_ref, g_ref, wr_ref, h_ref, aff_ref):
    h = _rmsnorm(x_ref[...], g_ref[...])
    _store_row_tiles(h_ref, _pack_bf16_pairs(h))
    h_hi, h_lo = _split_bf16(h)
    w_hi, w_lo = _split_bf16(wr_ref[...])
    ne = w_hi.shape[1]
    hh = _dot(h_hi, jnp.concatenate([w_hi, w_lo], axis=1))
    logits = hh[:, :ne] + (hh[:, ne:] + _dot(h_lo, w_hi))
    m = jnp.max(logits, axis=1, keepdims=True)
    p = jnp.exp(logits - m)
    aff_ref[...] = p / jnp.sum(p, axis=1, keepdims=True)


def router(x, g, w_router, *, tm):
    s, d = x.shape
    assert d == 2 * SUBLANES * LANES
    e = w_router.shape[1]
    return pl.pallas_call(
        _router_kernel,
        grid=(s // tm,),
        in_specs=[
            pl.BlockSpec((tm, d), lambda i: (i, 0)),
            pl.BlockSpec((1, d), lambda i: (0, 0)),
            pl.BlockSpec((d, e), lambda i: (0, 0)),
        ],
        out_specs=[pl.BlockSpec((tm * SUBLANES, LANES), lambda i: (i, 0)), pl.BlockSpec((tm, e), lambda i: (i, 0))],
        out_shape=[jax.ShapeDtypeStruct((s * SUBLANES, LANES), I32), jax.ShapeDtypeStruct((s, e), F32)],
        compiler_params=_cparams(("parallel",)),
        name="router",
    )(x, g.reshape(1, d), w_router)


def _t5_bucket(rel):
    nb = N_REL_BUCKETS // 2
    max_exact = nb // 2
    ret = (rel > 0).astype(jnp.int32) * nb
    n = jnp.abs(rel)
    nf = jnp.maximum(n, 1).astype(jnp.float32)
    large = max_exact + (jnp.log(nf / max_exact) / math.log(REL_MAX_DISTANCE / max_exact)
                         * (nb - max_exact)).astype(jnp.int32)
    large = jnp.minimum(large, nb - 1)
    return ret + jnp.where(n < max_exact, n, large)


def _bias_kernel(tab_ref, bucket_ref, o_ref):
    h = pl.program_id(0)
    bucket = bucket_ref[...]
    acc = jnp.zeros(bucket.shape, F32)
    for b in range(N_REL_BUCKETS):
        acc = jnp.where(bucket == b, tab_ref[b * N_Q_HEADS + h], acc)
    i = lax.broadcasted_iota(I32, bucket.shape, 0)
    j = lax.broadcasted_iota(I32, bucket.shape, 1)
    valid = jnp.abs(j - WINDOW - i) <= WINDOW
    o_ref[0] = jnp.where(valid, acc * LOG2E, NEG_INF)


def window_bias(rel_bias):
    i = jnp.arange(ATTN_BLOCK, dtype=jnp.int32)[:, None]
    j = jnp.arange(3 * ATTN_BLOCK, dtype=jnp.int32)[None, :]
    bucket = _t5_bucket((j - WINDOW) - i)
    return pl.pallas_call(
        _bias_kernel,
        grid_spec=pltpu.PrefetchScalarGridSpec(
            num_scalar_prefetch=1,
            grid=(N_Q_HEADS,),
            in_specs=[pl.BlockSpec((ATTN_BLOCK, 3 * ATTN_BLOCK), lambda h, t: (0, 0))],
            out_specs=pl.BlockSpec((1, ATTN_BLOCK, 3 * ATTN_BLOCK), lambda h, t: (h, 0, 0)),
        ),
        out_shape=jax.ShapeDtypeStruct((N_Q_HEADS, ATTN_BLOCK, 3 * ATTN_BLOCK), F32),
        compiler_params=_cparams(("arbitrary",)),
        name="window_bias",
    )(rel_bias.reshape(-1), bucket)


def _attn_kernel(sink_ref, q_ref, kp_ref, kc_ref, kn_ref, vp_ref, vc_ref, vn_ref, bias_ref, o_ref, *, nb):
    n = pl.program_id(0)
    blk = ATTN_BLOCK
    ones = jnp.ones((3 * blk, HEAD_DIM), BF16)
    scale2 = HEAD_DIM ** -0.5 * LOG2E

    def block(at_edge):
        if at_edge:
            col = lax.broadcasted_iota(I32, (1, 3 * blk), 1)
            in_seq = ((col >= blk) | (n > 0)) & ((col < 2 * blk) | (n < nb - 1))
            edge_bias = jnp.where(in_seq, 0.0, NEG_INF)
        for kv in range(N_KV_HEADS):
            cs = slice(kv * HEAD_DIM, (kv + 1) * HEAD_DIM)
            k = jnp.concatenate([kp_ref[:, cs], kc_ref[:, cs], kn_ref[:, cs]], axis=0)
            v = jnp.concatenate([vp_ref[:, cs], vc_ref[:, cs], vn_ref[:, cs]], axis=0)
            v1 = jnp.concatenate([v, ones], axis=1)
            for g in range(GQA_GROUP):
                hq = kv * GQA_GROUP + g
                hs = slice(hq * HEAD_DIM, (hq + 1) * HEAD_DIM)
                s2 = _dot_nt(q_ref[:, hs], k) * scale2 + bias_ref[kv, g * blk:(g + 1) * blk, :]
                if at_edge:
                    s2 = s2 + edge_bias
                sk2 = sink_ref[hq] * LOG2E
                m2 = jnp.maximum(jnp.max(s2, axis=-1, keepdims=True), sk2)
                p = jnp.exp2(s2 - m2).astype(BF16)
                ov = _dot(p, v1)
                denom = ov[:, HEAD_DIM:HEAD_DIM + 1] + jnp.exp2(sk2 - m2)
                o_ref[:, hs] = (ov[:, :HEAD_DIM] / denom).astype(o_ref.dtype)

    @pl.when((n > 0) & (n < nb - 1))
    def _():
        block(False)

    @pl.when((n == 0) | (n == nb - 1))
    def _():
        block(True)


def windowed_attention(z, sink, bias):
    s = z.shape[0]
    blk = ATTN_BLOCK
    nb = s // blk
    kcol = Q_DIM // KV_DIM
    vcol = kcol + 1

    def prev(n, t):
        return jnp.maximum(n - 1, 0)

    def nxt(n, t):
        return jnp.minimum(n + 1, nb - 1)

    in_specs = [
        pl.BlockSpec((blk, Q_DIM), lambda n, t: (n, 0)),
        pl.BlockSpec((blk, KV_DIM), lambda n, t: (prev(n, t), kcol)),
        pl.BlockSpec((blk, KV_DIM), lambda n, t: (n, kcol)),
        pl.BlockSpec((blk, KV_DIM), lambda n, t: (nxt(n, t), kcol)),
        pl.BlockSpec((blk, KV_DIM), lambda n, t: (prev(n, t), vcol)),
        pl.BlockSpec((blk, KV_DIM), lambda n, t: (n, vcol)),
        pl.BlockSpec((blk, KV_DIM), lambda n, t: (nxt(n, t), vcol)),
        pl.BlockSpec((N_KV_HEADS, GQA_GROUP * blk, 3 * blk), lambda n, t: (0, 0, 0)),
    ]
    return pl.pallas_call(
        functools.partial(_attn_kernel, nb=nb),
        grid_spec=pltpu.PrefetchScalarGridSpec(
            num_scalar_prefetch=1,
            grid=(nb,),
            in_specs=in_specs,
            out_specs=pl.BlockSpec((blk, Q_DIM), lambda n, t: (n, 0)),
        ),
        out_shape=jax.ShapeDtypeStruct((s, Q_DIM), BF16),
        compiler_params=_cparams(("arbitrary",)),
        name="windowed_attention",
    )(sink, z, z, z, z, z, z, z, bias.reshape(N_KV_HEADS, GQA_GROUP * blk, 3 * blk))


def _fourier_tables(s):
    n2 = LANES
    n1 = s // n2
    k1 = np.arange(n1)[None, :, None]
    s1 = np.arange(n1)[None, None, :]
    s2 = np.arange(n2)[:, None, None]
    ang = 2.0 * np.pi * ((k1 * (s2 + n2 * s1)) % s) / s
    ma = np.concatenate([np.cos(ang), -np.sin(ang)], axis=1)
    a = 2.0 * np.pi * ((np.arange(n2)[:, None] * np.arange(n2)[None, :]) % n2) / n2
    c, sn = np.cos(a), np.sin(a)
    mc = np.block([[c, sn], [-sn, c]])
    scale = 1.0 / math.sqrt(s * FOURIER_GROUP_DIM)
    md = np.concatenate([c, sn], axis=0) * scale
    return (jnp.asarray(ma, BF16), jnp.asarray(mc, BF16), jnp.asarray(md, BF16))


FOURIER_K1_BATCH = 4


def _fourier_kernel(x_ref, ma_ref, mc_ref, md_ref, o_ref, xf, yr, yi, ot, *, n1):
    n2 = LANES
    pitch = n2 + SUBLANES
    pitch_o = n1 + SUBLANES
    kb = FOURIER_K1_BATCH
    for s1 in range(n1):
        xf[s1 * pitch:s1 * pitch + n2, :] = x_ref[s1 * n2:(s1 + 1) * n2, :].astype(F32)

    def stage_a(s2, carry):
        xs = xf[pl.ds(s2, n1, stride=pitch), :].astype(BF16)
        y = _dot(ma_ref[s2], xs)
        yr[pl.ds(s2, n1, stride=pitch), :] = y[:n1]
        yi[pl.ds(s2, n1, stride=pitch), :] = y[n1:]
        return carry

    lax.fori_loop(0, n2, stage_a, 0, unroll=8)

    def stage_c(b, carry):
        def slab(ref, j):
            return ref[pl.ds(pl.multiple_of((b * kb + j) * pitch, SUBLANES), n2), :]

        y = jnp.concatenate(
            [jnp.concatenate([slab(yr, j), slab(yi, j)], axis=0) for j in range(kb)], axis=1).astype(BF16)
        xc = _dot(mc_ref[...], y)
        xx = jnp.concatenate(
            [jnp.concatenate([xc[:n2, j * LANES:(j + 1) * LANES], xc[n2:, j * LANES:(j + 1) * LANES]], axis=1)
             for j in range(kb)], axis=0).astype(BF16)
        out = _dot(xx, md_ref[...])
        for j in range(kb):
            ot[pl.ds(b * kb + j, n2, stride=pitch_o), :] = out[j * n2:(j + 1) * n2]
        return carry

    lax.fori_loop(0, n1 // kb, stage_c, 0, unroll=2)
    for k2 in range(n2):
        o_ref[k2 * n1:(k2 + 1) * n1, :] = ot[k2 * pitch_o:k2 * pitch_o + n1, :]


def fourier_mix(z, col0):
    s, w = z.shape
    n2 = LANES
    n1 = s // n2
    assert n1 % FOURIER_K1_BATCH == 0 and n1 % SUBLANES == 0
    c = FOURIER_GROUP_DIM
    ma, mc, md = _fourier_tables(s)
    cb0 = col0 // c
    slabs = pltpu.VMEM((n1 * (n2 + SUBLANES), c), F32)
    return pl.pallas_call(
        functools.partial(_fourier_kernel, n1=n1),
        grid=(N_FOURIER_GROUPS,),
        in_specs=[
            pl.BlockSpec((s, c), lambda g: (0, cb0 + g)),
            pl.BlockSpec((n2, 2 * n1, n1), lambda g: (0, 0, 0)),
            pl.BlockSpec((2 * n2, 2 * n2), lambda g: (0, 0)),
            pl.BlockSpec((2 * c, c), lambda g: (0, 0)),
        ],
        out_specs=pl.BlockSpec((s, c), lambda g: (0, g)),
        out_shape=jax.ShapeDtypeStruct((s, FOURIER_DIM), F32),
        scratch_shapes=[slabs, slabs, slabs, pltpu.VMEM((n2 * (n1 + SUBLANES), c), F32)],
        compiler_params=_cparams(("arbitrary",)),
        name="fourier_mix",
    )(z, ma, mc, md)


def _convpool_kernel(zp_ref, zc_ref, zn_ref, cw_ref, pw_ref, ps_ref, o_ref, *, seq, tt):
    i = pl.program_id(0)
    ext_rows = tt + 2 * HALO
    grow = i * tt - HALO + lax.broadcasted_iota(I32, (ext_rows, 1), 0)
    row_ok = (grow >= 0) & (grow < seq)

    def ext(c0, c1):
        e = jnp.concatenate([zp_ref[:, c0:c1], zc_ref[:, c0:c1], zn_ref[:, c0:c1]], axis=0).astype(F32)
        return jnp.where(row_ok, e, 0.0)

    def shifted(e, d):
        return e[HALO + d:HALO + d + tt]

    prod = ext(CONV_DIM, 2 * CONV_DIM) * ext(2 * CONV_DIM, 3 * CONV_DIM)
    cw = cw_ref[...]
    conv = shifted(prod, -1) * cw[0:1] + shifted(prod, 0) * cw[1:2] + shifted(prod, 1) * cw[2:3]
    o_ref[:, :CONV_DIM] = (zc_ref[:, :CONV_DIM].astype(F32) * conv).astype(o_ref.dtype)

    t = grow[HALO:HALO + tt]
    for g, win in enumerate(POOL_WINDOWS):
        lo = win // 2
        hi = win - 1 - lo
        c0 = 3 * CONV_DIM + g * POOL_GROUP_DIM
        e = ext(c0, c0 + POOL_GROUP_DIM)
        assert win & (win - 1) == 0 and win <= HALO
        run, step = e, 1
        while step < win:
            run = run + pltpu.roll(run, ext_rows - step, axis=0)
            step *= 2
        total = shifted(run, -lo)
        count = (jnp.minimum(t + hi, seq - 1) - jnp.maximum(t - lo, 0) + 1).astype(F32)
        pooled = total / count - shifted(e, 0)
        dg = _dot(pooled.astype(BF16), pw_ref[g].astype(BF16)) * ps_ref[:, g * POOL_GROUP_DIM:(g + 1) * POOL_GROUP_DIM]
        o_ref[:, CONV_DIM + g * POOL_GROUP_DIM:CONV_DIM + (g + 1) * POOL_GROUP_DIM] = dg.astype(o_ref.dtype)


def conv_pool(z, conv_w, pool_w, pool_scale, *, tt):
    s, w = z.shape
    nh = tt // HALO
    last_h = s // HALO - 1
    return pl.pallas_call(
        functools.partial(_convpool_kernel, seq=s, tt=tt),
        grid=(s // tt,),
        in_specs=[
            pl.BlockSpec((HALO, w), lambda i: (jnp.maximum(i * nh - 1, 0), 0)),
            pl.BlockSpec((tt, w), lambda i: (i, 0)),
            pl.BlockSpec((HALO, w), lambda i: (jnp.minimum((i + 1) * nh, last_h), 0)),
            pl.BlockSpec((3, CONV_DIM), lambda i: (0, 0)),
            pl.BlockSpec((POOL_GROUPS, POOL_GROUP_DIM, POOL_GROUP_DIM), lambda i: (0, 0, 0)),
            pl.BlockSpec((1, POOL_DIM), lambda i: (0, 0)),
        ],
        out_specs=pl.BlockSpec((tt, CONV_DIM + POOL_DIM), lambda i: (i, 0)),
        out_shape=jax.ShapeDtypeStruct((s, CONV_DIM + POOL_DIM), BF16),
        compiler_params=_cparams(("arbitrary",)),
        name="conv_pool",
    )(z, z, z, conv_w, pool_w, pool_scale.reshape(1, POOL_DIM))


def _excl_cumsum_lanes(m, upper):
    r, s = m.shape
    off = jnp.zeros((r, 1), F32)
    pieces = []
    for c in range(s // LANES):
        mc = m[:, c * LANES:(c + 1) * LANES]
        pieces.append(_dot(mc.astype(BF16), upper) + off)
        off = off + jnp.sum(mc, axis=1, keepdims=True)
    return jnp.concatenate(pieces, axis=1)


def _select_kernel(aff_ref, upper_ref, pos_ref, sel_ref, v_ref, *, cap):
    a = aff_ref[...]
    e, s = a.shape
    bits = pltpu.bitcast(a, I32)
    thr = jnp.zeros((e, 1), I32)
    for bit in range(30, -1, -1):
        cand = thr | (1 << bit)
        cnt = jnp.sum((bits >= cand).astype(F32), axis=1, keepdims=True)
        thr = jnp.where(cnt >= cap, cand, thr)
    gt = bits > thr
    eq = (bits == thr).astype(F32)
    need = cap - jnp.sum(gt.astype(F32), axis=1, keepdims=True)
    upper = upper_ref[...]
    eq_rank = _excl_cumsum_lanes(eq, upper)
    sel = jnp.where(gt | ((eq > 0.5) & (eq_rank < need)), 1.0, 0.0)
    pos = _excl_cumsum_lanes(sel, upper)
    pos_ref[...] = pos
    sel_ref[...] = sel
    tok = lax.broadcasted_iota(I32, (1, s), 1).astype(F32)
    tok_hi = jnp.floor(tok * (1.0 / LANES))
    tok_lo = tok - tok_hi * LANES
    g1 = a.astype(BF16).astype(F32)
    g2 = (a - g1).astype(BF16).astype(F32)
    g3 = a - g1 - g2
    zero = jnp.zeros((SUBLANES - 5, s), F32)
    for x in range(e):
        v_ref[x] = jnp.concatenate([tok_hi, tok_lo, g1[x:x + 1], g2[x:x + 1], g3[x:x + 1], zero], axis=0)


def select_tokens(aff, cap):
    e, s = aff.shape
    upper = jnp.asarray(np.triu(np.ones((LANES, LANES), np.float32), k=1), BF16)
    full2 = lambda shp: pl.BlockSpec(shp, lambda i: (0,) * len(shp))
    return pl.pallas_call(
        functools.partial(_select_kernel, cap=cap),
        grid=(1,),
        in_specs=[full2((e, s)), full2((LANES, LANES))],
        out_specs=[full2((e, s)), full2((e, s)), full2((e, SUBLANES, s))],
        out_shape=[
            jax.ShapeDtypeStruct((e, s), F32),
            jax.ShapeDtypeStruct((e, s), F32),
            jax.ShapeDtypeStruct((e, SUBLANES, s), F32),
        ],
        compiler_params=_cparams(("arbitrary",)),
        name="select_tokens",
    )(aff, upper)


def _lists_kernel(off_ref, cross_ref, pos_ref, sel_ref, v_ref, o_ref, acc, *, nchunk, ntile):
    e = pl.program_id(0)
    acc[...] = jnp.zeros(acc.shape, F32)
    slot = lax.broadcasted_iota(I32, (LANES, LANES), 0).astype(F32)

    def add_chunk(c, tile):
        st = pl.multiple_of(c * LANES, LANES)
        p = pos_ref[0, :, pl.ds(st, LANES)]
        chosen = sel_ref[0, :, pl.ds(st, LANES)] > 0.5
        v = v_ref[0, :, pl.ds(st, LANES)].astype(BF16)
        base = (tile * LANES).astype(F32)
        onehot = jnp.where(((p - base) == slot) & chosen, 1.0, 0.0).astype(BF16)
        acc[tile] += _dot_nt(v, onehot)

    def body(c, carry):
        add_chunk(c, jnp.minimum(off_ref[e * nchunk + c] // LANES, ntile - 1))
        return carry

    lax.fori_loop(0, nchunk, body, 0, unroll=8)
    for j in range(1, ntile):
        add_chunk(cross_ref[e * ntile + j], jnp.int32(j))
    for j in range(ntile):
        o_ref[0, :, j * LANES:(j + 1) * LANES] = acc[j]


def build_lists(pos, sel, vals, cap):
    e, s = pos.shape
    nchunk = s // LANES
    ntile = cap // LANES
    off = pos[:, ::LANES].astype(I32)
    bounds = jnp.arange(ntile, dtype=I32) * LANES
    cross = jnp.maximum(jnp.sum((off[:, :, None] < bounds[None, None, :]).astype(I32), axis=1) - 1, 0)
    row = lambda nrow: pl.BlockSpec((1, nrow, s), lambda x, t, u: (x, 0, 0))
    lists = pl.pallas_call(
        functools.partial(_lists_kernel, nchunk=nchunk, ntile=ntile),
        grid_spec=pltpu.PrefetchScalarGridSpec(
            num_scalar_prefetch=2,
            grid=(e,),
            in_specs=[row(1), row(1), row(SUBLANES)],
            out_specs=pl.BlockSpec((1, SUBLANES, cap), lambda x, t, u: (x, 0, 0)),
            scratch_shapes=[pltpu.VMEM((ntile, SUBLANES, LANES), F32)],
        ),
        out_shape=jax.ShapeDtypeStruct((e, SUBLANES, cap), F32),
        compiler_params=_cparams(("arbitrary",)),
        name="build_lists",
    )(off.reshape(-1), cross.reshape(-1), pos.reshape(e, 1, s), sel.reshape(e, 1, s), vals)
    idx = (lists[:, 0] * LANES + lists[:, 1]).astype(I32)
    gate = (lists[:, 2] + lists[:, 3] + lists[:, 4])
    return idx, gate


def _ffn_kernel(idx_ref, h_hbm, gate_ref, wg_ref, wu_ref, wd_ref, y_ref, xbuf, xb, hid, sem_g, *, cap, nh, ne):
    e = pl.program_id(0)
    hs = pl.program_id(1)
    slot = e % 2
    rps = cap // nh

    def gather_row(expert, r, s):
        t = idx_ref[expert * cap + r]
        return pltpu.make_async_copy(h_hbm.at[pl.ds(pl.multiple_of(t * SUBLANES, SUBLANES), SUBLANES)],
                                     xbuf.at[s, pl.ds(pl.multiple_of(r * SUBLANES, SUBLANES), SUBLANES)],
                                     sem_g.at[s])

    def wait_gather(s):
        pltpu.make_async_copy(h_hbm.at[pl.ds(0, cap * SUBLANES)], xbuf.at[s], sem_g.at[s]).wait()

    @pl.when((e == 0) & (hs == 0))
    def _():
        def issue(r, carry):
            gather_row(0, r, 0).start()
            return carry

        lax.fori_loop(0, cap, issue, 0, unroll=8)

    @pl.when(hs == 0)
    def _():
        wait_gather(slot)
        lo, hi = _unpack_bf16_pairs(_load_row_tiles(xbuf.at[slot]))
        xb[...] = jnp.concatenate([lo, hi], axis=1)

    th = wg_ref.shape[1]
    w_gu = jnp.concatenate([wg_ref[...].astype(BF16), wu_ref[...].astype(BF16)], axis=1)
    gu = _dot(xb[...], w_gu)
    gp = gu[:, :th]
    hid[:, pl.ds(pl.multiple_of(hs * th, th), th)] = (gp * (1.0 / (1.0 + jnp.exp(-gp))) * gu[:, th:]).astype(BF16)
    nxt = lax.rem(e + 1, ne)
    for k in range(rps):
        gather_row(nxt, hs * rps + k, 1 - slot).start()

    @pl.when(hs == nh - 1)
    def _():
        w_d = wd_ref[...].astype(BF16)
        rb = cap // nh
        for b in range(nh):
            rows = slice(b * rb, (b + 1) * rb)
            y = _dot(hid[rows, :], w_d) * gate_ref[0, rows, :]
            _store_row_tiles(y_ref, _pack_bf16_pairs(y), b * rb)

    @pl.when((e == ne - 1) & (hs == nh - 1))
    def _():
        wait_gather(1 - slot)


def expert_ffn(h, idx, gate, w_gate, w_up, w_down, li, *, th):
    ne, cap = idx.shape
    d = w_gate.shape[2]
    hidden = w_gate.shape[3]
    nh = hidden // th
    assert d == 2 * SUBLANES * LANES
    return pl.pallas_call(
        functools.partial(_ffn_kernel, cap=cap, nh=nh, ne=ne),
        grid_spec=pltpu.PrefetchScalarGridSpec(
            num_scalar_prefetch=1,
            grid=(ne, nh),
            in_specs=[
                pl.BlockSpec(memory_space=pl.ANY),
                pl.BlockSpec((1, cap, 1), lambda x, j, a: (x, 0, 0)),
                pl.BlockSpec((None, None, d, th), lambda x, j, a: (li, x, 0, j)),
                pl.BlockSpec((None, None, d, th), lambda x, j, a: (li, x, 0, j)),
                pl.BlockSpec((None, None, hidden, d), lambda x, j, a: (li, x, 0, 0)),
            ],
            out_specs=pl.BlockSpec((cap * SUBLANES, LANES), lambda x, j, a: (x, 0),
                                   pipeline_mode=pl.Buffered(1)),
            scratch_shapes=[
                pltpu.VMEM((2, cap * SUBLANES, LANES), I32),
                pltpu.VMEM((cap, d), BF16),
                pltpu.VMEM((cap, hidden), BF16),
                pltpu.SemaphoreType.DMA((2,)),
            ],
        ),
        out_shape=jax.ShapeDtypeStruct((ne * cap * SUBLANES, LANES), I32),
        compiler_params=_cparams(("arbitrary", "arbitrary")),
        name="expert_ffn",
    )(idx.reshape(-1), h, gate.reshape(ne, cap, 1), w_gate, w_up, w_down)


def _combine_kernel(off_ref, x_ref, pos_ref, sel_ref, g_ref, y_hbm, *rest, ne, cap, nchunk, final):
    nout = 1 if final else 2
    o_refs = rest[:nout]
    buf, obuf, acc, member, sem, osem = rest[nout:]
    c = pl.program_id(0)
    slot = c % 2
    win = COMBINE_WINDOW
    nrows = ne * cap
    half = acc.shape[1] // 2

    def chunk_off(e, cc):
        return off_ref[e * (nchunk + 1) + cc]

    def win_row(e, cc, p):
        return jnp.minimum(e * cap + chunk_off(e, cc) + p * win, nrows - win)

    def tiles(row, count):
        return pl.ds(pl.multiple_of(row * SUBLANES, SUBLANES), count * SUBLANES)

    def fetch(e, cc, s):
        return pltpu.make_async_copy(y_hbm.at[tiles(win_row(e, cc, 0), win)], buf.at[s, tiles(e * win, win)],
                                     sem.at[s])

    @pl.when(c == 0)
    def _():
        for e in range(ne):
            fetch(e, 0, 0).start()

    @pl.when(c + 1 < nchunk)
    def _():
        for e in range(ne):
            fetch(e, c + 1, 1 - slot).start()

    for e in range(ne):
        fetch(e, c, slot).wait()

    pos = pos_ref[...]
    sel = sel_ref[...]

    def window_col(e, p):
        local = pos[:, e:e + 1] - chunk_off(e, c).astype(F32)
        col = pos[:, e:e + 1] + (e * cap - win_row(e, c, p)).astype(F32)
        ok = (sel[:, e:e + 1] > 0.5) & (local >= p * win) & (local < (p + 1) * win)
        return jnp.where(ok, col, -1.0)

    lane = lax.broadcasted_iota(I32, (1, LANES), 1)
    first = lane < win
    w = jnp.where(first, lane, lane - win).astype(F32)
    for q in range(ne // 2):
        col = jnp.where(first, window_col(2 * q, 0), window_col(2 * q + 1, 0))
        member[:, q * LANES:(q + 1) * LANES] = jnp.where(col == w, 1.0, 0.0).astype(BF16)
    a, b = _unpack_bf16_pairs(_load_row_tiles(buf.at[slot]))
    acc[:, :half] = _dot(member[...], a)
    acc[:, half:] = _dot(member[...], b)

    for e in range(ne):
        npieces = (chunk_off(e, c + 1) - chunk_off(e, c) + win - 1) // win

        def extra(p, carry, e=e):
            cp = pltpu.make_async_copy(y_hbm.at[tiles(win_row(e, c, p), win)], obuf, osem)
            cp.start()
            cp.wait()
            wi = lax.broadcasted_iota(I32, (1, win), 1).astype(F32)
            m = jnp.where(window_col(e, p) == wi, 1.0, 0.0).astype(BF16)
            oa, ob = _unpack_bf16_pairs(_load_row_tiles(obuf))
            acc[:, :half] += _dot(m, oa)
            acc[:, half:] += _dot(m, ob)
            return carry

        lax.fori_loop(1, npieces, extra, 0)

    y = x_ref[...] + acc[...]
    if final:
        o_refs[0][...] = _rmsnorm(y, g_ref[...])
    else:
        o_refs[0][...] = y
        o_refs[1][...] = _rmsnorm(y, g_ref[...]).astype(BF16)


def combine(x, y_rows, pos, sel, cap, g, *, tc, final):
    s, d = x.shape
    ne = pos.shape[0]
    nchunk = s // tc
    assert 2 * COMBINE_WINDOW == LANES and ne % 2 == 0 and ne * cap >= COMBINE_WINDOW
    off = jnp.concatenate([pos[:, ::tc].astype(I32), jnp.full((ne, 1), cap, I32)], axis=1).reshape(-1)
    row_block = pl.BlockSpec((tc, d), lambda c, t: (c, 0))
    if final:
        out_specs, out_shape = row_block, jax.ShapeDtypeStruct((s, d), F32)
    else:
        out_specs = [row_block, row_block]
        out_shape = [jax.ShapeDtypeStruct((s, d), F32), jax.ShapeDtypeStruct((s, d), BF16)]
    return pl.pallas_call(
        functools.partial(_combine_kernel, ne=ne, cap=cap, nchunk=nchunk, final=final),
        grid_spec=pltpu.PrefetchScalarGridSpec(
            num_scalar_prefetch=1,
            grid=(nchunk,),
            in_specs=[
                pl.BlockSpec((tc, d), lambda c, t: (c, 0)),
                pl.BlockSpec((tc, ne), lambda c, t: (c, 0)),
                pl.BlockSpec((tc, ne), lambda c, t: (c, 0)),
                pl.BlockSpec((1, d), lambda c, t: (0, 0)),
                pl.BlockSpec(memory_space=pl.ANY),
            ],
            out_specs=out_specs,
            scratch_shapes=[
                pltpu.VMEM((2, ne * COMBINE_WINDOW * SUBLANES, LANES), I32),
                pltpu.VMEM((COMBINE_WINDOW * SUBLANES, LANES), I32),
                pltpu.VMEM((tc, d), F32),
                pltpu.VMEM((tc, ne * COMBINE_WINDOW), BF16),
                pltpu.SemaphoreType.DMA((2,)),
                pltpu.SemaphoreType.DMA(()),
            ],
        ),
        out_shape=out_shape,
        compiler_params=_cparams(("arbitrary",)),
        name="combine",
    )(off, x, pos.T, sel.T, g.reshape(1, d), y_rows)


def ec_moe_block(x1, h, aff, w_gate, w_up, w_down, li, g_next, *, final, th, tc):
    s = x1.shape[0]
    cap = CAPACITY_FACTOR * s // N_EXPERTS
    pos, sel, vals = select_tokens(aff, cap)
    idx, gate = build_lists(pos, sel, vals, cap)
    y_rows = expert_ffn(h, idx, gate, w_gate, w_up, w_down, li, th=th)
    return combine(x1, y_rows, pos, sel, cap, g_next, tc=tc, final=final)


def _tiles(s):
    return dict(tm=min(1024, s), tn=1024, to=min(512, s), tt=min(512, s), th=256, tc=min(256, s))


def kernel(x, rel_bias, norm_mix_g, norm_ffn_g, final_norm_g, ev_w_in, ev_sink, ev_w_out, od_w_in, od_conv_w,
           od_pool_w, od_pool_scale, od_w_out, moe_w_router, moe_w_gate, moe_w_up, moe_w_down):
    b, s, d = x.shape
    assert b == 1
    t = _tiles(s)
    depth = norm_mix_g.shape[0]
    bias = window_bias(rel_bias)
    xs = x.reshape(s, d)
    hn = prenorm(xs, norm_mix_g[0], tm=t["tm"])
    for layer in range(depth):
        i = layer // 2
        if layer % 2 == 0:
            z = in_proj(hn, ev_w_in, i, tm=t["tm"], tn=t["tn"])
            parts = [windowed_attention(z, ev_sink[i], bias), fourier_mix(z, Q_DIM + 2 * KV_DIM)]
            w_out = ev_w_out
        else:
            z = in_proj(hn, od_w_in, i, tm=t["tm"], tn=t["tn"])
            parts = [conv_pool(z, od_conv_w[i], od_pool_w[i], od_pool_scale[i], tt=t["tt"])]
            w_out = od_w_out
        x1 = out_proj(xs, parts, w_out, i, tm=t["tm"], tn=t["tn"])
        h, aff = router(x1, norm_ffn_g[layer], moe_w_router[layer], tm=t["to"])
        final = layer == depth - 1
        g_next = final_norm_g if final else norm_mix_g[layer + 1]
        res = ec_moe_block(x1, h, aff.T, moe_w_gate, moe_w_up, moe_w_down, layer, g_next,
                           final=final, th=t["th"], tc=t["tc"])
        if final:
            xs = res
        else:
            xs, hn = res
    return xs.reshape(b, s, d)
```

```python
import functools
import math

import numpy as np
import jax
import jax.numpy as jnp
from jax import lax
from jax.experimental import pallas as pl
from jax.experimental.pallas import tpu as pltpu

F32 = jnp.float32
BF16 = jnp.bfloat16
I32 = jnp.int32

HEAD_DIM = 128
N_Q_HEADS = 12
N_KV_HEADS = 4
GQA_GROUP = N_Q_HEADS // N_KV_HEADS
WINDOW = 128
ATTN_BLOCK = 128
N_FOURIER_GROUPS = 4
FOURIER_GROUP_DIM = 128
Q_DIM = N_Q_HEADS * HEAD_DIM
KV_DIM = N_KV_HEADS * HEAD_DIM
FOURIER_DIM = N_FOURIER_GROUPS * FOURIER_GROUP_DIM
N_REL_BUCKETS = 32
REL_MAX_DISTANCE = 128
CONV_DIM = 1024
POOL_WINDOWS = (2, 4, 8, 16)
POOL_GROUPS = len(POOL_WINDOWS)
POOL_GROUP_DIM = 256
POOL_DIM = POOL_GROUPS * POOL_GROUP_DIM
N_EXPERTS = 16
CAPACITY_FACTOR = 2
RMS_EPS = 1e-6
NEG_INF = -1e30
LOG2E = math.log2(math.e)

LANES = 128
SUBLANES = 8
BF16_ROWS = 16
VMEM_LIMIT = 56 * 1024 * 1024

HALO = BF16_ROWS
COMBINE_WINDOW = 64


def _cparams(sem):
    return pltpu.CompilerParams(dimension_semantics=sem, vmem_limit_bytes=VMEM_LIMIT)


def _dot(a, b):
    return jnp.dot(a, b, preferred_element_type=F32)


def _dot_nt(a, b):
    return lax.dot_general(a, b, (((1,), (1,)), ((), ())), preferred_element_type=F32)


def _rmsnorm(x, g):
    ms = jnp.mean(x * x, axis=-1, keepdims=True)
    return x * lax.rsqrt(ms + RMS_EPS) * g


def _pack_bf16_pairs(y):
    n = y.shape[1] // 2
    lo = lax.bitcast_convert_type(y[:, :n].astype(BF16).astype(F32), I32)
    hi = lax.bitcast_convert_type(y[:, n:].astype(BF16).astype(F32), I32)
    return lax.shift_right_logical(lo, 16) | hi


def _unpack_bf16_pairs(p):
    lo = lax.bitcast_convert_type(lax.shift_left(p, 16), F32).astype(BF16)
    hi = lax.bitcast_convert_type(p & (-65536), F32).astype(BF16)
    return lo, hi


def _store_row_tiles(ref, packed, row0=0):
    rows = packed.shape[0]
    for j in range(SUBLANES):
        ref[pl.ds(row0 * SUBLANES + j, rows, stride=SUBLANES), :] = packed[:, j * LANES:(j + 1) * LANES]


def _load_row_tiles(ref):
    rows = ref.shape[0] // SUBLANES
    return jnp.concatenate([ref[pl.ds(j, rows, stride=SUBLANES), :] for j in range(SUBLANES)], axis=1)


def _prenorm_kernel(x_ref, g_ref, o_ref):
    o_ref[...] = _rmsnorm(x_ref[...], g_ref[...]).astype(o_ref.dtype)


def prenorm(x, g, *, tm):
    s, d = x.shape
    return pl.pallas_call(
        _prenorm_kernel,
        grid=(s // tm,),
        in_specs=[pl.BlockSpec((tm, d), lambda i: (i, 0)), pl.BlockSpec((1, d), lambda i: (0, 0))],
        out_specs=pl.BlockSpec((tm, d), lambda i: (i, 0)),
        out_shape=jax.ShapeDtypeStruct((s, d), BF16),
        compiler_params=_cparams(("parallel",)),
        name="prenorm",
    )(x, g.reshape(1, d))


def _in_proj_kernel(h_ref, w_ref, o_ref, wb_ref):
    @pl.when(pl.program_id(1) == 0)
    def _():
        wb_ref[...] = w_ref[...].astype(BF16)

    o_ref[...] = _dot(h_ref[...], wb_ref[...]).astype(o_ref.dtype)


def in_proj(h, w, li, *, tm, tn):
    s, d = h.shape
    n = w.shape[2]
    return pl.pallas_call(
        _in_proj_kernel,
        grid=(n // tn, s // tm),
        in_specs=[
            pl.BlockSpec((tm, d), lambda j, i: (i, 0)),
            pl.BlockSpec((None, d, tn), lambda j, i: (li, 0, j)),
        ],
        out_specs=pl.BlockSpec((tm, tn), lambda j, i: (i, j)),
        out_shape=jax.ShapeDtypeStruct((s, n), BF16),
        scratch_shapes=[pltpu.VMEM((d, tn), BF16)],
        compiler_params=_cparams(("parallel", "arbitrary")),
        name="in_proj",
    )(h, w)


def _out_proj_kernel(*refs, widths):
    nparts = len(widths)
    x_ref = refs[0]
    p_refs = refs[1:1 + nparts]
    w_ref, o_ref, wb_ref = refs[1 + nparts:]

    @pl.when(pl.program_id(1) == 0)
    def _():
        wb_ref[...] = w_ref[...].astype(BF16)

    acc = x_ref[...]
    off = 0
    for p_ref, width in zip(p_refs, widths):
        acc = acc + _dot(p_ref[...].astype(BF16), wb_ref[off:off + width, :])
        off += width
    o_ref[...] = acc


def out_proj(x, parts, w, li, *, tm, tn):
    s, d = x.shape
    k = w.shape[1]
    widths = tuple(p.shape[1] for p in parts)
    assert sum(widths) == k
    in_specs = [pl.BlockSpec((tm, tn), lambda j, i: (i, j))]
    in_specs += [pl.BlockSpec((tm, wd), lambda j, i: (i, 0)) for wd in widths]
    in_specs += [pl.BlockSpec((None, k, tn), lambda j, i: (li, 0, j))]
    return pl.pallas_call(
        functools.partial(_out_proj_kernel, widths=widths),
        grid=(d // tn, s // tm),
        in_specs=in_specs,
        out_specs=pl.BlockSpec((tm, tn), lambda j, i: (i, j)),
        out_shape=jax.ShapeDtypeStruct((s, d), F32),
        scratch_shapes=[pltpu.VMEM((k, tn), BF16)],
        compiler_params=_cparams(("parallel", "arbitrary")),
        name="out_proj",
    )(x, *parts, w)


def _split_bf16(v):
    hi = v.astype(BF16)
    return hi, (v - hi.astype(F32)).astype(BF16)


def _router_kernel(x_ref, g_ref, wr_ref, h_ref, aff_ref):
    h = _rmsnorm(x_ref[...], g_ref[...])
    _store_row_tiles(h_ref, _pack_bf16_pairs(h))
    h_hi, h_lo = _split_bf16(h)
    w_hi, w_lo = _split_bf16(wr_ref[...])
    ne = w_hi.shape[1]
    hh = _dot(h_hi, jnp.concatenate([w_hi, w_lo], axis=1))
    logits = hh[:, :ne] + (hh[:, ne:] + _dot(h_lo, w_hi))
    m = jnp.max(logits, axis=1, keepdims=True)
    p = jnp.exp(logits - m)
    aff_ref[...] = p / jnp.sum(p, axis=1, keepdims=True)


def router(x, g, w_router, *, tm):
    s, d = x.shape
    assert d == 2 * SUBLANES * LANES
    e = w_router.shape[1]
    return pl.pallas_call(
        _router_kernel,
        grid=(s // tm,),
        in_specs=[
            pl.BlockSpec((tm, d), lambda i: (i, 0)),
            pl.BlockSpec((1, d), lambda i: (0, 0)),
            pl.BlockSpec((d, e), lambda i: (0, 0)),
        ],
        out_specs=[pl.BlockSpec((tm * SUBLANES, LANES), lambda i: (i, 0)), pl.BlockSpec((tm, e), lambda i: (i, 0))],
        out_shape=[jax.ShapeDtypeStruct((s * SUBLANES, LANES), I32), jax.ShapeDtypeStruct((s, e), F32)],
        compiler_params=_cparams(("parallel",)),
        name="router",
    )(x, g.reshape(1, d), w_router)


def _t5_bucket(rel):
    nb = N_REL_BUCKETS // 2
    max_exact = nb // 2
    ret = (rel > 0).astype(jnp.int32) * nb
    n = jnp.abs(rel)
    nf = jnp.maximum(n, 1).astype(jnp.float32)
    large = max_exact + (jnp.log(nf / max_exact) / math.log(REL_MAX_DISTANCE / max_exact)
                         * (nb - max_exact)).astype(jnp.int32)
    large = jnp.minimum(large, nb - 1)
    return ret + jnp.where(n < max_exact, n, large)


def _bias_kernel(tab_ref, bucket_ref, o_ref):
    h = pl.program_id(0)
    bucket = bucket_ref[...]
    acc = jnp.zeros(bucket.shape, F32)
    for b in range(N_REL_BUCKETS):
        acc = jnp.where(bucket == b, tab_ref[b * N_Q_HEADS + h], acc)
    i = lax.broadcasted_iota(I32, bucket.shape, 0)
    j = lax.broadcasted_iota(I32, bucket.shape, 1)
    valid = jnp.abs(j - WINDOW - i) <= WINDOW
    o_ref[0] = jnp.where(valid, acc * LOG2E, NEG_INF)


def window_bias(rel_bias):
    i = jnp.arange(ATTN_BLOCK, dtype=jnp.int32)[:, None]
    j = jnp.arange(3 * ATTN_BLOCK, dtype=jnp.int32)[None, :]
    bucket = _t5_bucket((j - WINDOW) - i)
    return pl.pallas_call(
        _bias_kernel,
        grid_spec=pltpu.PrefetchScalarGridSpec(
            num_scalar_prefetch=1,
            grid=(N_Q_HEADS,),
            in_specs=[pl.BlockSpec((ATTN_BLOCK, 3 * ATTN_BLOCK), lambda h, t: (0, 0))],
            out_specs=pl.BlockSpec((1, ATTN_BLOCK, 3 * ATTN_BLOCK), lambda h, t: (h, 0, 0)),
        ),
        out_shape=jax.ShapeDtypeStruct((N_Q_HEADS, ATTN_BLOCK, 3 * ATTN_BLOCK), F32),
        compiler_params=_cparams(("arbitrary",)),
        name="window_bias",
    )(rel_bias.reshape(-1), bucket)


def _attn_kernel(sink_ref, q_ref, kp_ref, kc_ref, kn_ref, vp_ref, vc_ref, vn_ref, bias_ref, o_ref, *, nb):
    n = pl.program_id(0)
    blk = ATTN_BLOCK
    ones = jnp.ones((3 * blk, HEAD_DIM), BF16)
    scale2 = HEAD_DIM ** -0.5 * LOG2E

    def block(at_edge):
        if at_edge:
            col = lax.broadcasted_iota(I32, (1, 3 * blk), 1)
            in_seq = ((col >= blk) | (n > 0)) & ((col < 2 * blk) | (n < nb - 1))
            edge_bias = jnp.where(in_seq, 0.0, NEG_INF)
        for kv in range(N_KV_HEADS):
            cs = slice(kv * HEAD_DIM, (kv + 1) * HEAD_DIM)
            k = jnp.concatenate([kp_ref[:, cs], kc_ref[:, cs], kn_ref[:, cs]], axis=0)
            v = jnp.concatenate([vp_ref[:, cs], vc_ref[:, cs], vn_ref[:, cs]], axis=0)
            v1 = jnp.concatenate([v, ones], axis=1)
            for g in range(GQA_GROUP):
                hq = kv * GQA_GROUP + g
                hs = slice(hq * HEAD_DIM, (hq + 1) * HEAD_DIM)
                s2 = _dot_nt(q_ref[:, hs], k) * scale2 + bias_ref[kv, g * blk:(g + 1) * blk, :]
                if at_edge:
                    s2 = s2 + edge_bias
                sk2 = sink_ref[hq] * LOG2E
                m2 = jnp.maximum(jnp.max(s2, axis=-1, keepdims=True), sk2)
                p = jnp.exp2(s2 - m2).astype(BF16)
                ov = _dot(p, v1)
                denom = ov[:, HEAD_DIM:HEAD_DIM + 1] + jnp.exp2(sk2 - m2)
                o_ref[:, hs] = (ov[:, :HEAD_DIM] / denom).astype(o_ref.dtype)

    @pl.when((n > 0) & (n < nb - 1))
    def _():
        block(False)

    @pl.when((n == 0) | (n == nb - 1))
    def _():
        block(True)


def windowed_attention(z, sink, bias):
    s = z.shape[0]
    blk = ATTN_BLOCK
    nb = s // blk
    kcol = Q_DIM // KV_DIM
    vcol = kcol + 1

    def prev(n, t):
        return jnp.maximum(n - 1, 0)

    def nxt(n, t):
        return jnp.minimum(n + 1, nb - 1)

    in_specs = [
        pl.BlockSpec((blk, Q_DIM), lambda n, t: (n, 0)),
        pl.BlockSpec((blk, KV_DIM), lambda n, t: (prev(n, t), kcol)),
        pl.BlockSpec((blk, KV_DIM), lambda n, t: (n, kcol)),
        pl.BlockSpec((blk, KV_DIM), lambda n, t: (nxt(n, t), kcol)),
        pl.BlockSpec((blk, KV_DIM), lambda n, t: (prev(n, t), vcol)),
        pl.BlockSpec((blk, KV_DIM), lambda n, t: (n, vcol)),
        pl.BlockSpec((blk, KV_DIM), lambda n, t: (nxt(n, t), vcol)),
        pl.BlockSpec((N_KV_HEADS, GQA_GROUP * blk, 3 * blk), lambda n, t: (0, 0, 0)),
    ]
    return pl.pallas_call(
        functools.partial(_attn_kernel, nb=nb),
        grid_spec=pltpu.PrefetchScalarGridSpec(
            num_scalar_prefetch=1,
            grid=(nb,),
            in_specs=in_specs,
            out_specs=pl.BlockSpec((blk, Q_DIM), lambda n, t: (n, 0)),
        ),
        out_shape=jax.ShapeDtypeStruct((s, Q_DIM), BF16),
        compiler_params=_cparams(("arbitrary",)),
        name="windowed_attention",
    )(sink, z, z, z, z, z, z, z, bias.reshape(N_KV_HEADS, GQA_GROUP * blk, 3 * blk))


def _fourier_tables(s):
    n2 = LANES
    n1 = s // n2
    k1 = np.arange(n1)[None, :, None]
    s1 = np.arange(n1)[None, None, :]
    s2 = np.arange(n2)[:, None, None]
    ang = 2.0 * np.pi * ((k1 * (s2 + n2 * s1)) % s) / s
    ma = np.concatenate([np.cos(ang), -np.sin(ang)], axis=1)
    a = 2.0 * np.pi * ((np.arange(n2)[:, None] * np.arange(n2)[None, :]) % n2) / n2
    c, sn = np.cos(a), np.sin(a)
    mc = np.block([[c, sn], [-sn, c]])
    scale = 1.0 / math.sqrt(s * FOURIER_GROUP_DIM)
    md = np.concatenate([c, sn], axis=0) * scale
    return (jnp.asarray(ma, BF16), jnp.asarray(mc, BF16), jnp.asarray(md, BF16))


FOURIER_K1_BATCH = 4


def _fourier_kernel(x_ref, ma_ref, mc_ref, md_ref, o_ref, xf, yr, yi, ot, *, n1):
    n2 = LANES
    pitch = n2 + SUBLANES
    pitch_o = n1 + SUBLANES
    kb = FOURIER_K1_BATCH
    for s1 in range(n1):
        xf[s1 * pitch:s1 * pitch + n2, :] = x_ref[s1 * n2:(s1 + 1) * n2, :].astype(F32)

    def stage_a(s2, carry):
        xs = xf[pl.ds(s2, n1, stride=pitch), :].astype(BF16)
        y = _dot(ma_ref[s2], xs)
        yr[pl.ds(s2, n1, stride=pitch), :] = y[:n1]
        yi[pl.ds(s2, n1, stride=pitch), :] = y[n1:]
        return carry

    lax.fori_loop(0, n2, stage_a, 0, unroll=8)

    def stage_c(b, carry):
        def slab(ref, j):
            return ref[pl.ds(pl.multiple_of((b * kb + j) * pitch, SUBLANES), n2), :]

        y = jnp.concatenate(
            [jnp.concatenate([slab(yr, j), slab(yi, j)], axis=0) for j in range(kb)], axis=1).astype(BF16)
        xc = _dot(mc_ref[...], y)
        xx = jnp.concatenate(
            [jnp.concatenate([xc[:n2, j * LANES:(j + 1) * LANES], xc[n2:, j * LANES:(j + 1) * LANES]], axis=1)
             for j in range(kb)], axis=0).astype(BF16)
        out = _dot(xx, md_ref[...])
        for j in range(kb):
            ot[pl.ds(b * kb + j, n2, stride=pitch_o), :] = out[j * n2:(j + 1) * n2]
        return carry

    lax.fori_loop(0, n1 // kb, stage_c, 0, unroll=2)
    for k2 in range(n2):
        o_ref[k2 * n1:(k2 + 1) * n1, :] = ot[k2 * pitch_o:k2 * pitch_o + n1, :]


def fourier_mix(z, col0):
    s, w = z.shape
    n2 = LANES
    n1 = s // n2
    assert n1 % FOURIER_K1_BATCH == 0 and n1 % SUBLANES == 0
    c = FOURIER_GROUP_DIM
    ma, mc, md = _fourier_tables(s)
    cb0 = col0 // c
    slabs = pltpu.VMEM((n1 * (n2 + SUBLANES), c), F32)
    return pl.pallas_call(
        functools.partial(_fourier_kernel, n1=n1),
        grid=(N_FOURIER_GROUPS,),
        in_specs=[
            pl.BlockSpec((s, c), lambda g: (0, cb0 + g)),
            pl.BlockSpec((n2, 2 * n1, n1), lambda g: (0, 0, 0)),
            pl.BlockSpec((2 * n2, 2 * n2), lambda g: (0, 0)),
            pl.BlockSpec((2 * c, c), lambda g: (0, 0)),
        ],
        out_specs=pl.BlockSpec((s, c), lambda g: (0, g)),
        out_shape=jax.ShapeDtypeStruct((s, FOURIER_DIM), F32),
        scratch_shapes=[slabs, slabs, slabs, pltpu.VMEM((n2 * (n1 + SUBLANES), c), F32)],
        compiler_params=_cparams(("arbitrary",)),
        name="fourier_mix",
    )(z, ma, mc, md)


def _convpool_kernel(zp_ref, zc_ref, zn_ref, cw_ref, pw_ref, ps_ref, o_ref, *, seq, tt):
    i = pl.program_id(0)
    ext_rows = tt + 2 * HALO
    grow = i * tt - HALO + lax.broadcasted_iota(I32, (ext_rows, 1), 0)
    row_ok = (grow >= 0) & (grow < seq)

    def ext(c0, c1):
        e = jnp.concatenate([zp_ref[:, c0:c1], zc_ref[:, c0:c1], zn_ref[:, c0:c1]], axis=0).astype(F32)
        return jnp.where(row_ok, e, 0.0)

    def shifted(e, d):
        return e[HALO + d:HALO + d + tt]

    prod = ext(CONV_DIM, 2 * CONV_DIM) * ext(2 * CONV_DIM, 3 * CONV_DIM)
    cw = cw_ref[...]
    conv = shifted(prod, -1) * cw[0:1] + shifted(prod, 0) * cw[1:2] + shifted(prod, 1) * cw[2:3]
    o_ref[:, :CONV_DIM] = (zc_ref[:, :CONV_DIM].astype(F32) * conv).astype(o_ref.dtype)

    t = grow[HALO:HALO + tt]
    for g, win in enumerate(POOL_WINDOWS):
        lo = win // 2
        hi = win - 1 - lo
        c0 = 3 * CONV_DIM + g * POOL_GROUP_DIM
        e = ext(c0, c0 + POOL_GROUP_DIM)
        assert win & (win - 1) == 0 and win <= HALO
        run, step = e, 1
        while step < win:
            run = run + pltpu.roll(run, ext_rows - step, axis=0)
            step *= 2
        total = shifted(run, -lo)
        count = (jnp.minimum(t + hi, seq - 1) - jnp.maximum(t - lo, 0) + 1).astype(F32)
        pooled = total / count - shifted(e, 0)
        dg = _dot(pooled.astype(BF16), pw_ref[g].astype(BF16)) * ps_ref[:, g * POOL_GROUP_DIM:(g + 1) * POOL_GROUP_DIM]
        o_ref[:, CONV_DIM + g * POOL_GROUP_DIM:CONV_DIM + (g + 1) * POOL_GROUP_DIM] = dg.astype(o_ref.dtype)


def conv_pool(z, conv_w, pool_w, pool_scale, *, tt):
    s, w = z.shape
    nh = tt // HALO
    last_h = s // HALO - 1
    return pl.pallas_call(
        functools.partial(_convpool_kernel, seq=s, tt=tt),
        grid=(s // tt,),
        in_specs=[
            pl.BlockSpec((HALO, w), lambda i: (jnp.maximum(i * nh - 1, 0), 0)),
            pl.BlockSpec((tt, w), lambda i: (i, 0)),
            pl.BlockSpec((HALO, w), lambda i: (jnp.minimum((i + 1) * nh, last_h), 0)),
            pl.BlockSpec((3, CONV_DIM), lambda i: (0, 0)),
            pl.BlockSpec((POOL_GROUPS, POOL_GROUP_DIM, POOL_GROUP_DIM), lambda i: (0, 0, 0)),
            pl.BlockSpec((1, POOL_DIM), lambda i: (0, 0)),
        ],
        out_specs=pl.BlockSpec((tt, CONV_DIM + POOL_DIM), lambda i: (i, 0)),
        out_shape=jax.ShapeDtypeStruct((s, CONV_DIM + POOL_DIM), BF16),
        compiler_params=_cparams(("arbitrary",)),
        name="conv_pool",
    )(z, z, z, conv_w, pool_w, pool_scale.reshape(1, POOL_DIM))


def _excl_cumsum_lanes(m, upper):
    r, s = m.shape
    off = jnp.zeros((r, 1), F32)
    pieces = []
    for c in range(s // LANES):
        mc = m[:, c * LANES:(c + 1) * LANES]
        pieces.append(_dot(mc.astype(BF16), upper) + off)
        off = off + jnp.sum(mc, axis=1, keepdims=True)
    return jnp.concatenate(pieces, axis=1)


def _select_kernel(aff_ref, upper_ref, pos_ref, sel_ref, v_ref, *, cap):
    a = aff_ref[...]
    e, s = a.shape
    bits = pltpu.bitcast(a, I32)
    thr = jnp.zeros((e, 1), I32)
    for bit in range(30, -1, -1):
        cand = thr | (1 << bit)
        cnt = jnp.sum((bits >= cand).astype(F32), axis=1, keepdims=True)
        thr = jnp.where(cnt >= cap, cand, thr)
    gt = bits > thr
    eq = (bits == thr).astype(F32)
    need = cap - jnp.sum(gt.astype(F32), axis=1, keepdims=True)
    upper = upper_ref[...]
    eq_rank = _excl_cumsum_lanes(eq, upper)
    sel = jnp.where(gt | ((eq > 0.5) & (eq_rank < need)), 1.0, 0.0)
    pos = _excl_cumsum_lanes(sel, upper)
    pos_ref[...] = pos
    sel_ref[...] = sel
    tok = lax.broadcasted_iota(I32, (1, s), 1).astype(F32)
    tok_hi = jnp.floor(tok * (1.0 / LANES))
    tok_lo = tok - tok_hi * LANES
    g1 = a.astype(BF16).astype(F32)
    g2 = (a - g1).astype(BF16).astype(F32)
    g3 = a - g1 - g2
    zero = jnp.zeros((SUBLANES - 5, s), F32)
    for x in range(e):
        v_ref[x] = jnp.concatenate([tok_hi, tok_lo, g1[x:x + 1], g2[x:x + 1], g3[x:x + 1], zero], axis=0)


def select_tokens(aff, cap):
    e, s = aff.shape
    upper = jnp.asarray(np.triu(np.ones((LANES, LANES), np.float32), k=1), BF16)
    full2 = lambda shp: pl.BlockSpec(shp, lambda i: (0,) * len(shp))
    return pl.pallas_call(
        functools.partial(_select_kernel, cap=cap),
        grid=(1,),
        in_specs=[full2((e, s)), full2((LANES, LANES))],
        out_specs=[full2((e, s)), full2((e, s)), full2((e, SUBLANES, s))],
        out_shape=[
            jax.ShapeDtypeStruct((e, s), F32),
            jax.ShapeDtypeStruct((e, s), F32),
            jax.ShapeDtypeStruct((e, SUBLANES, s), F32),
        ],
        compiler_params=_cparams(("arbitrary",)),
        name="select_tokens",
    )(aff, upper)


def _lists_kernel(off_ref, cross_ref, pos_ref, sel_ref, v_ref, o_ref, acc, *, nchunk, ntile):
    e = pl.program_id(0)
    acc[...] = jnp.zeros(acc.shape, F32)
    slot = lax.broadcasted_iota(I32, (LANES, LANES), 0).astype(F32)

    def add_chunk(c, tile):
        st = pl.multiple_of(c * LANES, LANES)
        p = pos_ref[0, :, pl.ds(st, LANES)]
        chosen = sel_ref[0, :, pl.ds(st, LANES)] > 0.5
        v = v_ref[0, :, pl.ds(st, LANES)].astype(BF16)
        base = (tile * LANES).astype(F32)
        onehot = jnp.where(((p - base) == slot) & chosen, 1.0, 0.0).astype(BF16)
        acc[tile] += _dot_nt(v, onehot)

    def body(c, carry):
        add_chunk(c, jnp.minimum(off_ref[e * nchunk + c] // LANES, ntile - 1))
        return carry

    lax.fori_loop(0, nchunk, body, 0, unroll=8)
    for j in range(1, ntile):
        add_chunk(cross_ref[e * ntile + j], jnp.int32(j))
    for j in range(ntile):
        o_ref[0, :, j * LANES:(j + 1) * LANES] = acc[j]


def build_lists(pos, sel, vals, cap):
    e, s = pos.shape
    nchunk = s // LANES
    ntile = cap // LANES
    off = pos[:, ::LANES].astype(I32)
    bounds = jnp.arange(ntile, dtype=I32) * LANES
    cross = jnp.maximum(jnp.sum((off[:, :, None] < bounds[None, None, :]).astype(I32), axis=1) - 1, 0)
    row = lambda nrow: pl.BlockSpec((1, nrow, s), lambda x, t, u: (x, 0, 0))
    lists = pl.pallas_call(
        functools.partial(_lists_kernel, nchunk=nchunk, ntile=ntile),
        grid_spec=pltpu.PrefetchScalarGridSpec(
            num_scalar_prefetch=2,
            grid=(e,),
            in_specs=[row(1), row(1), row(SUBLANES)],
            out_specs=pl.BlockSpec((1, SUBLANES, cap), lambda x, t, u: (x, 0, 0)),
            scratch_shapes=[pltpu.VMEM((ntile, SUBLANES, LANES), F32)],
        ),
        out_shape=jax.ShapeDtypeStruct((e, SUBLANES, cap), F32),
        compiler_params=_cparams(("arbitrary",)),
        name="build_lists",
    )(off.reshape(-1), cross.reshape(-1), pos.reshape(e, 1, s), sel.reshape(e, 1, s), vals)
    idx = (lists[:, 0] * LANES + lists[:, 1]).astype(I32)
    gate = (lists[:, 2] + lists[:, 3] + lists[:, 4])
    return idx, gate


def _ffn_kernel(idx_ref, h_hbm, gate_ref, wg_ref, wu_ref, wd_ref, y_ref, xbuf, xb, hid, sem_g, *, cap, nh, ne):
    e = pl.program_id(0)
    hs = pl.program_id(1)
    slot = e % 2
    rps = cap // nh

    def gather_row(expert, r, s):
        t = idx_ref[expert * cap + r]
        return pltpu.make_async_copy(h_hbm.at[pl.ds(pl.multiple_of(t * SUBLANES, SUBLANES), SUBLANES)],
                                     xbuf.at[s, pl.ds(pl.multiple_of(r * SUBLANES, SUBLANES), SUBLANES)],
                                     sem_g.at[s])

    def wait_gather(s):
        pltpu.make_async_copy(h_hbm.at[pl.ds(0, cap * SUBLANES)], xbuf.at[s], sem_g.at[s]).wait()

    @pl.when((e == 0) & (hs == 0))
    def _():
        def issue(r, carry):
            gather_row(0, r, 0).start()
            return carry

        lax.fori_loop(0, cap, issue, 0, unroll=8)

    @pl.when(hs == 0)
    def _():
        wait_gather(slot)
        lo, hi = _unpack_bf16_pairs(_load_row_tiles(xbuf.at[slot]))
        xb[...] = jnp.concatenate([lo, hi], axis=1)

    th = wg_ref.shape[1]
    w_gu = jnp.concatenate([wg_ref[...].astype(BF16), wu_ref[...].astype(BF16)], axis=1)
    gu = _dot(xb[...], w_gu)
    gp = gu[:, :th]
    hid[:, pl.ds(pl.multiple_of(hs * th, th), th)] = (gp * (1.0 / (1.0 + jnp.exp(-gp))) * gu[:, th:]).astype(BF16)
    nxt = lax.rem(e + 1, ne)
    for k in range(rps):
        gather_row(nxt, hs * rps + k, 1 - slot).start()

    @pl.when(hs == nh - 1)
    def _():
        w_d = wd_ref[...].astype(BF16)
        rb = cap // nh
        for b in range(nh):
            rows = slice(b * rb, (b + 1) * rb)
            y = _dot(hid[rows, :], w_d) * gate_ref[0, rows, :]
            _store_row_tiles(y_ref, _pack_bf16_pairs(y), b * rb)

    @pl.when((e == ne - 1) & (hs == nh - 1))
    def _():
        wait_gather(1 - slot)


def expert_ffn(h, idx, gate, w_gate, w_up, w_down, li, *, th):
    ne, cap = idx.shape
    d = w_gate.shape[2]
    hidden = w_gate.shape[3]
    nh = hidden // th
    assert d == 2 * SUBLANES * LANES
    return pl.pallas_call(
        functools.partial(_ffn_kernel, cap=cap, nh=nh, ne=ne),
        grid_spec=pltpu.PrefetchScalarGridSpec(
            num_scalar_prefetch=1,
            grid=(ne, nh),
            in_specs=[
                pl.BlockSpec(memory_space=pl.ANY),
                pl.BlockSpec((1, cap, 1), lambda x, j, a: (x, 0, 0)),
                pl.BlockSpec((None, None, d, th), lambda x, j, a: (li, x, 0, j)),
                pl.BlockSpec((None, None, d, th), lambda x, j, a: (li, x, 0, j)),
                pl.BlockSpec((None, None, hidden, d), lambda x, j, a: (li, x, 0, 0)),
            ],
            out_specs=pl.BlockSpec((cap * SUBLANES, LANES), lambda x, j, a: (x, 0),
                                   pipeline_mode=pl.Buffered(1)),
            scratch_shapes=[
                pltpu.VMEM((2, cap * SUBLANES, LANES), I32),
                pltpu.VMEM((cap, d), BF16),
                pltpu.VMEM((cap, hidden), BF16),
                pltpu.SemaphoreType.DMA((2,)),
            ],
        ),
        out_shape=jax.ShapeDtypeStruct((ne * cap * SUBLANES, LANES), I32),
        compiler_params=_cparams(("arbitrary", "arbitrary")),
        name="expert_ffn",
    )(idx.reshape(-1), h, gate.reshape(ne, cap, 1), w_gate, w_up, w_down)


def _combine_kernel(off_ref, x_ref, pos_ref, g_ref, y_hbm, *rest, ne, cap, nchunk, final):
    nout = 1 if final else 2
    o_refs = rest[:nout]
    buf, obuf, acc, member, sem, osem = rest[nout:]
    c = pl.program_id(0)
    slot = c % 2
    win = COMBINE_WINDOW
    nrows = ne * cap
    half = acc.shape[1] // 2

    def chunk_off(e, cc):
        return off_ref[e * (nchunk + 1) + cc]

    def win_row(e, cc, p):
        return jnp.minimum(e * cap + chunk_off(e, cc) + p * win, nrows - win)

    def tiles(row, count):
        return pl.ds(pl.multiple_of(row * SUBLANES, SUBLANES), count * SUBLANES)

    def fetch(e, cc, s):
        return pltpu.make_async_copy(y_hbm.at[tiles(win_row(e, cc, 0), win)], buf.at[s, tiles(e * win, win)],
                                     sem.at[s])

    @pl.when(c == 0)
    def _():
        for e in range(ne):
            fetch(e, 0, 0).start()

    @pl.when(c + 1 < nchunk)
    def _():
        for e in range(ne):
            fetch(e, c + 1, 1 - slot).start()

    for e in range(ne):
        fetch(e, c, slot).wait()

    pos = pos_ref[...]

    def window_col(e, p):
        local = pos[:, e:e + 1] - chunk_off(e, c).astype(F32)
        col = pos[:, e:e + 1] + (e * cap - win_row(e, c, p)).astype(F32)
        ok = (local >= p * win) & (local < (p + 1) * win)
        return jnp.where(ok, col, -1.0)

    lane = lax.broadcasted_iota(I32, (1, LANES), 1)
    first = lane < win
    w = jnp.where(first, lane, lane - win).astype(F32)
    for q in range(ne // 2):
        col = jnp.where(first, window_col(2 * q, 0), window_col(2 * q + 1, 0))
        member[:, q * LANES:(q + 1) * LANES] = jnp.where(col == w, 1.0, 0.0).astype(BF16)
    a, b = _unpack_bf16_pairs(_load_row_tiles(buf.at[slot]))
    acc[:, :half] = _dot(member[...], a)
    acc[:, half:] = _dot(member[...], b)

    for e in range(ne):
        npieces = (chunk_off(e, c + 1) - chunk_off(e, c) + win - 1) // win

        def extra(p, carry, e=e):
            cp = pltpu.make_async_copy(y_hbm.at[tiles(win_row(e, c, p), win)], obuf, osem)
            cp.start()
            cp.wait()
            wi = lax.broadcasted_iota(I32, (1, win), 1).astype(F32)
            m = jnp.where(window_col(e, p) == wi, 1.0, 0.0).astype(BF16)
            oa, ob = _unpack_bf16_pairs(_load_row_tiles(obuf))
            acc[:, :half] += _dot(m, oa)
            acc[:, half:] += _dot(m, ob)
            return carry

        lax.fori_loop(1, npieces, extra, 0)

    y = x_ref[...] + acc[...]
    if final:
        o_refs[0][...] = _rmsnorm(y, g_ref[...])
    else:
        o_refs[0][...] = y
        o_refs[1][...] = _rmsnorm(y, g_ref[...]).astype(BF16)


def combine(x, y_rows, pos, sel, cap, g, *, tc, final):
    s, d = x.shape
    ne = pos.shape[0]
    nchunk = s // tc
    assert 2 * COMBINE_WINDOW == LANES and ne % 2 == 0 and ne * cap >= COMBINE_WINDOW
    off = jnp.concatenate([pos[:, ::tc].astype(I32), jnp.full((ne, 1), cap, I32)], axis=1).reshape(-1)
    row_block = pl.BlockSpec((tc, d), lambda c, t: (c, 0))
    if final:
        out_specs, out_shape = row_block, jax.ShapeDtypeStruct((s, d), F32)
    else:
        out_specs = [row_block, row_block]
        out_shape = [jax.ShapeDtypeStruct((s, d), F32), jax.ShapeDtypeStruct((s, d), BF16)]
    return pl.pallas_call(
        functools.partial(_combine_kernel, ne=ne, cap=cap, nchunk=nchunk, final=final),
        grid_spec=pltpu.PrefetchScalarGridSpec(
            num_scalar_prefetch=1,
            grid=(nchunk,),
            in_specs=[
                pl.BlockSpec((tc, d), lambda c, t: (c, 0)),
                pl.BlockSpec((tc, ne), lambda c, t: (c, 0)),
                pl.BlockSpec((1, d), lambda c, t: (0, 0)),
                pl.BlockSpec(memory_space=pl.ANY),
            ],
            out_specs=out_specs,
            scratch_shapes=[
                pltpu.VMEM((2, ne * COMBINE_WINDOW * SUBLANES, LANES), I32),
                pltpu.VMEM((COMBINE_WINDOW * SUBLANES, LANES), I32),
                pltpu.VMEM((tc, d), F32),
                pltpu.VMEM((tc, ne * COMBINE_WINDOW), BF16),
                pltpu.SemaphoreType.DMA((2,)),
                pltpu.SemaphoreType.DMA(()),
            ],
        ),
        out_shape=out_shape,
        compiler_params=_cparams(("arbitrary",)),
        name="combine",
    )(off, x, jnp.where(sel > 0.5, pos, -1.0).T, g.reshape(1, d), y_rows)


def ec_moe_block(x1, h, aff, w_gate, w_up, w_down, li, g_next, *, final, th, tc):
    s = x1.shape[0]
    cap = CAPACITY_FACTOR * s // N_EXPERTS
    pos, sel, vals = select_tokens(aff, cap)
    idx, gate = build_lists(pos, sel, vals, cap)
    y_rows = expert_ffn(h, idx, gate, w_gate, w_up, w_down, li, th=th)
    return combine(x1, y_rows, pos, sel, cap, g_next, tc=tc, final=final)


def _tiles(s):
    return dict(tm=min(1024, s), tn=1024, to=min(512, s), tt=min(512, s), th=256, tc=min(256, s))


def kernel(x, rel_bias, norm_mix_g, norm_ffn_g, final_norm_g, ev_w_in, ev_sink, ev_w_out, od_w_in, od_conv_w,
           od_pool_w, od_pool_scale, od_w_out, moe_w_router, moe_w_gate, moe_w_up, moe_w_down):
    b, s, d = x.shape
    assert b == 1
    t = _tiles(s)
    depth = norm_mix_g.shape[0]
    bias = window_bias(rel_bias)
    xs = x.reshape(s, d)
    hn = prenorm(xs, norm_mix_g[0], tm=t["tm"])
    for layer in range(depth):
        i = layer // 2
        if layer % 2 == 0:
            z = in_proj(hn, ev_w_in, i, tm=t["tm"], tn=t["tn"])
            parts = [windowed_attention(z, ev_sink[i], bias), fourier_mix(z, Q_DIM + 2 * KV_DIM)]
            w_out = ev_w_out
        else:
            z = in_proj(hn, od_w_in, i, tm=t["tm"], tn=t["tn"])
            parts = [conv_pool(z, od_conv_w[i], od_pool_w[i], od_pool_scale[i], tt=t["tt"])]
            w_out = od_w_out
        x1 = out_proj(xs, parts, w_out, i, tm=t["tm"], tn=t["tn"])
        h, aff = router(x1, norm_ffn_g[layer], moe_w_router[layer], tm=t["to"])
        final = layer == depth - 1
        g_next = final_norm_g if final else norm_mix_g[layer + 1]
        res = ec_moe_block(x1, h, aff.T, moe_w_gate, moe_w_up, moe_w_down, layer, g_next,
                           final=final, th=t["th"], tc=t["tc"])
        if final:
            xs = res
        else:
            xs, hn = res
    return xs.reshape(b, s, d)
```

```python
import functools
import math

import numpy as np
import jax
import jax.numpy as jnp
from jax import lax
from jax.experimental import pallas as pl
from jax.experimental.pallas import tpu as pltpu

F32 = jnp.float32
BF16 = jnp.bfloat16
I32 = jnp.int32

HEAD_DIM = 128
N_Q_HEADS = 12
N_KV_HEADS = 4
GQA_GROUP = N_Q_HEADS // N_KV_HEADS
WINDOW = 128
ATTN_BLOCK = 128
N_FOURIER_GROUPS = 4
FOURIER_GROUP_DIM = 128
Q_DIM = N_Q_HEADS * HEAD_DIM
KV_DIM = N_KV_HEADS * HEAD_DIM
FOURIER_DIM = N_FOURIER_GROUPS * FOURIER_GROUP_DIM
N_REL_BUCKETS = 32
REL_MAX_DISTANCE = 128
CONV_DIM = 1024
POOL_WINDOWS = (2, 4, 8, 16)
POOL_GROUPS = len(POOL_WINDOWS)
POOL_GROUP_DIM = 256
POOL_DIM = POOL_GROUPS * POOL_GROUP_DIM
N_EXPERTS = 16
CAPACITY_FACTOR = 2
RMS_EPS = 1e-6
NEG_INF = -1e30
LOG2E = math.log2(math.e)

LANES = 128
SUBLANES = 8
BF16_ROWS = 16
VMEM_LIMIT = 56 * 1024 * 1024

HALO = BF16_ROWS
COMBINE_WINDOW = 64
LIST_GROUP = 4


def _cparams(sem):
    return pltpu.CompilerParams(dimension_semantics=sem, vmem_limit_bytes=VMEM_LIMIT)


def _dot(a, b):
    return jnp.dot(a, b, preferred_element_type=F32)


def _dot_nt(a, b):
    return lax.dot_general(a, b, (((1,), (1,)), ((), ())), preferred_element_type=F32)


def _rmsnorm(x, g):
    ms = jnp.mean(x * x, axis=-1, keepdims=True)
    return x * lax.rsqrt(ms + RMS_EPS) * g


def _pack_bf16_pairs(y):
    n = y.shape[1] // 2
    lo = lax.bitcast_convert_type(y[:, :n].astype(BF16).astype(F32), I32)
    hi = lax.bitcast_convert_type(y[:, n:].astype(BF16).astype(F32), I32)
    return lax.shift_right_logical(lo, 16) | hi


def _unpack_bf16_pairs(p):
    lo = lax.bitcast_convert_type(lax.shift_left(p, 16), F32).astype(BF16)
    hi = lax.bitcast_convert_type(p & (-65536), F32).astype(BF16)
    return lo, hi


def _store_row_tiles(ref, packed, row0=0):
    rows = packed.shape[0]
    for j in range(SUBLANES):
        ref[pl.ds(row0 * SUBLANES + j, rows, stride=SUBLANES), :] = packed[:, j * LANES:(j + 1) * LANES]


def _load_row_tiles(ref):
    rows = ref.shape[0] // SUBLANES
    return jnp.concatenate([ref[pl.ds(j, rows, stride=SUBLANES), :] for j in range(SUBLANES)], axis=1)


def _prenorm_kernel(x_ref, g_ref, o_ref):
    o_ref[...] = _rmsnorm(x_ref[...], g_ref[...]).astype(o_ref.dtype)


def prenorm(x, g, *, tm):
    s, d = x.shape
    return pl.pallas_call(
        _prenorm_kernel,
        grid=(s // tm,),
        in_specs=[pl.BlockSpec((tm, d), lambda i: (i, 0)), pl.BlockSpec((1, d), lambda i: (0, 0))],
        out_specs=pl.BlockSpec((tm, d), lambda i: (i, 0)),
        out_shape=jax.ShapeDtypeStruct((s, d), BF16),
        compiler_params=_cparams(("parallel",)),
        name="prenorm",
    )(x, g.reshape(1, d))


def _in_proj_kernel(h_ref, w_ref, o_ref, wb_ref):
    @pl.when(pl.program_id(1) == 0)
    def _():
        wb_ref[...] = w_ref[...].astype(BF16)

    o_ref[...] = _dot(h_ref[...], wb_ref[...]).astype(o_ref.dtype)


def in_proj(h, w, li, *, tm, tn):
    s, d = h.shape
    n = w.shape[2]
    return pl.pallas_call(
        _in_proj_kernel,
        grid=(n // tn, s // tm),
        in_specs=[
            pl.BlockSpec((tm, d), lambda j, i: (i, 0)),
            pl.BlockSpec((None, d, tn), lambda j, i: (li, 0, j)),
        ],
        out_specs=pl.BlockSpec((tm, tn), lambda j, i: (i, j)),
        out_shape=jax.ShapeDtypeStruct((s, n), BF16),
        scratch_shapes=[pltpu.VMEM((d, tn), BF16)],
        compiler_params=_cparams(("parallel", "arbitrary")),
        name="in_proj",
    )(h, w)


def _out_proj_kernel(*refs, widths):
    nparts = len(widths)
    x_ref = refs[0]
    p_refs = refs[1:1 + nparts]
    w_ref, o_ref, wb_ref = refs[1 + nparts:]

    @pl.when(pl.program_id(1) == 0)
    def _():
        wb_ref[...] = w_ref[...].astype(BF16)

    acc = x_ref[...]
    off = 0
    for p_ref, width in zip(p_refs, widths):
        acc = acc + _dot(p_ref[...].astype(BF16), wb_ref[off:off + width, :])
        off += width
    o_ref[...] = acc


def out_proj(x, parts, w, li, *, tm, tn):
    s, d = x.shape
    k = w.shape[1]
    widths = tuple(p.shape[1] for p in parts)
    assert sum(widths) == k
    in_specs = [pl.BlockSpec((tm, tn), lambda j, i: (i, j))]
    in_specs += [pl.BlockSpec((tm, wd), lambda j, i: (i, 0)) for wd in widths]
    in_specs += [pl.BlockSpec((None, k, tn), lambda j, i: (li, 0, j))]
    return pl.pallas_call(
        functools.partial(_out_proj_kernel, widths=widths),
        grid=(d // tn, s // tm),
        in_specs=in_specs,
        out_specs=pl.BlockSpec((tm, tn), lambda j, i: (i, j)),
        out_shape=jax.ShapeDtypeStruct((s, d), F32),
        scratch_shapes=[pltpu.VMEM((k, tn), BF16)],
        compiler_params=_cparams(("parallel", "arbitrary")),
        name="out_proj",
    )(x, *parts, w)


def _split_bf16(v):
    hi = v.astype(BF16)
    return hi, (v - hi.astype(F32)).astype(BF16)


def _router_kernel(x_ref, g_ref, wr_ref, h_ref, aff_ref):
    h = _rmsnorm(x_ref[...], g_ref[...])
    _store_row_tiles(h_ref, _pack_bf16_pairs(h))
    h_hi, h_lo = _split_bf16(h)
    w_hi, w_lo = _split_bf16(wr_ref[...])
    ne = w_hi.shape[1]
    hh = _dot(h_hi, jnp.concatenate([w_hi, w_lo], axis=1))
    logits = hh[:, :ne] + (hh[:, ne:] + _dot(h_lo, w_hi))
    m = jnp.max(logits, axis=1, keepdims=True)
    p = jnp.exp(logits - m)
    aff_ref[...] = p / jnp.sum(p, axis=1, keepdims=True)


def router(x, g, w_router, *, tm):
    s, d = x.shape
    assert d == 2 * SUBLANES * LANES
    e = w_router.shape[1]
    return pl.pallas_call(
        _router_kernel,
        grid=(s // tm,),
        in_specs=[
            pl.BlockSpec((tm, d), lambda i: (i, 0)),
            pl.BlockSpec((1, d), lambda i: (0, 0)),
            pl.BlockSpec((d, e), lambda i: (0, 0)),
        ],
        out_specs=[pl.BlockSpec((tm * SUBLANES, LANES), lambda i: (i, 0)), pl.BlockSpec((tm, e), lambda i: (i, 0))],
        out_shape=[jax.ShapeDtypeStruct((s * SUBLANES, LANES), I32), jax.ShapeDtypeStruct((s, e), F32)],
        compiler_params=_cparams(("parallel",)),
        name="router",
    )(x, g.reshape(1, d), w_router)


def _t5_bucket(rel):
    nb = N_REL_BUCKETS // 2
    max_exact = nb // 2
    ret = (rel > 0).astype(jnp.int32) * nb
    n = jnp.abs(rel)
    nf = jnp.maximum(n, 1).astype(jnp.float32)
    large = max_exact + (jnp.log(nf / max_exact) / math.log(REL_MAX_DISTANCE / max_exact)
                         * (nb - max_exact)).astype(jnp.int32)
    large = jnp.minimum(large, nb - 1)
    return ret + jnp.where(n < max_exact, n, large)


def _bias_kernel(tab_ref, bucket_ref, o_ref):
    h = pl.program_id(0)
    bucket = bucket_ref[...]
    acc = jnp.zeros(bucket.shape, F32)
    for b in range(N_REL_BUCKETS):
        acc = jnp.where(bucket == b, tab_ref[b * N_Q_HEADS + h], acc)
    i = lax.broadcasted_iota(I32, bucket.shape, 0)
    j = lax.broadcasted_iota(I32, bucket.shape, 1)
    valid = jnp.abs(j - WINDOW - i) <= WINDOW
    o_ref[0] = jnp.where(valid, acc * LOG2E, NEG_INF)


def window_bias(rel_bias):
    i = jnp.arange(ATTN_BLOCK, dtype=jnp.int32)[:, None]
    j = jnp.arange(3 * ATTN_BLOCK, dtype=jnp.int32)[None, :]
    bucket = _t5_bucket((j - WINDOW) - i)
    return pl.pallas_call(
        _bias_kernel,
        grid_spec=pltpu.PrefetchScalarGridSpec(
            num_scalar_prefetch=1,
            grid=(N_Q_HEADS,),
            in_specs=[pl.BlockSpec((ATTN_BLOCK, 3 * ATTN_BLOCK), lambda h, t: (0, 0))],
            out_specs=pl.BlockSpec((1, ATTN_BLOCK, 3 * ATTN_BLOCK), lambda h, t: (h, 0, 0)),
        ),
        out_shape=jax.ShapeDtypeStruct((N_Q_HEADS, ATTN_BLOCK, 3 * ATTN_BLOCK), F32),
        compiler_params=_cparams(("arbitrary",)),
        name="window_bias",
    )(rel_bias.reshape(-1), bucket)


def _attn_kernel(sink_ref, q_ref, kp_ref, kc_ref, kn_ref, vp_ref, vc_ref, vn_ref, bias_ref, o_ref, *, nb):
    n = pl.program_id(0)
    blk = ATTN_BLOCK
    ones = jnp.ones((3 * blk, HEAD_DIM), BF16)
    scale2 = HEAD_DIM ** -0.5 * LOG2E

    def block(at_edge):
        if at_edge:
            col = lax.broadcasted_iota(I32, (1, 3 * blk), 1)
            in_seq = ((col >= blk) | (n > 0)) & ((col < 2 * blk) | (n < nb - 1))
            edge_bias = jnp.where(in_seq, 0.0, NEG_INF)
        for kv in range(N_KV_HEADS):
            cs = slice(kv * HEAD_DIM, (kv + 1) * HEAD_DIM)
            k = jnp.concatenate([kp_ref[:, cs], kc_ref[:, cs], kn_ref[:, cs]], axis=0)
            v = jnp.concatenate([vp_ref[:, cs], vc_ref[:, cs], vn_ref[:, cs]], axis=0)
            v1 = jnp.concatenate([v, ones], axis=1)
            for g in range(GQA_GROUP):
                hq = kv * GQA_GROUP + g
                hs = slice(hq * HEAD_DIM, (hq + 1) * HEAD_DIM)
                s2 = _dot_nt(q_ref[:, hs], k) * scale2 + bias_ref[kv, g * blk:(g + 1) * blk, :]
                if at_edge:
                    s2 = s2 + edge_bias
                sk2 = sink_ref[hq] * LOG2E
                m2 = jnp.maximum(jnp.max(s2, axis=-1, keepdims=True), sk2)
                p = jnp.exp2(s2 - m2).astype(BF16)
                ov = _dot(p, v1)
                denom = ov[:, HEAD_DIM:HEAD_DIM + 1] + jnp.exp2(sk2 - m2)
                o_ref[:, hs] = (ov[:, :HEAD_DIM] / denom).astype(o_ref.dtype)

    @pl.when((n > 0) & (n < nb - 1))
    def _():
        block(False)

    @pl.when((n == 0) | (n == nb - 1))
    def _():
        block(True)


def windowed_attention(z, sink, bias):
    s = z.shape[0]
    blk = ATTN_BLOCK
    nb = s // blk
    kcol = Q_DIM // KV_DIM
    vcol = kcol + 1

    def prev(n, t):
        return jnp.maximum(n - 1, 0)

    def nxt(n, t):
        return jnp.minimum(n + 1, nb - 1)

    in_specs = [
        pl.BlockSpec((blk, Q_DIM), lambda n, t: (n, 0)),
        pl.BlockSpec((blk, KV_DIM), lambda n, t: (prev(n, t), kcol)),
        pl.BlockSpec((blk, KV_DIM), lambda n, t: (n, kcol)),
        pl.BlockSpec((blk, KV_DIM), lambda n, t: (nxt(n, t), kcol)),
        pl.BlockSpec((blk, KV_DIM), lambda n, t: (prev(n, t), vcol)),
        pl.BlockSpec((blk, KV_DIM), lambda n, t: (n, vcol)),
        pl.BlockSpec((blk, KV_DIM), lambda n, t: (nxt(n, t), vcol)),
        pl.BlockSpec((N_KV_HEADS, GQA_GROUP * blk, 3 * blk), lambda n, t: (0, 0, 0)),
    ]
    return pl.pallas_call(
        functools.partial(_attn_kernel, nb=nb),
        grid_spec=pltpu.PrefetchScalarGridSpec(
            num_scalar_prefetch=1,
            grid=(nb,),
            in_specs=in_specs,
            out_specs=pl.BlockSpec((blk, Q_DIM), lambda n, t: (n, 0)),
        ),
        out_shape=jax.ShapeDtypeStruct((s, Q_DIM), BF16),
        compiler_params=_cparams(("arbitrary",)),
        name="windowed_attention",
    )(sink, z, z, z, z, z, z, z, bias.reshape(N_KV_HEADS, GQA_GROUP * blk, 3 * blk))


def _fourier_tables(s):
    n2 = LANES
    n1 = s // n2
    k1 = np.arange(n1)[None, :, None]
    s1 = np.arange(n1)[None, None, :]
    s2 = np.arange(n2)[:, None, None]
    ang = 2.0 * np.pi * ((k1 * (s2 + n2 * s1)) % s) / s
    ma = np.concatenate([np.cos(ang), -np.sin(ang)], axis=1)
    a = 2.0 * np.pi * ((np.arange(n2)[:, None] * np.arange(n2)[None, :]) % n2) / n2
    c, sn = np.cos(a), np.sin(a)
    mc = np.block([[c, sn], [-sn, c]])
    scale = 1.0 / math.sqrt(s * FOURIER_GROUP_DIM)
    md = np.concatenate([c, sn], axis=0) * scale
    return (jnp.asarray(ma, BF16), jnp.asarray(mc, BF16), jnp.asarray(md, BF16))


FOURIER_K1_BATCH = 4


def _fourier_kernel(x_ref, ma_ref, mc_ref, md_ref, o_ref, xf, yr, yi, ot, *, n1):
    n2 = LANES
    pitch = n2 + SUBLANES
    pitch_o = n1 + SUBLANES
    kb = FOURIER_K1_BATCH
    for s1 in range(n1):
        xf[s1 * pitch:s1 * pitch + n2, :] = x_ref[s1 * n2:(s1 + 1) * n2, :].astype(F32)

    def stage_a(s2, carry):
        xs = xf[pl.ds(s2, n1, stride=pitch), :].astype(BF16)
        y = _dot(ma_ref[s2], xs)
        yr[pl.ds(s2, n1, stride=pitch), :] = y[:n1]
        yi[pl.ds(s2, n1, stride=pitch), :] = y[n1:]
        return carry

    lax.fori_loop(0, n2, stage_a, 0, unroll=8)

    def stage_c(b, carry):
        def slab(ref, j):
            return ref[pl.ds(pl.multiple_of((b * kb + j) * pitch, SUBLANES), n2), :]

        y = jnp.concatenate(
            [jnp.concatenate([slab(yr, j), slab(yi, j)], axis=0) for j in range(kb)], axis=1).astype(BF16)
        xc = _dot(mc_ref[...], y)
        xx = jnp.concatenate(
            [jnp.concatenate([xc[:n2, j * LANES:(j + 1) * LANES], xc[n2:, j * LANES:(j + 1) * LANES]], axis=1)
             for j in range(kb)], axis=0).astype(BF16)
        out = _dot(xx, md_ref[...])
        for j in range(kb):
            ot[pl.ds(b * kb + j, n2, stride=pitch_o), :] = out[j * n2:(j + 1) * n2]
        return carry

    lax.fori_loop(0, n1 // kb, stage_c, 0, unroll=2)
    for k2 in range(n2):
        o_ref[k2 * n1:(k2 + 1) * n1, :] = ot[k2 * pitch_o:k2 * pitch_o + n1, :]


def fourier_mix(z, col0):
    s, w = z.shape
    n2 = LANES
    n1 = s // n2
    assert n1 % FOURIER_K1_BATCH == 0 and n1 % SUBLANES == 0
    c = FOURIER_GROUP_DIM
    ma, mc, md = _fourier_tables(s)
    cb0 = col0 // c
    slabs = pltpu.VMEM((n1 * (n2 + SUBLANES), c), F32)
    return pl.pallas_call(
        functools.partial(_fourier_kernel, n1=n1),
        grid=(N_FOURIER_GROUPS,),
        in_specs=[
            pl.BlockSpec((s, c), lambda g: (0, cb0 + g)),
            pl.BlockSpec((n2, 2 * n1, n1), lambda g: (0, 0, 0)),
            pl.BlockSpec((2 * n2, 2 * n2), lambda g: (0, 0)),
            pl.BlockSpec((2 * c, c), lambda g: (0, 0)),
        ],
        out_specs=pl.BlockSpec((s, c), lambda g: (0, g)),
        out_shape=jax.ShapeDtypeStruct((s, FOURIER_DIM), F32),
        scratch_shapes=[slabs, slabs, slabs, pltpu.VMEM((n2 * (n1 + SUBLANES), c), F32)],
        compiler_params=_cparams(("arbitrary",)),
        name="fourier_mix",
    )(z, ma, mc, md)


def _convpool_kernel(zp_ref, zc_ref, zn_ref, cw_ref, pw_ref, ps_ref, o_ref, *, seq, tt):
    i = pl.program_id(0)
    ext_rows = tt + 2 * HALO
    grow = i * tt - HALO + lax.broadcasted_iota(I32, (ext_rows, 1), 0)
    row_ok = (grow >= 0) & (grow < seq)

    def ext(c0, c1):
        e = jnp.concatenate([zp_ref[:, c0:c1], zc_ref[:, c0:c1], zn_ref[:, c0:c1]], axis=0).astype(F32)
        return jnp.where(row_ok, e, 0.0)

    def shifted(e, d):
        return e[HALO + d:HALO + d + tt]

    prod = ext(CONV_DIM, 2 * CONV_DIM) * ext(2 * CONV_DIM, 3 * CONV_DIM)
    cw = cw_ref[...]
    conv = shifted(prod, -1) * cw[0:1] + shifted(prod, 0) * cw[1:2] + shifted(prod, 1) * cw[2:3]
    o_ref[:, :CONV_DIM] = (zc_ref[:, :CONV_DIM].astype(F32) * conv).astype(o_ref.dtype)

    t = grow[HALO:HALO + tt]
    for g, win in enumerate(POOL_WINDOWS):
        lo = win // 2
        hi = win - 1 - lo
        c0 = 3 * CONV_DIM + g * POOL_GROUP_DIM
        e = ext(c0, c0 + POOL_GROUP_DIM)
        assert win & (win - 1) == 0 and win <= HALO
        run, step = e, 1
        while step < win:
            run = run + pltpu.roll(run, ext_rows - step, axis=0)
            step *= 2
        total = shifted(run, -lo)
        count = (jnp.minimum(t + hi, seq - 1) - jnp.maximum(t - lo, 0) + 1).astype(F32)
        pooled = total / count - shifted(e, 0)
        dg = _dot(pooled.astype(BF16), pw_ref[g].astype(BF16)) * ps_ref[:, g * POOL_GROUP_DIM:(g + 1) * POOL_GROUP_DIM]
        o_ref[:, CONV_DIM + g * POOL_GROUP_DIM:CONV_DIM + (g + 1) * POOL_GROUP_DIM] = dg.astype(o_ref.dtype)


def conv_pool(z, conv_w, pool_w, pool_scale, *, tt):
    s, w = z.shape
    nh = tt // HALO
    last_h = s // HALO - 1
    return pl.pallas_call(
        functools.partial(_convpool_kernel, seq=s, tt=tt),
        grid=(s // tt,),
        in_specs=[
            pl.BlockSpec((HALO, w), lambda i: (jnp.maximum(i * nh - 1, 0), 0)),
            pl.BlockSpec((tt, w), lambda i: (i, 0)),
            pl.BlockSpec((HALO, w), lambda i: (jnp.minimum((i + 1) * nh, last_h), 0)),
            pl.BlockSpec((3, CONV_DIM), lambda i: (0, 0)),
            pl.BlockSpec((POOL_GROUPS, POOL_GROUP_DIM, POOL_GROUP_DIM), lambda i: (0, 0, 0)),
            pl.BlockSpec((1, POOL_DIM), lambda i: (0, 0)),
        ],
        out_specs=pl.BlockSpec((tt, CONV_DIM + POOL_DIM), lambda i: (i, 0)),
        out_shape=jax.ShapeDtypeStruct((s, CONV_DIM + POOL_DIM), BF16),
        compiler_params=_cparams(("arbitrary",)),
        name="conv_pool",
    )(z, z, z, conv_w, pool_w, pool_scale.reshape(1, POOL_DIM))


def _excl_cumsum_lanes(m, upper):
    r, s = m.shape
    off = jnp.zeros((r, 1), F32)
    pieces = []
    for c in range(s // LANES):
        mc = m[:, c * LANES:(c + 1) * LANES]
        pieces.append(_dot(mc.astype(BF16), upper) + off)
        off = off + jnp.sum(mc, axis=1, keepdims=True)
    return jnp.concatenate(pieces, axis=1)


def _select_kernel(aff_ref, upper_ref, pos_ref, sel_ref, v_ref, *, cap):
    a = aff_ref[...]
    e, s = a.shape
    bits = pltpu.bitcast(a, I32)
    thr = jnp.zeros((e, 1), I32)
    for bit in range(30, -1, -1):
        cand = thr | (1 << bit)
        cnt = jnp.sum((bits >= cand).astype(F32), axis=1, keepdims=True)
        thr = jnp.where(cnt >= cap, cand, thr)
    gt = bits > thr
    eq = (bits == thr).astype(F32)
    need = cap - jnp.sum(gt.astype(F32), axis=1, keepdims=True)
    upper = upper_ref[...]
    eq_rank = _excl_cumsum_lanes(eq, upper)
    sel = jnp.where(gt | ((eq > 0.5) & (eq_rank < need)), 1.0, 0.0)
    pos = _excl_cumsum_lanes(sel, upper)
    pos_ref[...] = pos
    sel_ref[...] = sel
    tok = lax.broadcasted_iota(I32, (1, s), 1).astype(F32)
    tok_hi = jnp.floor(tok * (1.0 / LANES))
    tok_lo = tok - tok_hi * LANES
    g1 = a.astype(BF16).astype(F32)
    g2 = (a - g1).astype(BF16).astype(F32)
    g3 = a - g1 - g2
    zero = jnp.zeros((SUBLANES - 5, s), F32)
    for x in range(e):
        v_ref[x] = jnp.concatenate([tok_hi, tok_lo, g1[x:x + 1], g2[x:x + 1], g3[x:x + 1], zero], axis=0)


def select_tokens(aff, cap):
    e, s = aff.shape
    upper = jnp.asarray(np.triu(np.ones((LANES, LANES), np.float32), k=1), BF16)
    full2 = lambda shp: pl.BlockSpec(shp, lambda i: (0,) * len(shp))
    return pl.pallas_call(
        functools.partial(_select_kernel, cap=cap),
        grid=(1,),
        in_specs=[full2((e, s)), full2((LANES, LANES))],
        out_specs=[full2((e, s)), full2((e, s)), full2((e, SUBLANES, s))],
        out_shape=[
            jax.ShapeDtypeStruct((e, s), F32),
            jax.ShapeDtypeStruct((e, s), F32),
            jax.ShapeDtypeStruct((e, SUBLANES, s), F32),
        ],
        compiler_params=_cparams(("arbitrary",)),
        name="select_tokens",
    )(aff, upper)


def _lists_kernel(off_ref, cross_ref, pos_ref, sel_ref, v_ref, o_ref, acc, *, nchunk, ntile, group):
    e0 = pl.program_id(0) * group
    acc[...] = jnp.zeros(acc.shape, F32)
    slot = lax.broadcasted_iota(I32, (LANES, LANES), 0).astype(F32)

    def add_chunk(x, c, tile):
        st = pl.multiple_of(c * LANES, LANES)
        p = pos_ref[x, :, pl.ds(st, LANES)]
        chosen = sel_ref[x, :, pl.ds(st, LANES)] > 0.5
        v = v_ref[x, :, pl.ds(st, LANES)].astype(BF16)
        base = (tile * LANES).astype(F32)
        onehot = jnp.where(((p - base) == slot) & chosen, 1.0, 0.0).astype(BF16)
        acc[x, tile] += _dot_nt(v, onehot)

    def body(c, carry):
        for x in range(group):
            add_chunk(x, c, jnp.minimum(off_ref[(e0 + x) * nchunk + c] // LANES, ntile - 1))
        return carry

    lax.fori_loop(0, nchunk, body, 0, unroll=4)
    for j in range(1, ntile):
        for x in range(group):
            add_chunk(x, cross_ref[(e0 + x) * ntile + j], jnp.int32(j))
    for x in range(group):
        for j in range(ntile):
            o_ref[x, :, j * LANES:(j + 1) * LANES] = acc[x, j]


def build_lists(pos, sel, vals, cap):
    e, s = pos.shape
    nchunk = s // LANES
    ntile = cap // LANES
    off = pos[:, ::LANES].astype(I32)
    bounds = jnp.arange(ntile, dtype=I32) * LANES
    cross = jnp.maximum(jnp.sum((off[:, :, None] < bounds[None, None, :]).astype(I32), axis=1) - 1, 0)
    group = LIST_GROUP
    assert e % group == 0
    row = lambda nrow: pl.BlockSpec((group, nrow, s), lambda x, t, u: (x, 0, 0))
    lists = pl.pallas_call(
        functools.partial(_lists_kernel, nchunk=nchunk, ntile=ntile, group=group),
        grid_spec=pltpu.PrefetchScalarGridSpec(
            num_scalar_prefetch=2,
            grid=(e // group,),
            in_specs=[row(1), row(1), row(SUBLANES)],
            out_specs=pl.BlockSpec((group, SUBLANES, cap), lambda x, t, u: (x, 0, 0)),
            scratch_shapes=[pltpu.VMEM((group, ntile, SUBLANES, LANES), F32)],
        ),
        out_shape=jax.ShapeDtypeStruct((e, SUBLANES, cap), F32),
        compiler_params=_cparams(("arbitrary",)),
        name="build_lists",
    )(off.reshape(-1), cross.reshape(-1), pos.reshape(e, 1, s), sel.reshape(e, 1, s), vals)
    idx = (lists[:, 0] * LANES + lists[:, 1]).astype(I32)
    gate = (lists[:, 2] + lists[:, 3] + lists[:, 4])
    return idx, gate


def _ffn_kernel(idx_ref, h_hbm, gate_ref, wg_ref, wu_ref, wd_ref, y_ref, xbuf, xb, hid, sem_g, *, cap, nh, ne):
    e = pl.program_id(0)
    hs = pl.program_id(1)
    slot = e % 2
    rps = cap // nh

    def gather_row(expert, r, s):
        t = idx_ref[expert * cap + r]
        return pltpu.make_async_copy(h_hbm.at[pl.ds(pl.multiple_of(t * SUBLANES, SUBLANES), SUBLANES)],
                                     xbuf.at[s, pl.ds(pl.multiple_of(r * SUBLANES, SUBLANES), SUBLANES)],
                                     sem_g.at[s])

    def wait_gather(s):
        pltpu.make_async_copy(h_hbm.at[pl.ds(0, cap * SUBLANES)], xbuf.at[s], sem_g.at[s]).wait()

    @pl.when((e == 0) & (hs == 0))
    def _():
        def issue(r, carry):
            gather_row(0, r, 0).start()
            return carry

        lax.fori_loop(0, cap, issue, 0, unroll=8)

    @pl.when(hs == 0)
    def _():
        wait_gather(slot)
        lo, hi = _unpack_bf16_pairs(_load_row_tiles(xbuf.at[slot]))
        xb[...] = jnp.concatenate([lo, hi], axis=1)

    th = wg_ref.shape[1]
    w_gu = jnp.concatenate([wg_ref[...].astype(BF16), wu_ref[...].astype(BF16)], axis=1)
    gu = _dot(xb[...], w_gu)
    gp = gu[:, :th]
    hid[:, pl.ds(pl.multiple_of(hs * th, th), th)] = (gp * (1.0 / (1.0 + jnp.exp(-gp))) * gu[:, th:]).astype(BF16)
    nxt = lax.rem(e + 1, ne)
    for k in range(rps):
        gather_row(nxt, hs * rps + k, 1 - slot).start()

    @pl.when(hs == nh - 1)
    def _():
        w_d = wd_ref[...].astype(BF16)
        rb = cap // nh
        for b in range(nh):
            rows = slice(b * rb, (b + 1) * rb)
            y = _dot(hid[rows, :], w_d) * gate_ref[0, rows, :]
            _store_row_tiles(y_ref, _pack_bf16_pairs(y), b * rb)

    @pl.when((e == ne - 1) & (hs == nh - 1))
    def _():
        wait_gather(1 - slot)


def expert_ffn(h, idx, gate, w_gate, w_up, w_down, li, *, th):
    ne, cap = idx.shape
    d = w_gate.shape[2]
    hidden = w_gate.shape[3]
    nh = hidden // th
    assert d == 2 * SUBLANES * LANES
    return pl.pallas_call(
        functools.partial(_ffn_kernel, cap=cap, nh=nh, ne=ne),
        grid_spec=pltpu.PrefetchScalarGridSpec(
            num_scalar_prefetch=1,
            grid=(ne, nh),
            in_specs=[
                pl.BlockSpec(memory_space=pl.ANY),
                pl.BlockSpec((1, cap, 1), lambda x, j, a: (x, 0, 0)),
                pl.BlockSpec((None, None, d, th), lambda x, j, a: (li, x, 0, j)),
                pl.BlockSpec((None, None, d, th), lambda x, j, a: (li, x, 0, j)),
                pl.BlockSpec((None, None, hidden, d), lambda x, j, a: (li, x, 0, 0)),
            ],
            out_specs=pl.BlockSpec((cap * SUBLANES, LANES), lambda x, j, a: (x, 0),
                                   pipeline_mode=pl.Buffered(1)),
            scratch_shapes=[
                pltpu.VMEM((2, cap * SUBLANES, LANES), I32),
                pltpu.VMEM((cap, d), BF16),
                pltpu.VMEM((cap, hidden), BF16),
                pltpu.SemaphoreType.DMA((2,)),
            ],
        ),
        out_shape=jax.ShapeDtypeStruct((ne * cap * SUBLANES, LANES), I32),
        compiler_params=_cparams(("arbitrary", "arbitrary")),
        name="expert_ffn",
    )(idx.reshape(-1), h, gate.reshape(ne, cap, 1), w_gate, w_up, w_down)


def _combine_kernel(off_ref, x_ref, pos_ref, g_ref, y_hbm, *rest, ne, cap, nchunk, final):
    nout = 1 if final else 2
    o_refs = rest[:nout]
    buf, obuf, acc, member, sem, osem = rest[nout:]
    c = pl.program_id(0)
    slot = c % 2
    win = COMBINE_WINDOW
    nrows = ne * cap
    half = acc.shape[1] // 2

    def chunk_off(e, cc):
        return off_ref[e * (nchunk + 1) + cc]

    def win_row(e, cc, p):
        return jnp.minimum(e * cap + chunk_off(e, cc) + p * win, nrows - win)

    def tiles(row, count):
        return pl.ds(pl.multiple_of(row * SUBLANES, SUBLANES), count * SUBLANES)

    def fetch(e, cc, s):
        return pltpu.make_async_copy(y_hbm.at[tiles(win_row(e, cc, 0), win)], buf.at[s, tiles(e * win, win)],
                                     sem.at[s])

    @pl.when(c == 0)
    def _():
        for e in range(ne):
            fetch(e, 0, 0).start()

    @pl.when(c + 1 < nchunk)
    def _():
        for e in range(ne):
            fetch(e, c + 1, 1 - slot).start()

    for e in range(ne):
        fetch(e, c, slot).wait()

    pos = pos_ref[...]

    def window_col(e, p):
        local = pos[:, e:e + 1] - chunk_off(e, c).astype(F32)
        col = pos[:, e:e + 1] + (e * cap - win_row(e, c, p)).astype(F32)
        ok = (local >= p * win) & (local < (p + 1) * win)
        return jnp.where(ok, col, -1.0)

    lane = lax.broadcasted_iota(I32, (1, LANES), 1)
    first = lane < win
    w = jnp.where(first, lane, lane - win).astype(F32)
    for q in range(ne // 2):
        col = jnp.where(first, window_col(2 * q, 0), window_col(2 * q + 1, 0))
        member[:, q * LANES:(q + 1) * LANES] = jnp.where(col == w, 1.0, 0.0).astype(BF16)
    a, b = _unpack_bf16_pairs(_load_row_tiles(buf.at[slot]))
    acc[:, :half] = _dot(member[...], a)
    acc[:, half:] = _dot(member[...], b)

    for e in range(ne):
        npieces = (chunk_off(e, c + 1) - chunk_off(e, c) + win - 1) // win

        def extra(p, carry, e=e):
            cp = pltpu.make_async_copy(y_hbm.at[tiles(win_row(e, c, p), win)], obuf, osem)
            cp.start()
            cp.wait()
            wi = lax.broadcasted_iota(I32, (1, win), 1).astype(F32)
            m = jnp.where(window_col(e, p) == wi, 1.0, 0.0).astype(BF16)
            oa, ob = _unpack_bf16_pairs(_load_row_tiles(obuf))
            acc[:, :half] += _dot(m, oa)
            acc[:, half:] += _dot(m, ob)
            return carry

        lax.fori_loop(1, npieces, extra, 0)

    y = x_ref[...] + acc[...]
    if final:
        o_refs[0][...] = _rmsnorm(y, g_ref[...])
    else:
        o_refs[0][...] = y
        o_refs[1][...] = _rmsnorm(y, g_ref[...]).astype(BF16)


def combine(x, y_rows, pos, sel, cap, g, *, tc, final):
    s, d = x.shape
    ne = pos.shape[0]
    nchunk = s // tc
    assert 2 * COMBINE_WINDOW == LANES and ne % 2 == 0 and ne * cap >= COMBINE_WINDOW
    off = jnp.concatenate([pos[:, ::tc].astype(I32), jnp.full((ne, 1), cap, I32)], axis=1).reshape(-1)
    row_block = pl.BlockSpec((tc, d), lambda c, t: (c, 0))
    if final:
        out_specs, out_shape = row_block, jax.ShapeDtypeStruct((s, d), F32)
    else:
        out_specs = [row_block, row_block]
        out_shape = [jax.ShapeDtypeStruct((s, d), F32), jax.ShapeDtypeStruct((s, d), BF16)]
    return pl.pallas_call(
        functools.partial(_combine_kernel, ne=ne, cap=cap, nchunk=nchunk, final=final),
        grid_spec=pltpu.PrefetchScalarGridSpec(
            num_scalar_prefetch=1,
            grid=(nchunk,),
            in_specs=[
                pl.BlockSpec((tc, d), lambda c, t: (c, 0)),
                pl.BlockSpec((tc, ne), lambda c, t: (c, 0)),
                pl.BlockSpec((1, d), lambda c, t: (0, 0)),
                pl.BlockSpec(memory_space=pl.ANY),
            ],
            out_specs=out_specs,
            scratch_shapes=[
                pltpu.VMEM((2, ne * COMBINE_WINDOW * SUBLANES, LANES), I32),
                pltpu.VMEM((COMBINE_WINDOW * SUBLANES, LANES), I32),
                pltpu.VMEM((tc, d), F32),
                pltpu.VMEM((tc, ne * COMBINE_WINDOW), BF16),
                pltpu.SemaphoreType.DMA((2,)),
                pltpu.SemaphoreType.DMA(()),
            ],
        ),
        out_shape=out_shape,
        compiler_params=_cparams(("arbitrary",)),
        name="combine",
    )(off, x, jnp.where(sel > 0.5, pos, -1.0).T, g.reshape(1, d), y_rows)


def ec_moe_block(x1, h, aff, w_gate, w_up, w_down, li, g_next, *, final, th, tc):
    s = x1.shape[0]
    cap = CAPACITY_FACTOR * s // N_EXPERTS
    pos, sel, vals = select_tokens(aff, cap)
    idx, gate = build_lists(pos, sel, vals, cap)
    y_rows = expert_ffn(h, idx, gate, w_gate, w_up, w_down, li, th=th)
    return combine(x1, y_rows, pos, sel, cap, g_next, tc=tc, final=final)


def _tiles(s):
    return dict(tm=min(1024, s), tn=1024, to=min(512, s), tt=min(512, s), th=256, tc=min(256, s))


def kernel(x, rel_bias, norm_mix_g, norm_ffn_g, final_norm_g, ev_w_in, ev_sink, ev_w_out, od_w_in, od_conv_w,
           od_pool_w, od_pool_scale, od_w_out, moe_w_router, moe_w_gate, moe_w_up, moe_w_down):
    b, s, d = x.shape
    assert b == 1
    t = _tiles(s)
    depth = norm_mix_g.shape[0]
    bias = window_bias(rel_bias)
    xs = x.reshape(s, d)
    hn = prenorm(xs, norm_mix_g[0], tm=t["tm"])
    for layer in range(depth):
        i = layer // 2
        if layer % 2 == 0:
            z = in_proj(hn, ev_w_in, i, tm=t["tm"], tn=t["tn"])
            parts = [windowed_attention(z, ev_sink[i], bias), fourier_mix(z, Q_DIM + 2 * KV_DIM)]
            w_out = ev_w_out
        else:
            z = in_proj(hn, od_w_in, i, tm=t["tm"], tn=t["tn"])
            parts = [conv_pool(z, od_conv_w[i], od_pool_w[i], od_pool_scale[i], tt=t["tt"])]
            w_out = od_w_out
        x1 = out_proj(xs, parts, w_out, i, tm=t["tm"], tn=t["tn"])
        h, aff = router(x1, norm_ffn_g[layer], moe_w_router[layer], tm=t["to"])
        final = layer == depth - 1
        g_next = final_norm_g if final else norm_mix_g[layer + 1]
        res = ec_moe_block(x1, h, aff.T, moe_w_gate, moe_w_up, moe_w_down, layer, g_next,
                           final=final, th=t["th"], tc=t["tc"])
        if final:
            xs = res
        else:
            xs, hn = res
    return xs.reshape(b, s, d)
```

```python
import functools
import math

import numpy as np
import jax
import jax.numpy as jnp
from jax import lax
from jax.experimental import pallas as pl
from jax.experimental.pallas import tpu as pltpu

F32 = jnp.float32
BF16 = jnp.bfloat16
I32 = jnp.int32

HEAD_DIM = 128
N_Q_HEADS = 12
N_KV_HEADS = 4
GQA_GROUP = N_Q_HEADS // N_KV_HEADS
WINDOW = 128
ATTN_BLOCK = 128
N_FOURIER_GROUPS = 4
FOURIER_GROUP_DIM = 128
Q_DIM = N_Q_HEADS * HEAD_DIM
KV_DIM = N_KV_HEADS * HEAD_DIM
FOURIER_DIM = N_FOURIER_GROUPS * FOURIER_GROUP_DIM
N_REL_BUCKETS = 32
REL_MAX_DISTANCE = 128
CONV_DIM = 1024
POOL_WINDOWS = (2, 4, 8, 16)
POOL_GROUPS = len(POOL_WINDOWS)
POOL_GROUP_DIM = 256
POOL_DIM = POOL_GROUPS * POOL_GROUP_DIM
N_EXPERTS = 16
CAPACITY_FACTOR = 2
RMS_EPS = 1e-6
NEG_INF = -1e30
LOG2E = math.log2(math.e)

LANES = 128
SUBLANES = 8
BF16_ROWS = 16
VMEM_LIMIT = 56 * 1024 * 1024

HALO = BF16_ROWS
COMBINE_WINDOW = 64
LIST_GROUP = 8
COMBINE_GROUP = 2


def _cparams(sem):
    return pltpu.CompilerParams(dimension_semantics=sem, vmem_limit_bytes=VMEM_LIMIT)


def _dot(a, b):
    return jnp.dot(a, b, preferred_element_type=F32)


def _dot_nt(a, b):
    return lax.dot_general(a, b, (((1,), (1,)), ((), ())), preferred_element_type=F32)


def _rmsnorm(x, g):
    ms = jnp.mean(x * x, axis=-1, keepdims=True)
    return x * lax.rsqrt(ms + RMS_EPS) * g


def _pack_bf16_pairs(y):
    n = y.shape[1] // 2
    lo = lax.bitcast_convert_type(y[:, :n].astype(BF16).astype(F32), I32)
    hi = lax.bitcast_convert_type(y[:, n:].astype(BF16).astype(F32), I32)
    return lax.shift_right_logical(lo, 16) | hi


def _unpack_bf16_pairs(p):
    lo = lax.bitcast_convert_type(lax.shift_left(p, 16), F32).astype(BF16)
    hi = lax.bitcast_convert_type(p & (-65536), F32).astype(BF16)
    return lo, hi


def _store_row_tiles(ref, packed, row0=0):
    rows = packed.shape[0]
    for j in range(SUBLANES):
        ref[pl.ds(row0 * SUBLANES + j, rows, stride=SUBLANES), :] = packed[:, j * LANES:(j + 1) * LANES]


def _load_row_tiles(ref):
    rows = ref.shape[0] // SUBLANES
    return jnp.concatenate([ref[pl.ds(j, rows, stride=SUBLANES), :] for j in range(SUBLANES)], axis=1)


def _prenorm_kernel(x_ref, g_ref, o_ref):
    o_ref[...] = _rmsnorm(x_ref[...], g_ref[...]).astype(o_ref.dtype)


def prenorm(x, g, *, tm):
    s, d = x.shape
    return pl.pallas_call(
        _prenorm_kernel,
        grid=(s // tm,),
        in_specs=[pl.BlockSpec((tm, d), lambda i: (i, 0)), pl.BlockSpec((1, d), lambda i: (0, 0))],
        out_specs=pl.BlockSpec((tm, d), lambda i: (i, 0)),
        out_shape=jax.ShapeDtypeStruct((s, d), BF16),
        compiler_params=_cparams(("parallel",)),
        name="prenorm",
    )(x, g.reshape(1, d))


def _in_proj_kernel(h_ref, w_ref, o_ref, wb_ref):
    @pl.when(pl.program_id(1) == 0)
    def _():
        wb_ref[...] = w_ref[...].astype(BF16)

    o_ref[...] = _dot(h_ref[...], wb_ref[...]).astype(o_ref.dtype)


def in_proj(h, w, li, *, tm, tn):
    s, d = h.shape
    n = w.shape[2]
    return pl.pallas_call(
        _in_proj_kernel,
        grid=(n // tn, s // tm),
        in_specs=[
            pl.BlockSpec((tm, d), lambda j, i: (i, 0)),
            pl.BlockSpec((None, d, tn), lambda j, i: (li, 0, j)),
        ],
        out_specs=pl.BlockSpec((tm, tn), lambda j, i: (i, j)),
        out_shape=jax.ShapeDtypeStruct((s, n), BF16),
        scratch_shapes=[pltpu.VMEM((d, tn), BF16)],
        compiler_params=_cparams(("parallel", "arbitrary")),
        name="in_proj",
    )(h, w)


def _out_proj_kernel(*refs, widths):
    nparts = len(widths)
    x_ref = refs[0]
    p_refs = refs[1:1 + nparts]
    w_ref, o_ref, wb_ref = refs[1 + nparts:]

    @pl.when(pl.program_id(1) == 0)
    def _():
        wb_ref[...] = w_ref[...].astype(BF16)

    acc = x_ref[...]
    off = 0
    for p_ref, width in zip(p_refs, widths):
        acc = acc + _dot(p_ref[...].astype(BF16), wb_ref[off:off + width, :])
        off += width
    o_ref[...] = acc


def out_proj(x, parts, w, li, *, tm, tn):
    s, d = x.shape
    k = w.shape[1]
    widths = tuple(p.shape[1] for p in parts)
    assert sum(widths) == k
    in_specs = [pl.BlockSpec((tm, tn), lambda j, i: (i, j))]
    in_specs += [pl.BlockSpec((tm, wd), lambda j, i: (i, 0)) for wd in widths]
    in_specs += [pl.BlockSpec((None, k, tn), lambda j, i: (li, 0, j))]
    return pl.pallas_call(
        functools.partial(_out_proj_kernel, widths=widths),
        grid=(d // tn, s // tm),
        in_specs=in_specs,
        out_specs=pl.BlockSpec((tm, tn), lambda j, i: (i, j)),
        out_shape=jax.ShapeDtypeStruct((s, d), F32),
        scratch_shapes=[pltpu.VMEM((k, tn), BF16)],
        compiler_params=_cparams(("parallel", "arbitrary")),
        name="out_proj",
    )(x, *parts, w)


def _split_bf16(v):
    hi = v.astype(BF16)
    return hi, (v - hi.astype(F32)).astype(BF16)


def _router_kernel(x_ref, g_ref, wr_ref, h_ref, aff_ref):
    h = _rmsnorm(x_ref[...], g_ref[...])
    _store_row_tiles(h_ref, _pack_bf16_pairs(h))
    h_hi, h_lo = _split_bf16(h)
    w_hi, w_lo = _split_bf16(wr_ref[...])
    ne = w_hi.shape[1]
    hh = _dot(h_hi, jnp.concatenate([w_hi, w_lo], axis=1))
    logits = hh[:, :ne] + (hh[:, ne:] + _dot(h_lo, w_hi))
    m = jnp.max(logits, axis=1, keepdims=True)
    p = jnp.exp(logits - m)
    aff_ref[...] = p / jnp.sum(p, axis=1, keepdims=True)


def router(x, g, w_router, *, tm):
    s, d = x.shape
    assert d == 2 * SUBLANES * LANES
    e = w_router.shape[1]
    return pl.pallas_call(
        _router_kernel,
        grid=(s // tm,),
        in_specs=[
            pl.BlockSpec((tm, d), lambda i: (i, 0)),
            pl.BlockSpec((1, d), lambda i: (0, 0)),
            pl.BlockSpec((d, e), lambda i: (0, 0)),
        ],
        out_specs=[pl.BlockSpec((tm * SUBLANES, LANES), lambda i: (i, 0)), pl.BlockSpec((tm, e), lambda i: (i, 0))],
        out_shape=[jax.ShapeDtypeStruct((s * SUBLANES, LANES), I32), jax.ShapeDtypeStruct((s, e), F32)],
        compiler_params=_cparams(("parallel",)),
        name="router",
    )(x, g.reshape(1, d), w_router)


def _t5_bucket(rel):
    nb = N_REL_BUCKETS // 2
    max_exact = nb // 2
    ret = (rel > 0).astype(jnp.int32) * nb
    n = jnp.abs(rel)
    nf = jnp.maximum(n, 1).astype(jnp.float32)
    large = max_exact + (jnp.log(nf / max_exact) / math.log(REL_MAX_DISTANCE / max_exact)
                         * (nb - max_exact)).astype(jnp.int32)
    large = jnp.minimum(large, nb - 1)
    return ret + jnp.where(n < max_exact, n, large)


def _bias_kernel(tab_ref, bucket_ref, o_ref):
    h = pl.program_id(0)
    bucket = bucket_ref[...]
    acc = jnp.zeros(bucket.shape, F32)
    for b in range(N_REL_BUCKETS):
        acc = jnp.where(bucket == b, tab_ref[b * N_Q_HEADS + h], acc)
    i = lax.broadcasted_iota(I32, bucket.shape, 0)
    j = lax.broadcasted_iota(I32, bucket.shape, 1)
    valid = jnp.abs(j - WINDOW - i) <= WINDOW
    o_ref[0] = jnp.where(valid, acc * LOG2E, NEG_INF)


def window_bias(rel_bias):
    i = jnp.arange(ATTN_BLOCK, dtype=jnp.int32)[:, None]
    j = jnp.arange(3 * ATTN_BLOCK, dtype=jnp.int32)[None, :]
    bucket = _t5_bucket((j - WINDOW) - i)
    return pl.pallas_call(
        _bias_kernel,
        grid_spec=pltpu.PrefetchScalarGridSpec(
            num_scalar_prefetch=1,
            grid=(N_Q_HEADS,),
            in_specs=[pl.BlockSpec((ATTN_BLOCK, 3 * ATTN_BLOCK), lambda h, t: (0, 0))],
            out_specs=pl.BlockSpec((1, ATTN_BLOCK, 3 * ATTN_BLOCK), lambda h, t: (h, 0, 0)),
        ),
        out_shape=jax.ShapeDtypeStruct((N_Q_HEADS, ATTN_BLOCK, 3 * ATTN_BLOCK), F32),
        compiler_params=_cparams(("arbitrary",)),
        name="window_bias",
    )(rel_bias.reshape(-1), bucket)


def _attn_kernel(sink_ref, q_ref, kp_ref, kc_ref, kn_ref, vp_ref, vc_ref, vn_ref, bias_ref, o_ref, *, nb):
    n = pl.program_id(0)
    blk = ATTN_BLOCK
    ones = jnp.ones((3 * blk, HEAD_DIM), BF16)
    scale2 = HEAD_DIM ** -0.5 * LOG2E

    def block(at_edge):
        if at_edge:
            col = lax.broadcasted_iota(I32, (1, 3 * blk), 1)
            in_seq = ((col >= blk) | (n > 0)) & ((col < 2 * blk) | (n < nb - 1))
            edge_bias = jnp.where(in_seq, 0.0, NEG_INF)
        for kv in range(N_KV_HEADS):
            cs = slice(kv * HEAD_DIM, (kv + 1) * HEAD_DIM)
            k = jnp.concatenate([kp_ref[:, cs], kc_ref[:, cs], kn_ref[:, cs]], axis=0)
            v = jnp.concatenate([vp_ref[:, cs], vc_ref[:, cs], vn_ref[:, cs]], axis=0)
            v1 = jnp.concatenate([v, ones], axis=1)
            for g in range(GQA_GROUP):
                hq = kv * GQA_GROUP + g
                hs = slice(hq * HEAD_DIM, (hq + 1) * HEAD_DIM)
                s2 = _dot_nt(q_ref[:, hs], k) * scale2 + bias_ref[kv, g * blk:(g + 1) * blk, :]
                if at_edge:
                    s2 = s2 + edge_bias
                sk2 = sink_ref[hq] * LOG2E
                m2 = jnp.maximum(jnp.max(s2, axis=-1, keepdims=True), sk2)
                p = jnp.exp2(s2 - m2).astype(BF16)
                ov = _dot(p, v1)
                denom = ov[:, HEAD_DIM:HEAD_DIM + 1] + jnp.exp2(sk2 - m2)
                o_ref[:, hs] = (ov[:, :HEAD_DIM] / denom).astype(o_ref.dtype)

    @pl.when((n > 0) & (n < nb - 1))
    def _():
        block(False)

    @pl.when((n == 0) | (n == nb - 1))
    def _():
        block(True)


def windowed_attention(z, sink, bias):
    s = z.shape[0]
    blk = ATTN_BLOCK
    nb = s // blk
    kcol = Q_DIM // KV_DIM
    vcol = kcol + 1

    def prev(n, t):
        return jnp.maximum(n - 1, 0)

    def nxt(n, t):
        return jnp.minimum(n + 1, nb - 1)

    in_specs = [
        pl.BlockSpec((blk, Q_DIM), lambda n, t: (n, 0)),
        pl.BlockSpec((blk, KV_DIM), lambda n, t: (prev(n, t), kcol)),
        pl.BlockSpec((blk, KV_DIM), lambda n, t: (n, kcol)),
        pl.BlockSpec((blk, KV_DIM), lambda n, t: (nxt(n, t), kcol)),
        pl.BlockSpec((blk, KV_DIM), lambda n, t: (prev(n, t), vcol)),
        pl.BlockSpec((blk, KV_DIM), lambda n, t: (n, vcol)),
        pl.BlockSpec((blk, KV_DIM), lambda n, t: (nxt(n, t), vcol)),
        pl.BlockSpec((N_KV_HEADS, GQA_GROUP * blk, 3 * blk), lambda n, t: (0, 0, 0)),
    ]
    return pl.pallas_call(
        functools.partial(_attn_kernel, nb=nb),
        grid_spec=pltpu.PrefetchScalarGridSpec(
            num_scalar_prefetch=1,
            grid=(nb,),
            in_specs=in_specs,
            out_specs=pl.BlockSpec((blk, Q_DIM), lambda n, t: (n, 0)),
        ),
        out_shape=jax.ShapeDtypeStruct((s, Q_DIM), BF16),
        compiler_params=_cparams(("arbitrary",)),
        name="windowed_attention",
    )(sink, z, z, z, z, z, z, z, bias.reshape(N_KV_HEADS, GQA_GROUP * blk, 3 * blk))


def _fourier_tables(s):
    n2 = LANES
    n1 = s // n2
    k1 = np.arange(n1)[None, :, None]
    s1 = np.arange(n1)[None, None, :]
    s2 = np.arange(n2)[:, None, None]
    ang = 2.0 * np.pi * ((k1 * (s2 + n2 * s1)) % s) / s
    ma = np.concatenate([np.cos(ang), -np.sin(ang)], axis=1)
    a = 2.0 * np.pi * ((np.arange(n2)[:, None] * np.arange(n2)[None, :]) % n2) / n2
    c, sn = np.cos(a), np.sin(a)
    mc = np.block([[c, sn], [-sn, c]])
    scale = 1.0 / math.sqrt(s * FOURIER_GROUP_DIM)
    md = np.concatenate([c, sn], axis=0) * scale
    return (jnp.asarray(ma, BF16), jnp.asarray(mc, BF16), jnp.asarray(md, BF16))


FOURIER_K1_BATCH = 4


def _fourier_kernel(x_ref, ma_ref, mc_ref, md_ref, o_ref, xf, yr, yi, ot, *, n1):
    n2 = LANES
    pitch = n2 + SUBLANES
    pitch_o = n1 + SUBLANES
    kb = FOURIER_K1_BATCH
    for s1 in range(n1):
        xf[s1 * pitch:s1 * pitch + n2, :] = x_ref[s1 * n2:(s1 + 1) * n2, :].astype(F32)

    def stage_a(s2, carry):
        xs = xf[pl.ds(s2, n1, stride=pitch), :].astype(BF16)
        y = _dot(ma_ref[s2], xs)
        yr[pl.ds(s2, n1, stride=pitch), :] = y[:n1]
        yi[pl.ds(s2, n1, stride=pitch), :] = y[n1:]
        return carry

    lax.fori_loop(0, n2, stage_a, 0, unroll=8)

    def stage_c(b, carry):
        def slab(ref, j):
            return ref[pl.ds(pl.multiple_of((b * kb + j) * pitch, SUBLANES), n2), :]

        y = jnp.concatenate(
            [jnp.concatenate([slab(yr, j), slab(yi, j)], axis=0) for j in range(kb)], axis=1).astype(BF16)
        xc = _dot(mc_ref[...], y)
        xx = jnp.concatenate(
            [jnp.concatenate([xc[:n2, j * LANES:(j + 1) * LANES], xc[n2:, j * LANES:(j + 1) * LANES]], axis=1)
             for j in range(kb)], axis=0).astype(BF16)
        out = _dot(xx, md_ref[...])
        for j in range(kb):
            ot[pl.ds(b * kb + j, n2, stride=pitch_o), :] = out[j * n2:(j + 1) * n2]
        return carry

    lax.fori_loop(0, n1 // kb, stage_c, 0, unroll=2)
    for k2 in range(n2):
        o_ref[k2 * n1:(k2 + 1) * n1, :] = ot[k2 * pitch_o:k2 * pitch_o + n1, :]


def fourier_mix(z, col0):
    s, w = z.shape
    n2 = LANES
    n1 = s // n2
    assert n1 % FOURIER_K1_BATCH == 0 and n1 % SUBLANES == 0
    c = FOURIER_GROUP_DIM
    ma, mc, md = _fourier_tables(s)
    cb0 = col0 // c
    slabs = pltpu.VMEM((n1 * (n2 + SUBLANES), c), F32)
    return pl.pallas_call(
        functools.partial(_fourier_kernel, n1=n1),
        grid=(N_FOURIER_GROUPS,),
        in_specs=[
            pl.BlockSpec((s, c), lambda g: (0, cb0 + g)),
            pl.BlockSpec((n2, 2 * n1, n1), lambda g: (0, 0, 0)),
            pl.BlockSpec((2 * n2, 2 * n2), lambda g: (0, 0)),
            pl.BlockSpec((2 * c, c), lambda g: (0, 0)),
        ],
        out_specs=pl.BlockSpec((s, c), lambda g: (0, g)),
        out_shape=jax.ShapeDtypeStruct((s, FOURIER_DIM), F32),
        scratch_shapes=[slabs, slabs, slabs, pltpu.VMEM((n2 * (n1 + SUBLANES), c), F32)],
        compiler_params=_cparams(("arbitrary",)),
        name="fourier_mix",
    )(z, ma, mc, md)


def _convpool_kernel(zp_ref, zc_ref, zn_ref, cw_ref, pw_ref, ps_ref, o_ref, *, seq, tt):
    i = pl.program_id(0)
    ext_rows = tt + 2 * HALO
    grow = i * tt - HALO + lax.broadcasted_iota(I32, (ext_rows, 1), 0)
    row_ok = (grow >= 0) & (grow < seq)

    def ext(c0, c1):
        e = jnp.concatenate([zp_ref[:, c0:c1], zc_ref[:, c0:c1], zn_ref[:, c0:c1]], axis=0).astype(F32)
        return jnp.where(row_ok, e, 0.0)

    def shifted(e, d):
        return e[HALO + d:HALO + d + tt]

    prod = ext(CONV_DIM, 2 * CONV_DIM) * ext(2 * CONV_DIM, 3 * CONV_DIM)
    cw = cw_ref[...]
    conv = shifted(prod, -1) * cw[0:1] + shifted(prod, 0) * cw[1:2] + shifted(prod, 1) * cw[2:3]
    o_ref[:, :CONV_DIM] = (zc_ref[:, :CONV_DIM].astype(F32) * conv).astype(o_ref.dtype)

    t = grow[HALO:HALO + tt]
    for g, win in enumerate(POOL_WINDOWS):
        lo = win // 2
        hi = win - 1 - lo
        c0 = 3 * CONV_DIM + g * POOL_GROUP_DIM
        e = ext(c0, c0 + POOL_GROUP_DIM)
        assert win & (win - 1) == 0 and win <= HALO
        run, step = e, 1
        while step < win:
            run = run + pltpu.roll(run, ext_rows - step, axis=0)
            step *= 2
        total = shifted(run, -lo)
        count = (jnp.minimum(t + hi, seq - 1) - jnp.maximum(t - lo, 0) + 1).astype(F32)
        pooled = total / count - shifted(e, 0)
        dg = _dot(pooled.astype(BF16), pw_ref[g].astype(BF16)) * ps_ref[:, g * POOL_GROUP_DIM:(g + 1) * POOL_GROUP_DIM]
        o_ref[:, CONV_DIM + g * POOL_GROUP_DIM:CONV_DIM + (g + 1) * POOL_GROUP_DIM] = dg.astype(o_ref.dtype)


def conv_pool(z, conv_w, pool_w, pool_scale, *, tt):
    s, w = z.shape
    nh = tt // HALO
    last_h = s // HALO - 1
    return pl.pallas_call(
        functools.partial(_convpool_kernel, seq=s, tt=tt),
        grid=(s // tt,),
        in_specs=[
            pl.BlockSpec((HALO, w), lambda i: (jnp.maximum(i * nh - 1, 0), 0)),
            pl.BlockSpec((tt, w), lambda i: (i, 0)),
            pl.BlockSpec((HALO, w), lambda i: (jnp.minimum((i + 1) * nh, last_h), 0)),
            pl.BlockSpec((3, CONV_DIM), lambda i: (0, 0)),
            pl.BlockSpec((POOL_GROUPS, POOL_GROUP_DIM, POOL_GROUP_DIM), lambda i: (0, 0, 0)),
            pl.BlockSpec((1, POOL_DIM), lambda i: (0, 0)),
        ],
        out_specs=pl.BlockSpec((tt, CONV_DIM + POOL_DIM), lambda i: (i, 0)),
        out_shape=jax.ShapeDtypeStruct((s, CONV_DIM + POOL_DIM), BF16),
        compiler_params=_cparams(("arbitrary",)),
        name="conv_pool",
    )(z, z, z, conv_w, pool_w, pool_scale.reshape(1, POOL_DIM))


def _excl_cumsum_lanes(m, upper):
    r, s = m.shape
    off = jnp.zeros((r, 1), F32)
    pieces = []
    for c in range(s // LANES):
        mc = m[:, c * LANES:(c + 1) * LANES]
        pieces.append(_dot(mc.astype(BF16), upper) + off)
        off = off + jnp.sum(mc, axis=1, keepdims=True)
    return jnp.concatenate(pieces, axis=1)


def _select_kernel(aff_ref, upper_ref, pos_ref, sel_ref, v_ref, *, cap):
    a = aff_ref[...]
    e, s = a.shape
    bits = pltpu.bitcast(a, I32)
    thr = jnp.zeros((e, 1), I32)
    for bit in range(30, -1, -1):
        cand = thr | (1 << bit)
        cnt = jnp.sum((bits >= cand).astype(F32), axis=1, keepdims=True)
        thr = jnp.where(cnt >= cap, cand, thr)
    gt = bits > thr
    eq = (bits == thr).astype(F32)
    need = cap - jnp.sum(gt.astype(F32), axis=1, keepdims=True)
    upper = upper_ref[...]
    eq_rank = _excl_cumsum_lanes(eq, upper)
    sel = jnp.where(gt | ((eq > 0.5) & (eq_rank < need)), 1.0, 0.0)
    pos = _excl_cumsum_lanes(sel, upper)
    pos_ref[...] = pos
    sel_ref[...] = sel
    tok = lax.broadcasted_iota(I32, (1, s), 1).astype(F32)
    tok_hi = jnp.floor(tok * (1.0 / LANES))
    tok_lo = tok - tok_hi * LANES
    g1 = a.astype(BF16).astype(F32)
    g2 = (a - g1).astype(BF16).astype(F32)
    g3 = a - g1 - g2
    zero = jnp.zeros((SUBLANES - 5, s), F32)
    for x in range(e):
        v_ref[x] = jnp.concatenate([tok_hi, tok_lo, g1[x:x + 1], g2[x:x + 1], g3[x:x + 1], zero], axis=0)


def select_tokens(aff, cap):
    e, s = aff.shape
    upper = jnp.asarray(np.triu(np.ones((LANES, LANES), np.float32), k=1), BF16)
    full2 = lambda shp: pl.BlockSpec(shp, lambda i: (0,) * len(shp))
    return pl.pallas_call(
        functools.partial(_select_kernel, cap=cap),
        grid=(1,),
        in_specs=[full2((e, s)), full2((LANES, LANES))],
        out_specs=[full2((e, s)), full2((e, s)), full2((e, SUBLANES, s))],
        out_shape=[
            jax.ShapeDtypeStruct((e, s), F32),
            jax.ShapeDtypeStruct((e, s), F32),
            jax.ShapeDtypeStruct((e, SUBLANES, s), F32),
        ],
        compiler_params=_cparams(("arbitrary",)),
        name="select_tokens",
    )(aff, upper)


def _lists_kernel(off_ref, cross_ref, pos_ref, sel_ref, v_ref, o_ref, acc, *, nchunk, ntile, group):
    e0 = pl.program_id(0) * group
    acc[...] = jnp.zeros(acc.shape, F32)
    slot = lax.broadcasted_iota(I32, (LANES, LANES), 0).astype(F32)

    def add_chunk(x, c, tile):
        st = pl.multiple_of(c * LANES, LANES)
        p = pos_ref[x, :, pl.ds(st, LANES)]
        chosen = sel_ref[x, :, pl.ds(st, LANES)] > 0.5
        v = v_ref[x, :, pl.ds(st, LANES)].astype(BF16)
        base = (tile * LANES).astype(F32)
        onehot = jnp.where(((p - base) == slot) & chosen, 1.0, 0.0).astype(BF16)
        acc[x, tile] += _dot_nt(v, onehot)

    def body(c, carry):
        for x in range(group):
            add_chunk(x, c, jnp.minimum(off_ref[(e0 + x) * nchunk + c] // LANES, ntile - 1))
        return carry

    lax.fori_loop(0, nchunk, body, 0, unroll=4)
    for j in range(1, ntile):
        for x in range(group):
            add_chunk(x, cross_ref[(e0 + x) * ntile + j], jnp.int32(j))
    for x in range(group):
        for j in range(ntile):
            o_ref[x, :, j * LANES:(j + 1) * LANES] = acc[x, j]


def build_lists(pos, sel, vals, cap):
    e, s = pos.shape
    nchunk = s // LANES
    ntile = cap // LANES
    off = pos[:, ::LANES].astype(I32)
    bounds = jnp.arange(ntile, dtype=I32) * LANES
    cross = jnp.maximum(jnp.sum((off[:, :, None] < bounds[None, None, :]).astype(I32), axis=1) - 1, 0)
    group = LIST_GROUP
    assert e % group == 0
    row = lambda nrow: pl.BlockSpec((group, nrow, s), lambda x, t, u: (x, 0, 0))
    lists = pl.pallas_call(
        functools.partial(_lists_kernel, nchunk=nchunk, ntile=ntile, group=group),
        grid_spec=pltpu.PrefetchScalarGridSpec(
            num_scalar_prefetch=2,
            grid=(e // group,),
            in_specs=[row(1), row(1), row(SUBLANES)],
            out_specs=pl.BlockSpec((group, SUBLANES, cap), lambda x, t, u: (x, 0, 0)),
            scratch_shapes=[pltpu.VMEM((group, ntile, SUBLANES, LANES), F32)],
        ),
        out_shape=jax.ShapeDtypeStruct((e, SUBLANES, cap), F32),
        compiler_params=_cparams(("arbitrary",)),
        name="build_lists",
    )(off.reshape(-1), cross.reshape(-1), pos.reshape(e, 1, s), sel.reshape(e, 1, s), vals)
    idx = (lists[:, 0] * LANES + lists[:, 1]).astype(I32)
    gate = (lists[:, 2] + lists[:, 3] + lists[:, 4])
    return idx, gate


def _ffn_kernel(idx_ref, h_hbm, gate_ref, wg_ref, wu_ref, wd_ref, y_ref, xbuf, xb, hid, sem_g, *, cap, nh, ne):
    e = pl.program_id(0)
    hs = pl.program_id(1)
    slot = e % 2
    rps = cap // nh

    def gather_row(expert, r, s):
        t = idx_ref[expert * cap + r]
        return pltpu.make_async_copy(h_hbm.at[pl.ds(pl.multiple_of(t * SUBLANES, SUBLANES), SUBLANES)],
                                     xbuf.at[s, pl.ds(pl.multiple_of(r * SUBLANES, SUBLANES), SUBLANES)],
                                     sem_g.at[s])

    def wait_gather(s):
        pltpu.make_async_copy(h_hbm.at[pl.ds(0, cap * SUBLANES)], xbuf.at[s], sem_g.at[s]).wait()

    @pl.when((e == 0) & (hs == 0))
    def _():
        def issue(r, carry):
            gather_row(0, r, 0).start()
            return carry

        lax.fori_loop(0, cap, issue, 0, unroll=8)

    @pl.when(hs == 0)
    def _():
        wait_gather(slot)
        lo, hi = _unpack_bf16_pairs(_load_row_tiles(xbuf.at[slot]))
        xb[...] = jnp.concatenate([lo, hi], axis=1)

    th = wg_ref.shape[1]
    w_gu = jnp.concatenate([wg_ref[...].astype(BF16), wu_ref[...].astype(BF16)], axis=1)
    gu = _dot(xb[...], w_gu)
    gp = gu[:, :th]
    hid[:, pl.ds(pl.multiple_of(hs * th, th), th)] = (gp * (1.0 / (1.0 + jnp.exp(-gp))) * gu[:, th:]).astype(BF16)
    nxt = lax.rem(e + 1, ne)
    for k in range(rps):
        gather_row(nxt, hs * rps + k, 1 - slot).start()

    @pl.when(hs == nh - 1)
    def _():
        w_d = wd_ref[...].astype(BF16)
        rb = cap // nh
        for b in range(nh):
            rows = slice(b * rb, (b + 1) * rb)
            y = _dot(hid[rows, :], w_d) * gate_ref[0, rows, :]
            _store_row_tiles(y_ref, _pack_bf16_pairs(y), b * rb)

    @pl.when((e == ne - 1) & (hs == nh - 1))
    def _():
        wait_gather(1 - slot)


def expert_ffn(h, idx, gate, w_gate, w_up, w_down, li, *, th):
    ne, cap = idx.shape
    d = w_gate.shape[2]
    hidden = w_gate.shape[3]
    nh = hidden // th
    assert d == 2 * SUBLANES * LANES
    return pl.pallas_call(
        functools.partial(_ffn_kernel, cap=cap, nh=nh, ne=ne),
        grid_spec=pltpu.PrefetchScalarGridSpec(
            num_scalar_prefetch=1,
            grid=(ne, nh),
            in_specs=[
                pl.BlockSpec(memory_space=pl.ANY),
                pl.BlockSpec((1, cap, 1), lambda x, j, a: (x, 0, 0)),
                pl.BlockSpec((None, None, d, th), lambda x, j, a: (li, x, 0, j)),
                pl.BlockSpec((None, None, d, th), lambda x, j, a: (li, x, 0, j)),
                pl.BlockSpec((None, None, hidden, d), lambda x, j, a: (li, x, 0, 0)),
            ],
            out_specs=pl.BlockSpec((cap * SUBLANES, LANES), lambda x, j, a: (x, 0),
                                   pipeline_mode=pl.Buffered(1)),
            scratch_shapes=[
                pltpu.VMEM((2, cap * SUBLANES, LANES), I32),
                pltpu.VMEM((cap, d), BF16),
                pltpu.VMEM((cap, hidden), BF16),
                pltpu.SemaphoreType.DMA((2,)),
            ],
        ),
        out_shape=jax.ShapeDtypeStruct((ne * cap * SUBLANES, LANES), I32),
        compiler_params=_cparams(("arbitrary", "arbitrary")),
        name="expert_ffn",
    )(idx.reshape(-1), h, gate.reshape(ne, cap, 1), w_gate, w_up, w_down)


def _combine_kernel(off_ref, x_ref, pos_ref, g_ref, y_hbm, *rest, ne, cap, nchunk, final):
    nout = 1 if final else 2
    o_refs = rest[:nout]
    buf, obuf, acc, member, sem, osem = rest[nout:]
    step = pl.program_id(0)
    slot = step % 2
    grp = COMBINE_GROUP
    tc = member.shape[1]
    windows = _CombineWindows(off_ref, ne, cap, nchunk)

    def fetch(e, st, g, s):
        return pltpu.make_async_copy(y_hbm.at[windows.tiles(windows.row(e, st * grp + g, 0), COMBINE_WINDOW)],
                                     buf.at[s, g, windows.tiles(e * COMBINE_WINDOW, COMBINE_WINDOW)], sem.at[s])

    @pl.when(step == 0)
    def _():
        for g in range(grp):
            for e in range(ne):
                fetch(e, 0, g, 0).start()

    @pl.when((step + 1) * grp < nchunk)
    def _():
        for g in range(grp):
            for e in range(ne):
                fetch(e, step + 1, g, 1 - slot).start()

    for g in range(grp):
        for e in range(ne):
            fetch(e, step, g, slot).wait()

    for first_pieces in (True, False):
        for g in range(grp):
            _combine_chunk(step * grp + g, g, slot, windows, pos_ref, y_hbm, buf, obuf, acc, member, osem, tc,
                           first_pieces)

    y = x_ref[...] + acc[...]
    if final:
        o_refs[0][...] = _rmsnorm(y, g_ref[...])
    else:
        o_refs[0][...] = y
        o_refs[1][...] = _rmsnorm(y, g_ref[...]).astype(BF16)


class _CombineWindows:
    def __init__(self, off_ref, ne, cap, nchunk):
        self.off_ref, self.ne, self.cap, self.nchunk = off_ref, ne, cap, nchunk

    def off(self, e, cc):
        return self.off_ref[e * (self.nchunk + 1) + cc]

    def row(self, e, cc, p):
        return jnp.minimum(e * self.cap + self.off(e, cc) + p * COMBINE_WINDOW, self.ne * self.cap - COMBINE_WINDOW)

    @staticmethod
    def tiles(row, count):
        return pl.ds(pl.multiple_of(row * SUBLANES, SUBLANES), count * SUBLANES)


def _combine_chunk(c, g, slot, windows, pos_ref, y_hbm, buf, obuf, acc, member, osem, tc, first_pieces):
    win = COMBINE_WINDOW
    ne, cap = windows.ne, windows.cap
    chunk_off, win_row, tiles = windows.off, windows.row, windows.tiles
    half = acc.shape[1] // 2
    rows = slice(g * tc, (g + 1) * tc)
    pos = pos_ref[rows, :]

    def window_col(e, p):
        local = pos[:, e:e + 1] - chunk_off(e, c).astype(F32)
        col = pos[:, e:e + 1] + (e * cap - win_row(e, c, p)).astype(F32)
        ok = (local >= p * win) & (local < (p + 1) * win)
        return jnp.where(ok, col, -1.0)

    if first_pieces:
        lane = lax.broadcasted_iota(I32, (1, LANES), 1)
        first = lane < win
        w = jnp.where(first, lane, lane - win).astype(F32)
        for q in range(ne // 2):
            col = jnp.where(first, window_col(2 * q, 0), window_col(2 * q + 1, 0))
            member[g, :, q * LANES:(q + 1) * LANES] = jnp.where(col == w, 1.0, 0.0).astype(BF16)
        a, b = _unpack_bf16_pairs(_load_row_tiles(buf.at[slot, g]))
        acc[rows, :half] = _dot(member[g], a)
        acc[rows, half:] = _dot(member[g], b)
        return

    for e in range(ne):
        npieces = (chunk_off(e, c + 1) - chunk_off(e, c) + win - 1) // win

        def extra(p, carry, e=e):
            cp = pltpu.make_async_copy(y_hbm.at[tiles(win_row(e, c, p), win)], obuf, osem)
            cp.start()
            cp.wait()
            wi = lax.broadcasted_iota(I32, (1, win), 1).astype(F32)
            m = jnp.where(window_col(e, p) == wi, 1.0, 0.0).astype(BF16)
            oa, ob = _unpack_bf16_pairs(_load_row_tiles(obuf))
            acc[rows, :half] += _dot(m, oa)
            acc[rows, half:] += _dot(m, ob)
            return carry

        lax.fori_loop(1, npieces, extra, 0)


def combine(x, y_rows, pos, sel, cap, g, *, tc, final):
    s, d = x.shape
    ne = pos.shape[0]
    nchunk = s // tc
    grp = COMBINE_GROUP
    assert 2 * COMBINE_WINDOW == LANES and ne % 2 == 0 and ne * cap >= COMBINE_WINDOW and nchunk % grp == 0
    off = jnp.concatenate([pos[:, ::tc].astype(I32), jnp.full((ne, 1), cap, I32)], axis=1).reshape(-1)
    row_block = pl.BlockSpec((grp * tc, d), lambda c, t: (c, 0))
    if final:
        out_specs, out_shape = row_block, jax.ShapeDtypeStruct((s, d), F32)
    else:
        out_specs = [row_block, row_block]
        out_shape = [jax.ShapeDtypeStruct((s, d), F32), jax.ShapeDtypeStruct((s, d), BF16)]
    return pl.pallas_call(
        functools.partial(_combine_kernel, ne=ne, cap=cap, nchunk=nchunk, final=final),
        grid_spec=pltpu.PrefetchScalarGridSpec(
            num_scalar_prefetch=1,
            grid=(nchunk // grp,),
            in_specs=[
                row_block,
                pl.BlockSpec((grp * tc, ne), lambda c, t: (c, 0)),
                pl.BlockSpec((1, d), lambda c, t: (0, 0)),
                pl.BlockSpec(memory_space=pl.ANY),
            ],
            out_specs=out_specs,
            scratch_shapes=[
                pltpu.VMEM((2, grp, ne * COMBINE_WINDOW * SUBLANES, LANES), I32),
                pltpu.VMEM((COMBINE_WINDOW * SUBLANES, LANES), I32),
                pltpu.VMEM((grp * tc, d), F32),
                pltpu.VMEM((grp, tc, ne * COMBINE_WINDOW), BF16),
                pltpu.SemaphoreType.DMA((2,)),
                pltpu.SemaphoreType.DMA(()),
            ],
        ),
        out_shape=out_shape,
        compiler_params=_cparams(("arbitrary",)),
        name="combine",
    )(off, x, jnp.where(sel > 0.5, pos, -1.0).T, g.reshape(1, d), y_rows)


def ec_moe_block(x1, h, aff, w_gate, w_up, w_down, li, g_next, *, final, th, tc):
    s = x1.shape[0]
    cap = CAPACITY_FACTOR * s // N_EXPERTS
    pos, sel, vals = select_tokens(aff, cap)
    idx, gate = build_lists(pos, sel, vals, cap)
    y_rows = expert_ffn(h, idx, gate, w_gate, w_up, w_down, li, th=th)
    return combine(x1, y_rows, pos, sel, cap, g_next, tc=tc, final=final)


def _tiles(s):
    return dict(tm=min(1024, s), tn=1024, to=min(512, s), tt=min(512, s), th=256, tc=min(256, s))


def kernel(x, rel_bias, norm_mix_g, norm_ffn_g, final_norm_g, ev_w_in, ev_sink, ev_w_out, od_w_in, od_conv_w,
           od_pool_w, od_pool_scale, od_w_out, moe_w_router, moe_w_gate, moe_w_up, moe_w_down):
    b, s, d = x.shape
    assert b == 1
    t = _tiles(s)
    depth = norm_mix_g.shape[0]
    bias = window_bias(rel_bias)
    xs = x.reshape(s, d)
    hn = prenorm(xs, norm_mix_g[0], tm=t["tm"])
    for layer in range(depth):
        i = layer // 2
        if layer % 2 == 0:
            z = in_proj(hn, ev_w_in, i, tm=t["tm"], tn=t["tn"])
            parts = [windowed_attention(z, ev_sink[i], bias), fourier_mix(z, Q_DIM + 2 * KV_DIM)]
            w_out = ev_w_out
        else:
            z = in_proj(hn, od_w_in, i, tm=t["tm"], tn=t["tn"])
            parts = [conv_pool(z, od_conv_w[i], od_pool_w[i], od_pool_scale[i], tt=t["tt"])]
            w_out = od_w_out
        x1 = out_proj(xs, parts, w_out, i, tm=t["tm"], tn=t["tn"])
        h, aff = router(x1, norm_ffn_g[layer], moe_w_router[layer], tm=t["to"])
        final = layer == depth - 1
        g_next = final_norm_g if final else norm_mix_g[layer + 1]
        res = ec_moe_block(x1, h, aff.T, moe_w_gate, moe_w_up, moe_w_down, layer, g_next,
                           final=final, th=t["th"], tc=t["tc"])
        if final:
            xs = res
        else:
            xs, hn = res
    return xs.reshape(b, s, d)
```

```python
import functools
import math

import numpy as np
import jax
import jax.numpy as jnp
from jax import lax
from jax.experimental import pallas as pl
from jax.experimental.pallas import tpu as pltpu

F32 = jnp.float32
BF16 = jnp.bfloat16
I32 = jnp.int32

HEAD_DIM = 128
N_Q_HEADS = 12
N_KV_HEADS = 4
GQA_GROUP = N_Q_HEADS // N_KV_HEADS
WINDOW = 128
ATTN_BLOCK = 128
N_FOURIER_GROUPS = 4
FOURIER_GROUP_DIM = 128
Q_DIM = N_Q_HEADS * HEAD_DIM
KV_DIM = N_KV_HEADS * HEAD_DIM
FOURIER_DIM = N_FOURIER_GROUPS * FOURIER_GROUP_DIM
N_REL_BUCKETS = 32
REL_MAX_DISTANCE = 128
CONV_DIM = 1024
POOL_WINDOWS = (2, 4, 8, 16)
POOL_GROUPS = len(POOL_WINDOWS)
POOL_GROUP_DIM = 256
POOL_DIM = POOL_GROUPS * POOL_GROUP_DIM
N_EXPERTS = 16
CAPACITY_FACTOR = 2
RMS_EPS = 1e-6
NEG_INF = -1e30
LOG2E = math.log2(math.e)

LANES = 128
SUBLANES = 8
BF16_ROWS = 16
VMEM_LIMIT = 56 * 1024 * 1024

HALO = BF16_ROWS
COMBINE_WINDOW = 64
LIST_GROUP = 16
COMBINE_GROUP = 2


def _cparams(sem):
    return pltpu.CompilerParams(dimension_semantics=sem, vmem_limit_bytes=VMEM_LIMIT)


def _dot(a, b):
    return jnp.dot(a, b, preferred_element_type=F32)


def _dot_nt(a, b):
    return lax.dot_general(a, b, (((1,), (1,)), ((), ())), preferred_element_type=F32)


def _rmsnorm(x, g):
    ms = jnp.mean(x * x, axis=-1, keepdims=True)
    return x * lax.rsqrt(ms + RMS_EPS) * g


def _pack_bf16_pairs(y):
    n = y.shape[1] // 2
    lo = lax.bitcast_convert_type(y[:, :n].astype(BF16).astype(F32), I32)
    hi = lax.bitcast_convert_type(y[:, n:].astype(BF16).astype(F32), I32)
    return lax.shift_right_logical(lo, 16) | hi


def _unpack_bf16_pairs(p):
    lo = lax.bitcast_convert_type(lax.shift_left(p, 16), F32).astype(BF16)
    hi = lax.bitcast_convert_type(p & (-65536), F32).astype(BF16)
    return lo, hi


def _store_row_tiles(ref, packed, row0=0):
    rows = packed.shape[0]
    for j in range(SUBLANES):
        ref[pl.ds(row0 * SUBLANES + j, rows, stride=SUBLANES), :] = packed[:, j * LANES:(j + 1) * LANES]


def _load_row_tiles(ref):
    rows = ref.shape[0] // SUBLANES
    return jnp.concatenate([ref[pl.ds(j, rows, stride=SUBLANES), :] for j in range(SUBLANES)], axis=1)


def _prenorm_kernel(x_ref, g_ref, o_ref):
    o_ref[...] = _rmsnorm(x_ref[...], g_ref[...]).astype(o_ref.dtype)


def prenorm(x, g, *, tm):
    s, d = x.shape
    return pl.pallas_call(
        _prenorm_kernel,
        grid=(s // tm,),
        in_specs=[pl.BlockSpec((tm, d), lambda i: (i, 0)), pl.BlockSpec((1, d), lambda i: (0, 0))],
        out_specs=pl.BlockSpec((tm, d), lambda i: (i, 0)),
        out_shape=jax.ShapeDtypeStruct((s, d), BF16),
        compiler_params=_cparams(("parallel",)),
        name="prenorm",
    )(x, g.reshape(1, d))


def _in_proj_kernel(h_ref, w_ref, o_ref, wb_ref):
    @pl.when(pl.program_id(1) == 0)
    def _():
        wb_ref[...] = w_ref[...].astype(BF16)

    o_ref[...] = _dot(h_ref[...], wb_ref[...]).astype(o_ref.dtype)


def in_proj(h, w, li, *, tm, tn):
    s, d = h.shape
    n = w.shape[2]
    return pl.pallas_call(
        _in_proj_kernel,
        grid=(n // tn, s // tm),
        in_specs=[
            pl.BlockSpec((tm, d), lambda j, i: (i, 0)),
            pl.BlockSpec((None, d, tn), lambda j, i: (li, 0, j)),
        ],
        out_specs=pl.BlockSpec((tm, tn), lambda j, i: (i, j)),
        out_shape=jax.ShapeDtypeStruct((s, n), BF16),
        scratch_shapes=[pltpu.VMEM((d, tn), BF16)],
        compiler_params=_cparams(("parallel", "arbitrary")),
        name="in_proj",
    )(h, w)


def _out_proj_kernel(*refs, widths):
    nparts = len(widths)
    x_ref = refs[0]
    p_refs = refs[1:1 + nparts]
    w_ref, o_ref, wb_ref = refs[1 + nparts:]

    @pl.when(pl.program_id(1) == 0)
    def _():
        wb_ref[...] = w_ref[...].astype(BF16)

    acc = x_ref[...]
    off = 0
    for p_ref, width in zip(p_refs, widths):
        acc = acc + _dot(p_ref[...].astype(BF16), wb_ref[off:off + width, :])
        off += width
    o_ref[...] = acc


def out_proj(x, parts, w, li, *, tm, tn):
    s, d = x.shape
    k = w.shape[1]
    widths = tuple(p.shape[1] for p in parts)
    assert sum(widths) == k
    in_specs = [pl.BlockSpec((tm, tn), lambda j, i: (i, j))]
    in_specs += [pl.BlockSpec((tm, wd), lambda j, i: (i, 0)) for wd in widths]
    in_specs += [pl.BlockSpec((None, k, tn), lambda j, i: (li, 0, j))]
    return pl.pallas_call(
        functools.partial(_out_proj_kernel, widths=widths),
        grid=(d // tn, s // tm),
        in_specs=in_specs,
        out_specs=pl.BlockSpec((tm, tn), lambda j, i: (i, j)),
        out_shape=jax.ShapeDtypeStruct((s, d), F32),
        scratch_shapes=[pltpu.VMEM((k, tn), BF16)],
        compiler_params=_cparams(("parallel", "arbitrary")),
        name="out_proj",
    )(x, *parts, w)


def _split_bf16(v):
    hi = v.astype(BF16)
    return hi, (v - hi.astype(F32)).astype(BF16)


def _router_kernel(x_ref, g_ref, wr_ref, h_ref, aff_ref):
    h = _rmsnorm(x_ref[...], g_ref[...])
    _store_row_tiles(h_ref, _pack_bf16_pairs(h))
    h_hi, h_lo = _split_bf16(h)
    w_hi, w_lo = _split_bf16(wr_ref[...])
    ne = w_hi.shape[1]
    hh = _dot(h_hi, jnp.concatenate([w_hi, w_lo], axis=1))
    logits = hh[:, :ne] + (hh[:, ne:] + _dot(h_lo, w_hi))
    m = jnp.max(logits, axis=1, keepdims=True)
    p = jnp.exp(logits - m)
    aff_ref[...] = p / jnp.sum(p, axis=1, keepdims=True)


def router(x, g, w_router, *, tm):
    s, d = x.shape
    assert d == 2 * SUBLANES * LANES
    e = w_router.shape[1]
    return pl.pallas_call(
        _router_kernel,
        grid=(s // tm,),
        in_specs=[
            pl.BlockSpec((tm, d), lambda i: (i, 0)),
            pl.BlockSpec((1, d), lambda i: (0, 0)),
            pl.BlockSpec((d, e), lambda i: (0, 0)),
        ],
        out_specs=[pl.BlockSpec((tm * SUBLANES, LANES), lambda i: (i, 0)), pl.BlockSpec((tm, e), lambda i: (i, 0))],
        out_shape=[jax.ShapeDtypeStruct((s * SUBLANES, LANES), I32), jax.ShapeDtypeStruct((s, e), F32)],
        compiler_params=_cparams(("parallel",)),
        name="router",
    )(x, g.reshape(1, d), w_router)


def _t5_bucket(rel):
    nb = N_REL_BUCKETS // 2
    max_exact = nb // 2
    ret = (rel > 0).astype(jnp.int32) * nb
    n = jnp.abs(rel)
    nf = jnp.maximum(n, 1).astype(jnp.float32)
    large = max_exact + (jnp.log(nf / max_exact) / math.log(REL_MAX_DISTANCE / max_exact)
                         * (nb - max_exact)).astype(jnp.int32)
    large = jnp.minimum(large, nb - 1)
    return ret + jnp.where(n < max_exact, n, large)


def _bias_kernel(tab_ref, bucket_ref, o_ref):
    h = pl.program_id(0)
    bucket = bucket_ref[...]
    acc = jnp.zeros(bucket.shape, F32)
    for b in range(N_REL_BUCKETS):
        acc = jnp.where(bucket == b, tab_ref[b * N_Q_HEADS + h], acc)
    i = lax.broadcasted_iota(I32, bucket.shape, 0)
    j = lax.broadcasted_iota(I32, bucket.shape, 1)
    valid = jnp.abs(j - WINDOW - i) <= WINDOW
    o_ref[0] = jnp.where(valid, acc * LOG2E, NEG_INF)


def window_bias(rel_bias):
    i = jnp.arange(ATTN_BLOCK, dtype=jnp.int32)[:, None]
    j = jnp.arange(3 * ATTN_BLOCK, dtype=jnp.int32)[None, :]
    bucket = _t5_bucket((j - WINDOW) - i)
    return pl.pallas_call(
        _bias_kernel,
        grid_spec=pltpu.PrefetchScalarGridSpec(
            num_scalar_prefetch=1,
            grid=(N_Q_HEADS,),
            in_specs=[pl.BlockSpec((ATTN_BLOCK, 3 * ATTN_BLOCK), lambda h, t: (0, 0))],
            out_specs=pl.BlockSpec((1, ATTN_BLOCK, 3 * ATTN_BLOCK), lambda h, t: (h, 0, 0)),
        ),
        out_shape=jax.ShapeDtypeStruct((N_Q_HEADS, ATTN_BLOCK, 3 * ATTN_BLOCK), F32),
        compiler_params=_cparams(("arbitrary",)),
        name="window_bias",
    )(rel_bias.reshape(-1), bucket)


def _attn_kernel(sink_ref, q_ref, kp_ref, kc_ref, kn_ref, vp_ref, vc_ref, vn_ref, bias_ref, o_ref, *, nb):
    n = pl.program_id(0)
    blk = ATTN_BLOCK
    ones = jnp.ones((3 * blk, HEAD_DIM), BF16)
    scale2 = HEAD_DIM ** -0.5 * LOG2E

    def block(at_edge):
        if at_edge:
            col = lax.broadcasted_iota(I32, (1, 3 * blk), 1)
            in_seq = ((col >= blk) | (n > 0)) & ((col < 2 * blk) | (n < nb - 1))
            edge_bias = jnp.where(in_seq, 0.0, NEG_INF)
        for kv in range(N_KV_HEADS):
            cs = slice(kv * HEAD_DIM, (kv + 1) * HEAD_DIM)
            k = jnp.concatenate([kp_ref[:, cs], kc_ref[:, cs], kn_ref[:, cs]], axis=0)
            v = jnp.concatenate([vp_ref[:, cs], vc_ref[:, cs], vn_ref[:, cs]], axis=0)
            v1 = jnp.concatenate([v, ones], axis=1)
            for g in range(GQA_GROUP):
                hq = kv * GQA_GROUP + g
                hs = slice(hq * HEAD_DIM, (hq + 1) * HEAD_DIM)
                s2 = _dot_nt(q_ref[:, hs], k) * scale2 + bias_ref[kv, g * blk:(g + 1) * blk, :]
                if at_edge:
                    s2 = s2 + edge_bias
                sk2 = sink_ref[hq] * LOG2E
                m2 = jnp.maximum(jnp.max(s2, axis=-1, keepdims=True), sk2)
                p = jnp.exp2(s2 - m2).astype(BF16)
                ov = _dot(p, v1)
                denom = ov[:, HEAD_DIM:HEAD_DIM + 1] + jnp.exp2(sk2 - m2)
                o_ref[:, hs] = (ov[:, :HEAD_DIM] / denom).astype(o_ref.dtype)

    @pl.when((n > 0) & (n < nb - 1))
    def _():
        block(False)

    @pl.when((n == 0) | (n == nb - 1))
    def _():
        block(True)


def windowed_attention(z, sink, bias):
    s = z.shape[0]
    blk = ATTN_BLOCK
    nb = s // blk
    kcol = Q_DIM // KV_DIM
    vcol = kcol + 1

    def prev(n, t):
        return jnp.maximum(n - 1, 0)

    def nxt(n, t):
        return jnp.minimum(n + 1, nb - 1)

    in_specs = [
        pl.BlockSpec((blk, Q_DIM), lambda n, t: (n, 0)),
        pl.BlockSpec((blk, KV_DIM), lambda n, t: (prev(n, t), kcol)),
        pl.BlockSpec((blk, KV_DIM), lambda n, t: (n, kcol)),
        pl.BlockSpec((blk, KV_DIM), lambda n, t: (nxt(n, t), kcol)),
        pl.BlockSpec((blk, KV_DIM), lambda n, t: (prev(n, t), vcol)),
        pl.BlockSpec((blk, KV_DIM), lambda n, t: (n, vcol)),
        pl.BlockSpec((blk, KV_DIM), lambda n, t: (nxt(n, t), vcol)),
        pl.BlockSpec((N_KV_HEADS, GQA_GROUP * blk, 3 * blk), lambda n, t: (0, 0, 0)),
    ]
    return pl.pallas_call(
        functools.partial(_attn_kernel, nb=nb),
        grid_spec=pltpu.PrefetchScalarGridSpec(
            num_scalar_prefetch=1,
            grid=(nb,),
            in_specs=in_specs,
            out_specs=pl.BlockSpec((blk, Q_DIM), lambda n, t: (n, 0)),
        ),
        out_shape=jax.ShapeDtypeStruct((s, Q_DIM), BF16),
        compiler_params=_cparams(("arbitrary",)),
        name="windowed_attention",
    )(sink, z, z, z, z, z, z, z, bias.reshape(N_KV_HEADS, GQA_GROUP * blk, 3 * blk))


def _fourier_tables(s):
    n2 = LANES
    n1 = s // n2
    k1 = np.arange(n1)[None, :, None]
    s1 = np.arange(n1)[None, None, :]
    s2 = np.arange(n2)[:, None, None]
    ang = 2.0 * np.pi * ((k1 * (s2 + n2 * s1)) % s) / s
    ma = np.concatenate([np.cos(ang), -np.sin(ang)], axis=1)
    a = 2.0 * np.pi * ((np.arange(n2)[:, None] * np.arange(n2)[None, :]) % n2) / n2
    c, sn = np.cos(a), np.sin(a)
    mc = np.block([[c, sn], [-sn, c]])
    scale = 1.0 / math.sqrt(s * FOURIER_GROUP_DIM)
    md = np.concatenate([c, sn], axis=0) * scale
    return (jnp.asarray(ma, BF16), jnp.asarray(mc, BF16), jnp.asarray(md, BF16))


FOURIER_K1_BATCH = 4


def _fourier_kernel(x_ref, ma_ref, mc_ref, md_ref, o_ref, xf, yr, yi, ot, *, n1):
    n2 = LANES
    pitch = n2 + SUBLANES
    pitch_o = n1 + SUBLANES
    kb = FOURIER_K1_BATCH
    for s1 in range(n1):
        xf[s1 * pitch:s1 * pitch + n2, :] = x_ref[s1 * n2:(s1 + 1) * n2, :].astype(F32)

    def stage_a(s2, carry):
        xs = xf[pl.ds(s2, n1, stride=pitch), :].astype(BF16)
        y = _dot(ma_ref[s2], xs)
        yr[pl.ds(s2, n1, stride=pitch), :] = y[:n1]
        yi[pl.ds(s2, n1, stride=pitch), :] = y[n1:]
        return carry

    lax.fori_loop(0, n2, stage_a, 0, unroll=8)

    def stage_c(b, carry):
        def slab(ref, j):
            return ref[pl.ds(pl.multiple_of((b * kb + j) * pitch, SUBLANES), n2), :]

        y = jnp.concatenate(
            [jnp.concatenate([slab(yr, j), slab(yi, j)], axis=0) for j in range(kb)], axis=1).astype(BF16)
        xc = _dot(mc_ref[...], y)
        xx = jnp.concatenate(
            [jnp.concatenate([xc[:n2, j * LANES:(j + 1) * LANES], xc[n2:, j * LANES:(j + 1) * LANES]], axis=1)
             for j in range(kb)], axis=0).astype(BF16)
        out = _dot(xx, md_ref[...])
        for j in range(kb):
            ot[pl.ds(b * kb + j, n2, stride=pitch_o), :] = out[j * n2:(j + 1) * n2]
        return carry

    lax.fori_loop(0, n1 // kb, stage_c, 0, unroll=4)
    for k2 in range(n2):
        o_ref[k2 * n1:(k2 + 1) * n1, :] = ot[k2 * pitch_o:k2 * pitch_o + n1, :]


def fourier_mix(z, col0):
    s, w = z.shape
    n2 = LANES
    n1 = s // n2
    assert n1 % FOURIER_K1_BATCH == 0 and n1 % SUBLANES == 0
    c = FOURIER_GROUP_DIM
    ma, mc, md = _fourier_tables(s)
    cb0 = col0 // c
    slabs = pltpu.VMEM((n1 * (n2 + SUBLANES), c), F32)
    return pl.pallas_call(
        functools.partial(_fourier_kernel, n1=n1),
        grid=(N_FOURIER_GROUPS,),
        in_specs=[
            pl.BlockSpec((s, c), lambda g: (0, cb0 + g)),
            pl.BlockSpec((n2, 2 * n1, n1), lambda g: (0, 0, 0)),
            pl.BlockSpec((2 * n2, 2 * n2), lambda g: (0, 0)),
            pl.BlockSpec((2 * c, c), lambda g: (0, 0)),
        ],
        out_specs=pl.BlockSpec((s, c), lambda g: (0, g)),
        out_shape=jax.ShapeDtypeStruct((s, FOURIER_DIM), F32),
        scratch_shapes=[slabs, slabs, slabs, pltpu.VMEM((n2 * (n1 + SUBLANES), c), F32)],
        compiler_params=_cparams(("arbitrary",)),
        name="fourier_mix",
    )(z, ma, mc, md)


def _convpool_kernel(zp_ref, zc_ref, zn_ref, cw_ref, pw_ref, ps_ref, o_ref, *, seq, tt):
    i = pl.program_id(0)
    ext_rows = tt + 2 * HALO
    grow = i * tt - HALO + lax.broadcasted_iota(I32, (ext_rows, 1), 0)
    row_ok = (grow >= 0) & (grow < seq)

    def ext(c0, c1):
        e = jnp.concatenate([zp_ref[:, c0:c1], zc_ref[:, c0:c1], zn_ref[:, c0:c1]], axis=0).astype(F32)
        return jnp.where(row_ok, e, 0.0)

    def shifted(e, d):
        return e[HALO + d:HALO + d + tt]

    prod = ext(CONV_DIM, 2 * CONV_DIM) * ext(2 * CONV_DIM, 3 * CONV_DIM)
    cw = cw_ref[...]
    conv = shifted(prod, -1) * cw[0:1] + shifted(prod, 0) * cw[1:2] + shifted(prod, 1) * cw[2:3]
    o_ref[:, :CONV_DIM] = (zc_ref[:, :CONV_DIM].astype(F32) * conv).astype(o_ref.dtype)

    t = grow[HALO:HALO + tt]
    for g, win in enumerate(POOL_WINDOWS):
        lo = win // 2
        hi = win - 1 - lo
        c0 = 3 * CONV_DIM + g * POOL_GROUP_DIM
        e = ext(c0, c0 + POOL_GROUP_DIM)
        assert win & (win - 1) == 0 and win <= HALO
        run, step = e, 1
        while step < win:
            run = run + pltpu.roll(run, ext_rows - step, axis=0)
            step *= 2
        total = shifted(run, -lo)
        count = (jnp.minimum(t + hi, seq - 1) - jnp.maximum(t - lo, 0) + 1).astype(F32)
        pooled = total / count - shifted(e, 0)
        dg = _dot(pooled.astype(BF16), pw_ref[g].astype(BF16)) * ps_ref[:, g * POOL_GROUP_DIM:(g + 1) * POOL_GROUP_DIM]
        o_ref[:, CONV_DIM + g * POOL_GROUP_DIM:CONV_DIM + (g + 1) * POOL_GROUP_DIM] = dg.astype(o_ref.dtype)


def conv_pool(z, conv_w, pool_w, pool_scale, *, tt):
    s, w = z.shape
    nh = tt // HALO
    last_h = s // HALO - 1
    return pl.pallas_call(
        functools.partial(_convpool_kernel, seq=s, tt=tt),
        grid=(s // tt,),
        in_specs=[
            pl.BlockSpec((HALO, w), lambda i: (jnp.maximum(i * nh - 1, 0), 0)),
            pl.BlockSpec((tt, w), lambda i: (i, 0)),
            pl.BlockSpec((HALO, w), lambda i: (jnp.minimum((i + 1) * nh, last_h), 0)),
            pl.BlockSpec((3, CONV_DIM), lambda i: (0, 0)),
            pl.BlockSpec((POOL_GROUPS, POOL_GROUP_DIM, POOL_GROUP_DIM), lambda i: (0, 0, 0)),
            pl.BlockSpec((1, POOL_DIM), lambda i: (0, 0)),
        ],
        out_specs=pl.BlockSpec((tt, CONV_DIM + POOL_DIM), lambda i: (i, 0)),
        out_shape=jax.ShapeDtypeStruct((s, CONV_DIM + POOL_DIM), BF16),
        compiler_params=_cparams(("arbitrary",)),
        name="conv_pool",
    )(z, z, z, conv_w, pool_w, pool_scale.reshape(1, POOL_DIM))


def _excl_cumsum_lanes(m, upper):
    r, s = m.shape
    off = jnp.zeros((r, 1), F32)
    pieces = []
    for c in range(s // LANES):
        mc = m[:, c * LANES:(c + 1) * LANES]
        pieces.append(_dot(mc.astype(BF16), upper) + off)
        off = off + jnp.sum(mc, axis=1, keepdims=True)
    return jnp.concatenate(pieces, axis=1)


def _select_kernel(aff_ref, upper_ref, pos_ref, sel_ref, v_ref, *, cap):
    a = aff_ref[...]
    e, s = a.shape
    bits = pltpu.bitcast(a, I32)
    thr = jnp.zeros((e, 1), I32)
    for bit in range(30, -1, -1):
        cand = thr | (1 << bit)
        cnt = jnp.sum((bits >= cand).astype(F32), axis=1, keepdims=True)
        thr = jnp.where(cnt >= cap, cand, thr)
    gt = bits > thr
    eq = (bits == thr).astype(F32)
    need = cap - jnp.sum(gt.astype(F32), axis=1, keepdims=True)
    upper = upper_ref[...]
    eq_rank = _excl_cumsum_lanes(eq, upper)
    sel = jnp.where(gt | ((eq > 0.5) & (eq_rank < need)), 1.0, 0.0)
    pos = _excl_cumsum_lanes(sel, upper)
    pos_ref[...] = pos
    sel_ref[...] = sel
    tok = lax.broadcasted_iota(I32, (1, s), 1).astype(F32)
    tok_hi = jnp.floor(tok * (1.0 / LANES))
    tok_lo = tok - tok_hi * LANES
    g1 = a.astype(BF16).astype(F32)
    g2 = (a - g1).astype(BF16).astype(F32)
    g3 = a - g1 - g2
    zero = jnp.zeros((SUBLANES - 5, s), F32)
    for x in range(e):
        v_ref[x] = jnp.concatenate([tok_hi, tok_lo, g1[x:x + 1], g2[x:x + 1], g3[x:x + 1], zero], axis=0)


def select_tokens(aff, cap):
    e, s = aff.shape
    upper = jnp.asarray(np.triu(np.ones((LANES, LANES), np.float32), k=1), BF16)
    full2 = lambda shp: pl.BlockSpec(shp, lambda i: (0,) * len(shp))
    return pl.pallas_call(
        functools.partial(_select_kernel, cap=cap),
        grid=(1,),
        in_specs=[full2((e, s)), full2((LANES, LANES))],
        out_specs=[full2((e, s)), full2((e, s)), full2((e, SUBLANES, s))],
        out_shape=[
            jax.ShapeDtypeStruct((e, s), F32),
            jax.ShapeDtypeStruct((e, s), F32),
            jax.ShapeDtypeStruct((e, SUBLANES, s), F32),
        ],
        compiler_params=_cparams(("arbitrary",)),
        name="select_tokens",
    )(aff, upper)


def _lists_kernel(off_ref, cross_ref, pos_ref, sel_ref, v_ref, o_ref, acc, *, nchunk, ntile, group):
    e0 = pl.program_id(0) * group
    acc[...] = jnp.zeros(acc.shape, F32)
    slot = lax.broadcasted_iota(I32, (LANES, LANES), 0).astype(F32)

    def add_chunk(x, c, tile):
        st = pl.multiple_of(c * LANES, LANES)
        p = pos_ref[x, :, pl.ds(st, LANES)]
        chosen = sel_ref[x, :, pl.ds(st, LANES)] > 0.5
        v = v_ref[x, :, pl.ds(st, LANES)].astype(BF16)
        base = (tile * LANES).astype(F32)
        onehot = jnp.where(((p - base) == slot) & chosen, 1.0, 0.0).astype(BF16)
        acc[x, tile] += _dot_nt(v, onehot)

    def body(c, carry):
        for x in range(group):
            add_chunk(x, c, jnp.minimum(off_ref[(e0 + x) * nchunk + c] // LANES, ntile - 1))
        return carry

    lax.fori_loop(0, nchunk, body, 0, unroll=4)
    for j in range(1, ntile):
        for x in range(group):
            add_chunk(x, cross_ref[(e0 + x) * ntile + j], jnp.int32(j))
    for x in range(group):
        for j in range(ntile):
            o_ref[x, :, j * LANES:(j + 1) * LANES] = acc[x, j]


def build_lists(pos, sel, vals, cap):
    e, s = pos.shape
    nchunk = s // LANES
    ntile = cap // LANES
    off = pos[:, ::LANES].astype(I32)
    bounds = jnp.arange(ntile, dtype=I32) * LANES
    cross = jnp.maximum(jnp.sum((off[:, :, None] < bounds[None, None, :]).astype(I32), axis=1) - 1, 0)
    group = LIST_GROUP
    assert e % group == 0
    row = lambda nrow: pl.BlockSpec((group, nrow, s), lambda x, t, u: (x, 0, 0))
    lists = pl.pallas_call(
        functools.partial(_lists_kernel, nchunk=nchunk, ntile=ntile, group=group),
        grid_spec=pltpu.PrefetchScalarGridSpec(
            num_scalar_prefetch=2,
            grid=(e // group,),
            in_specs=[row(1), row(1), row(SUBLANES)],
            out_specs=pl.BlockSpec((group, SUBLANES, cap), lambda x, t, u: (x, 0, 0)),
            scratch_shapes=[pltpu.VMEM((group, ntile, SUBLANES, LANES), F32)],
        ),
        out_shape=jax.ShapeDtypeStruct((e, SUBLANES, cap), F32),
        compiler_params=_cparams(("arbitrary",)),
        name="build_lists",
    )(off.reshape(-1), cross.reshape(-1), pos.reshape(e, 1, s), sel.reshape(e, 1, s), vals)
    idx = (lists[:, 0] * LANES + lists[:, 1]).astype(I32)
    gate = (lists[:, 2] + lists[:, 3] + lists[:, 4])
    return idx, gate


def _ffn_kernel(idx_ref, h_hbm, gate_ref, wg_ref, wu_ref, wd_ref, y_ref, xbuf, xb, hid, sem_g, *, cap, nh, ne):
    e = pl.program_id(0)
    hs = pl.program_id(1)
    slot = e % 2
    rps = cap // nh

    def gather_row(expert, r, s):
        t = idx_ref[expert * cap + r]
        return pltpu.make_async_copy(h_hbm.at[pl.ds(pl.multiple_of(t * SUBLANES, SUBLANES), SUBLANES)],
                                     xbuf.at[s, pl.ds(pl.multiple_of(r * SUBLANES, SUBLANES), SUBLANES)],
                                     sem_g.at[s])

    def wait_gather(s):
        pltpu.make_async_copy(h_hbm.at[pl.ds(0, cap * SUBLANES)], xbuf.at[s], sem_g.at[s]).wait()

    @pl.when((e == 0) & (hs == 0))
    def _():
        def issue(r, carry):
            gather_row(0, r, 0).start()
            return carry

        lax.fori_loop(0, cap, issue, 0, unroll=8)

    @pl.when(hs == 0)
    def _():
        wait_gather(slot)
        lo, hi = _unpack_bf16_pairs(_load_row_tiles(xbuf.at[slot]))
        xb[...] = jnp.concatenate([lo, hi], axis=1)

    th = wg_ref.shape[1]
    w_gu = jnp.concatenate([wg_ref[...].astype(BF16), wu_ref[...].astype(BF16)], axis=1)
    gu = _dot(xb[...], w_gu)
    gp = gu[:, :th]
    hid[:, pl.ds(pl.multiple_of(hs * th, th), th)] = (gp * (1.0 / (1.0 + jnp.exp(-gp))) * gu[:, th:]).astype(BF16)
    nxt = lax.rem(e + 1, ne)
    for k in range(rps):
        gather_row(nxt, hs * rps + k, 1 - slot).start()

    @pl.when(hs == nh - 1)
    def _():
        w_d = wd_ref[...].astype(BF16)
        rb = cap // nh
        for b in range(nh):
            rows = slice(b * rb, (b + 1) * rb)
            y = _dot(hid[rows, :], w_d) * gate_ref[0, rows, :]
            _store_row_tiles(y_ref, _pack_bf16_pairs(y), b * rb)

    @pl.when((e == ne - 1) & (hs == nh - 1))
    def _():
        wait_gather(1 - slot)


def expert_ffn(h, idx, gate, w_gate, w_up, w_down, li, *, th):
    ne, cap = idx.shape
    d = w_gate.shape[2]
    hidden = w_gate.shape[3]
    nh = hidden // th
    assert d == 2 * SUBLANES * LANES
    return pl.pallas_call(
        functools.partial(_ffn_kernel, cap=cap, nh=nh, ne=ne),
        grid_spec=pltpu.PrefetchScalarGridSpec(
            num_scalar_prefetch=1,
            grid=(ne, nh),
            in_specs=[
                pl.BlockSpec(memory_space=pl.ANY),
                pl.BlockSpec((1, cap, 1), lambda x, j, a: (x, 0, 0)),
                pl.BlockSpec((None, None, d, th), lambda x, j, a: (li, x, 0, j)),
                pl.BlockSpec((None, None, d, th), lambda x, j, a: (li, x, 0, j)),
                pl.BlockSpec((None, None, hidden, d), lambda x, j, a: (li, x, 0, 0)),
            ],
            out_specs=pl.BlockSpec((cap * SUBLANES, LANES), lambda x, j, a: (x, 0),
                                   pipeline_mode=pl.Buffered(1)),
            scratch_shapes=[
                pltpu.VMEM((2, cap * SUBLANES, LANES), I32),
                pltpu.VMEM((cap, d), BF16),
                pltpu.VMEM((cap, hidden), BF16),
                pltpu.SemaphoreType.DMA((2,)),
            ],
        ),
        out_shape=jax.ShapeDtypeStruct((ne * cap * SUBLANES, LANES), I32),
        compiler_params=_cparams(("arbitrary", "arbitrary")),
        name="expert_ffn",
    )(idx.reshape(-1), h, gate.reshape(ne, cap, 1), w_gate, w_up, w_down)


def _combine_kernel(off_ref, x_ref, pos_ref, g_ref, y_hbm, *rest, ne, cap, nchunk, final):
    nout = 1 if final else 2
    o_refs = rest[:nout]
    buf, obuf, acc, member, sem, osem = rest[nout:]
    step = pl.program_id(0)
    slot = step % 2
    grp = COMBINE_GROUP
    tc = member.shape[1]
    windows = _CombineWindows(off_ref, ne, cap, nchunk)

    def fetch(e, st, g, s):
        return pltpu.make_async_copy(y_hbm.at[windows.tiles(windows.row(e, st * grp + g, 0), COMBINE_WINDOW)],
                                     buf.at[s, g, windows.tiles(e * COMBINE_WINDOW, COMBINE_WINDOW)], sem.at[s])

    @pl.when(step == 0)
    def _():
        for g in range(grp):
            for e in range(ne):
                fetch(e, 0, g, 0).start()

    @pl.when((step + 1) * grp < nchunk)
    def _():
        for g in range(grp):
            for e in range(ne):
                fetch(e, step + 1, g, 1 - slot).start()

    for g in range(grp):
        for e in range(ne):
            fetch(e, step, g, slot).wait()

    for first_pieces in (True, False):
        for g in range(grp):
            _combine_chunk(step * grp + g, g, slot, windows, pos_ref, y_hbm, buf, obuf, acc, member, osem, tc,
                           first_pieces)

    y = x_ref[...] + acc[...]
    if final:
        o_refs[0][...] = _rmsnorm(y, g_ref[...])
    else:
        o_refs[0][...] = y
        o_refs[1][...] = _rmsnorm(y, g_ref[...]).astype(BF16)


class _CombineWindows:
    def __init__(self, off_ref, ne, cap, nchunk):
        self.off_ref, self.ne, self.cap, self.nchunk = off_ref, ne, cap, nchunk

    def off(self, e, cc):
        return self.off_ref[e * (self.nchunk + 1) + cc]

    def row(self, e, cc, p):
        return jnp.minimum(e * self.cap + self.off(e, cc) + p * COMBINE_WINDOW, self.ne * self.cap - COMBINE_WINDOW)

    @staticmethod
    def tiles(row, count):
        return pl.ds(pl.multiple_of(row * SUBLANES, SUBLANES), count * SUBLANES)


def _combine_chunk(c, g, slot, windows, pos_ref, y_hbm, buf, obuf, acc, member, osem, tc, first_pieces):
    win = COMBINE_WINDOW
    ne, cap = windows.ne, windows.cap
    chunk_off, win_row, tiles = windows.off, windows.row, windows.tiles
    half = acc.shape[1] // 2
    rows = slice(g * tc, (g + 1) * tc)
    pos = pos_ref[rows, :]

    def window_col(e, p):
        local = pos[:, e:e + 1] - chunk_off(e, c).astype(F32)
        col = pos[:, e:e + 1] + (e * cap - win_row(e, c, p)).astype(F32)
        ok = (local >= p * win) & (local < (p + 1) * win)
        return jnp.where(ok, col, -1.0)

    if first_pieces:
        lane = lax.broadcasted_iota(I32, (1, LANES), 1)
        first = lane < win
        w = jnp.where(first, lane, lane - win).astype(F32)
        for q in range(ne // 2):
            col = jnp.where(first, window_col(2 * q, 0), window_col(2 * q + 1, 0))
            member[g, :, q * LANES:(q + 1) * LANES] = jnp.where(col == w, 1.0, 0.0).astype(BF16)
        a, b = _unpack_bf16_pairs(_load_row_tiles(buf.at[slot, g]))
        acc[rows, :half] = _dot(member[g], a)
        acc[rows, half:] = _dot(member[g], b)
        return

    for e in range(ne):
        npieces = (chunk_off(e, c + 1) - chunk_off(e, c) + win - 1) // win

        def extra(p, carry, e=e):
            cp = pltpu.make_async_copy(y_hbm.at[tiles(win_row(e, c, p), win)], obuf, osem)
            cp.start()
            cp.wait()
            wi = lax.broadcasted_iota(I32, (1, win), 1).astype(F32)
            m = jnp.where(window_col(e, p) == wi, 1.0, 0.0).astype(BF16)
            oa, ob = _unpack_bf16_pairs(_load_row_tiles(obuf))
            acc[rows, :half] += _dot(m, oa)
            acc[rows, half:] += _dot(m, ob)
            return carry

        lax.fori_loop(1, npieces, extra, 0)


def combine(x, y_rows, pos, sel, cap, g, *, tc, final):
    s, d = x.shape
    ne = pos.shape[0]
    nchunk = s // tc
    grp = COMBINE_GROUP
    assert 2 * COMBINE_WINDOW == LANES and ne % 2 == 0 and ne * cap >= COMBINE_WINDOW and nchunk % grp == 0
    off = jnp.concatenate([pos[:, ::tc].astype(I32), jnp.full((ne, 1), cap, I32)], axis=1).reshape(-1)
    row_block = pl.BlockSpec((grp * tc, d), lambda c, t: (c, 0))
    if final:
        out_specs, out_shape = row_block, jax.ShapeDtypeStruct((s, d), F32)
    else:
        out_specs = [row_block, row_block]
        out_shape = [jax.ShapeDtypeStruct((s, d), F32), jax.ShapeDtypeStruct((s, d), BF16)]
    return pl.pallas_call(
        functools.partial(_combine_kernel, ne=ne, cap=cap, nchunk=nchunk, final=final),
        grid_spec=pltpu.PrefetchScalarGridSpec(
            num_scalar_prefetch=1,
            grid=(nchunk // grp,),
            in_specs=[
                row_block,
                pl.BlockSpec((grp * tc, ne), lambda c, t: (c, 0)),
                pl.BlockSpec((1, d), lambda c, t: (0, 0)),
                pl.BlockSpec(memory_space=pl.ANY),
            ],
            out_specs=out_specs,
            scratch_shapes=[
                pltpu.VMEM((2, grp, ne * COMBINE_WINDOW * SUBLANES, LANES), I32),
                pltpu.VMEM((COMBINE_WINDOW * SUBLANES, LANES), I32),
                pltpu.VMEM((grp * tc, d), F32),
                pltpu.VMEM((grp, tc, ne * COMBINE_WINDOW), BF16),
                pltpu.SemaphoreType.DMA((2,)),
                pltpu.SemaphoreType.DMA(()),
            ],
        ),
        out_shape=out_shape,
        compiler_params=_cparams(("arbitrary",)),
        name="combine",
    )(off, x, jnp.where(sel > 0.5, pos, -1.0).T, g.reshape(1, d), y_rows)


def ec_moe_block(x1, h, aff, w_gate, w_up, w_down, li, g_next, *, final, th, tc):
    s = x1.shape[0]
    cap = CAPACITY_FACTOR * s // N_EXPERTS
    pos, sel, vals = select_tokens(aff, cap)
    idx, gate = build_lists(pos, sel, vals, cap)
    y_rows = expert_ffn(h, idx, gate, w_gate, w_up, w_down, li, th=th)
    return combine(x1, y_rows, pos, sel, cap, g_next, tc=tc, final=final)


def _tiles(s):
    return dict(tm=min(1024, s), tn=1024, to=min(512, s), tt=min(512, s), th=256, tc=min(256, s))


def kernel(x, rel_bias, norm_mix_g, norm_ffn_g, final_norm_g, ev_w_in, ev_sink, ev_w_out, od_w_in, od_conv_w,
           od_pool_w, od_pool_scale, od_w_out, moe_w_router, moe_w_gate, moe_w_up, moe_w_down):
    b, s, d = x.shape
    assert b == 1
    t = _tiles(s)
    depth = norm_mix_g.shape[0]
    bias = window_bias(rel_bias)
    xs = x.reshape(s, d)
    hn = prenorm(xs, norm_mix_g[0], tm=t["tm"])
    for layer in range(depth):
        i = layer // 2
        if layer % 2 == 0:
            z = in_proj(hn, ev_w_in, i, tm=t["tm"], tn=t["tn"])
            parts = [windowed_attention(z, ev_sink[i], bias), fourier_mix(z, Q_DIM + 2 * KV_DIM)]
            w_out = ev_w_out
        else:
            z = in_proj(hn, od_w_in, i, tm=t["tm"], tn=t["tn"])
            parts = [conv_pool(z, od_conv_w[i], od_pool_w[i], od_pool_scale[i], tt=t["tt"])]
            w_out = od_w_out
        x1 = out_proj(xs, parts, w_out, i, tm=t["tm"], tn=t["tn"])
        h, aff = router(x1, norm_ffn_g[layer], moe_w_router[layer], tm=t["to"])
        final = layer == depth - 1
        g_next = final_norm_g if final else norm_mix_g[layer + 1]
        res = ec_moe_block(x1, h, aff.T, moe_w_gate, moe_w_up, moe_w_down, layer, g_next,
                           final=final, th=t["th"], tc=t["tc"])
        if final:
            xs = res
        else:
            xs, hn = res
    return xs.reshape(b, s, d)
```

```python
import functools
import math

import numpy as np
import jax
import jax.numpy as jnp
from jax import lax
from jax.experimental import pallas as pl
from jax.experimental.pallas import tpu as pltpu

F32 = jnp.float32
BF16 = jnp.bfloat16
I32 = jnp.int32

HEAD_DIM = 128
N_Q_HEADS = 12
N_KV_HEADS = 4
GQA_GROUP = N_Q_HEADS // N_KV_HEADS
WINDOW = 128
ATTN_BLOCK = 128
N_FOURIER_GROUPS = 4
FOURIER_GROUP_DIM = 128
Q_DIM = N_Q_HEADS * HEAD_DIM
KV_DIM = N_KV_HEADS * HEAD_DIM
FOURIER_DIM = N_FOURIER_GROUPS * FOURIER_GROUP_DIM
N_REL_BUCKETS = 32
REL_MAX_DISTANCE = 128
CONV_DIM = 1024
POOL_WINDOWS = (2, 4, 8, 16)
POOL_GROUPS = len(POOL_WINDOWS)
POOL_GROUP_DIM = 256
POOL_DIM = POOL_GROUPS * POOL_GROUP_DIM
N_EXPERTS = 16
CAPACITY_FACTOR = 2
RMS_EPS = 1e-6
NEG_INF = -1e30
LOG2E = math.log2(math.e)

LANES = 128
SUBLANES = 8
BF16_ROWS = 16
VMEM_LIMIT = 56 * 1024 * 1024

HALO = BF16_ROWS
COMBINE_WINDOW = 64
LIST_GROUP = 16
COMBINE_GROUP = 2


def _cparams(sem):
    return pltpu.CompilerParams(dimension_semantics=sem, vmem_limit_bytes=VMEM_LIMIT)


def _dot(a, b):
    return jnp.dot(a, b, preferred_element_type=F32)


def _dot_nt(a, b):
    return lax.dot_general(a, b, (((1,), (1,)), ((), ())), preferred_element_type=F32)


def _rmsnorm(x, g):
    ms = jnp.mean(x * x, axis=-1, keepdims=True)
    return x * lax.rsqrt(ms + RMS_EPS) * g


def _pack_bf16_pairs(y):
    n = y.shape[1] // 2
    lo = lax.bitcast_convert_type(y[:, :n].astype(BF16).astype(F32), I32)
    hi = lax.bitcast_convert_type(y[:, n:].astype(BF16).astype(F32), I32)
    return lax.shift_right_logical(lo, 16) | hi


def _unpack_bf16_pairs(p):
    lo = lax.bitcast_convert_type(lax.shift_left(p, 16), F32).astype(BF16)
    hi = lax.bitcast_convert_type(p & (-65536), F32).astype(BF16)
    return lo, hi


def _store_row_tiles(ref, packed, row0=0):
    rows = packed.shape[0]
    for j in range(SUBLANES):
        ref[pl.ds(row0 * SUBLANES + j, rows, stride=SUBLANES), :] = packed[:, j * LANES:(j + 1) * LANES]


def _load_row_tiles(ref):
    rows = ref.shape[0] // SUBLANES
    return jnp.concatenate([ref[pl.ds(j, rows, stride=SUBLANES), :] for j in range(SUBLANES)], axis=1)


def _prenorm_kernel(x_ref, g_ref, o_ref):
    o_ref[...] = _rmsnorm(x_ref[...], g_ref[...]).astype(o_ref.dtype)


def prenorm(x, g, *, tm):
    s, d = x.shape
    return pl.pallas_call(
        _prenorm_kernel,
        grid=(s // tm,),
        in_specs=[pl.BlockSpec((tm, d), lambda i: (i, 0)), pl.BlockSpec((1, d), lambda i: (0, 0))],
        out_specs=pl.BlockSpec((tm, d), lambda i: (i, 0)),
        out_shape=jax.ShapeDtypeStruct((s, d), BF16),
        compiler_params=_cparams(("parallel",)),
        name="prenorm",
    )(x, g.reshape(1, d))


def _in_proj_kernel(h_ref, w_ref, o_ref, wb_ref):
    @pl.when(pl.program_id(1) == 0)
    def _():
        wb_ref[...] = w_ref[...].astype(BF16)

    o_ref[...] = _dot(h_ref[...], wb_ref[...]).astype(o_ref.dtype)


def in_proj(h, w, li, *, tm, tn):
    s, d = h.shape
    n = w.shape[2]
    return pl.pallas_call(
        _in_proj_kernel,
        grid=(n // tn, s // tm),
        in_specs=[
            pl.BlockSpec((tm, d), lambda j, i: (i, 0)),
            pl.BlockSpec((None, d, tn), lambda j, i: (li, 0, j)),
        ],
        out_specs=pl.BlockSpec((tm, tn), lambda j, i: (i, j)),
        out_shape=jax.ShapeDtypeStruct((s, n), BF16),
        scratch_shapes=[pltpu.VMEM((d, tn), BF16)],
        compiler_params=_cparams(("parallel", "arbitrary")),
        name="in_proj",
    )(h, w)


def _out_proj_kernel(*refs, widths):
    nparts = len(widths)
    x_ref = refs[0]
    p_refs = refs[1:1 + nparts]
    w_ref, o_ref, wb_ref = refs[1 + nparts:]

    @pl.when(pl.program_id(1) == 0)
    def _():
        wb_ref[...] = w_ref[...].astype(BF16)

    acc = x_ref[...]
    off = 0
    for p_ref, width in zip(p_refs, widths):
        acc = acc + _dot(p_ref[...].astype(BF16), wb_ref[off:off + width, :])
        off += width
    o_ref[...] = acc


def out_proj(x, parts, w, li, *, tm, tn):
    s, d = x.shape
    k = w.shape[1]
    widths = tuple(p.shape[1] for p in parts)
    assert sum(widths) == k
    in_specs = [pl.BlockSpec((tm, tn), lambda j, i: (i, j))]
    in_specs += [pl.BlockSpec((tm, wd), lambda j, i: (i, 0)) for wd in widths]
    in_specs += [pl.BlockSpec((None, k, tn), lambda j, i: (li, 0, j))]
    return pl.pallas_call(
        functools.partial(_out_proj_kernel, widths=widths),
        grid=(d // tn, s // tm),
        in_specs=in_specs,
        out_specs=pl.BlockSpec((tm, tn), lambda j, i: (i, j)),
        out_shape=jax.ShapeDtypeStruct((s, d), F32),
        scratch_shapes=[pltpu.VMEM((k, tn), BF16)],
        compiler_params=_cparams(("parallel", "arbitrary")),
        name="out_proj",
    )(x, *parts, w)


def _split_bf16(v):
    hi = v.astype(BF16)
    return hi, (v - hi.astype(F32)).astype(BF16)


def _router_kernel(x_ref, g_ref, wr_ref, h_ref, aff_ref):
    h = _rmsnorm(x_ref[...], g_ref[...])
    _store_row_tiles(h_ref, _pack_bf16_pairs(h))
    h_hi, h_lo = _split_bf16(h)
    w_hi, w_lo = _split_bf16(wr_ref[...])
    ne = w_hi.shape[1]
    hh = _dot(h_hi, jnp.concatenate([w_hi, w_lo], axis=1))
    logits = hh[:, :ne] + (hh[:, ne:] + _dot(h_lo, w_hi))
    m = jnp.max(logits, axis=1, keepdims=True)
    p = jnp.exp(logits - m)
    aff_ref[...] = p / jnp.sum(p, axis=1, keepdims=True)


def router(x, g, w_router, *, tm):
    s, d = x.shape
    assert d == 2 * SUBLANES * LANES
    e = w_router.shape[1]
    return pl.pallas_call(
        _router_kernel,
        grid=(s // tm,),
        in_specs=[
            pl.BlockSpec((tm, d), lambda i: (i, 0)),
            pl.BlockSpec((1, d), lambda i: (0, 0)),
            pl.BlockSpec((d, e), lambda i: (0, 0)),
        ],
        out_specs=[pl.BlockSpec((tm * SUBLANES, LANES), lambda i: (i, 0)), pl.BlockSpec((tm, e), lambda i: (i, 0))],
        out_shape=[jax.ShapeDtypeStruct((s * SUBLANES, LANES), I32), jax.ShapeDtypeStruct((s, e), F32)],
        compiler_params=_cparams(("parallel",)),
        name="router",
    )(x, g.reshape(1, d), w_router)


def _t5_bucket(rel):
    nb = N_REL_BUCKETS // 2
    max_exact = nb // 2
    ret = (rel > 0).astype(jnp.int32) * nb
    n = jnp.abs(rel)
    nf = jnp.maximum(n, 1).astype(jnp.float32)
    large = max_exact + (jnp.log(nf / max_exact) / math.log(REL_MAX_DISTANCE / max_exact)
                         * (nb - max_exact)).astype(jnp.int32)
    large = jnp.minimum(large, nb - 1)
    return ret + jnp.where(n < max_exact, n, large)


def _bias_kernel(tab_ref, bucket_ref, o_ref):
    h = pl.program_id(0)
    bucket = bucket_ref[...]
    acc = jnp.zeros(bucket.shape, F32)
    for b in range(N_REL_BUCKETS):
        acc = jnp.where(bucket == b, tab_ref[b * N_Q_HEADS + h], acc)
    i = lax.broadcasted_iota(I32, bucket.shape, 0)
    j = lax.broadcasted_iota(I32, bucket.shape, 1)
    valid = jnp.abs(j - WINDOW - i) <= WINDOW
    o_ref[0] = jnp.where(valid, acc * LOG2E, NEG_INF)


def window_bias(rel_bias):
    i = jnp.arange(ATTN_BLOCK, dtype=jnp.int32)[:, None]
    j = jnp.arange(3 * ATTN_BLOCK, dtype=jnp.int32)[None, :]
    bucket = _t5_bucket((j - WINDOW) - i)
    return pl.pallas_call(
        _bias_kernel,
        grid_spec=pltpu.PrefetchScalarGridSpec(
            num_scalar_prefetch=1,
            grid=(N_Q_HEADS,),
            in_specs=[pl.BlockSpec((ATTN_BLOCK, 3 * ATTN_BLOCK), lambda h, t: (0, 0))],
            out_specs=pl.BlockSpec((1, ATTN_BLOCK, 3 * ATTN_BLOCK), lambda h, t: (h, 0, 0)),
        ),
        out_shape=jax.ShapeDtypeStruct((N_Q_HEADS, ATTN_BLOCK, 3 * ATTN_BLOCK), F32),
        compiler_params=_cparams(("arbitrary",)),
        name="window_bias",
    )(rel_bias.reshape(-1), bucket)


def _attn_kernel(sink_ref, q_ref, kp_ref, kc_ref, kn_ref, vp_ref, vc_ref, vn_ref, bias_ref, o_ref, *, nb):
    n = pl.program_id(0)
    blk = ATTN_BLOCK
    ones = jnp.ones((3 * blk, HEAD_DIM), BF16)
    scale2 = HEAD_DIM ** -0.5 * LOG2E

    def block(at_edge):
        if at_edge:
            col = lax.broadcasted_iota(I32, (1, 3 * blk), 1)
            in_seq = ((col >= blk) | (n > 0)) & ((col < 2 * blk) | (n < nb - 1))
            edge_bias = jnp.where(in_seq, 0.0, NEG_INF)
        for kv in range(N_KV_HEADS):
            cs = slice(kv * HEAD_DIM, (kv + 1) * HEAD_DIM)
            k = jnp.concatenate([kp_ref[:, cs], kc_ref[:, cs], kn_ref[:, cs]], axis=0)
            v = jnp.concatenate([vp_ref[:, cs], vc_ref[:, cs], vn_ref[:, cs]], axis=0)
            v1 = jnp.concatenate([v, ones], axis=1)
            for g in range(GQA_GROUP):
                hq = kv * GQA_GROUP + g
                hs = slice(hq * HEAD_DIM, (hq + 1) * HEAD_DIM)
                s2 = _dot_nt(q_ref[:, hs], k) * scale2 + bias_ref[kv, g * blk:(g + 1) * blk, :]
                if at_edge:
                    s2 = s2 + edge_bias
                sk2 = sink_ref[hq] * LOG2E
                m2 = jnp.maximum(jnp.max(s2, axis=-1, keepdims=True), sk2)
                p = jnp.exp2(s2 - m2).astype(BF16)
                ov = _dot(p, v1)
                denom = ov[:, HEAD_DIM:HEAD_DIM + 1] + jnp.exp2(sk2 - m2)
                o_ref[:, hs] = (ov[:, :HEAD_DIM] / denom).astype(o_ref.dtype)

    @pl.when((n > 0) & (n < nb - 1))
    def _():
        block(False)

    @pl.when((n == 0) | (n == nb - 1))
    def _():
        block(True)


def windowed_attention(z, sink, bias):
    s = z.shape[0]
    blk = ATTN_BLOCK
    nb = s // blk
    kcol = Q_DIM // KV_DIM
    vcol = kcol + 1

    def prev(n, t):
        return jnp.maximum(n - 1, 0)

    def nxt(n, t):
        return jnp.minimum(n + 1, nb - 1)

    in_specs = [
        pl.BlockSpec((blk, Q_DIM), lambda n, t: (n, 0)),
        pl.BlockSpec((blk, KV_DIM), lambda n, t: (prev(n, t), kcol)),
        pl.BlockSpec((blk, KV_DIM), lambda n, t: (n, kcol)),
        pl.BlockSpec((blk, KV_DIM), lambda n, t: (nxt(n, t), kcol)),
        pl.BlockSpec((blk, KV_DIM), lambda n, t: (prev(n, t), vcol)),
        pl.BlockSpec((blk, KV_DIM), lambda n, t: (n, vcol)),
        pl.BlockSpec((blk, KV_DIM), lambda n, t: (nxt(n, t), vcol)),
        pl.BlockSpec((N_KV_HEADS, GQA_GROUP * blk, 3 * blk), lambda n, t: (0, 0, 0)),
    ]
    return pl.pallas_call(
        functools.partial(_attn_kernel, nb=nb),
        grid_spec=pltpu.PrefetchScalarGridSpec(
            num_scalar_prefetch=1,
            grid=(nb,),
            in_specs=in_specs,
            out_specs=pl.BlockSpec((blk, Q_DIM), lambda n, t: (n, 0)),
        ),
        out_shape=jax.ShapeDtypeStruct((s, Q_DIM), BF16),
        compiler_params=_cparams(("arbitrary",)),
        name="windowed_attention",
    )(sink, z, z, z, z, z, z, z, bias.reshape(N_KV_HEADS, GQA_GROUP * blk, 3 * blk))


def _fourier_tables(s):
    n2 = LANES
    n1 = s // n2
    k1 = np.arange(n1)[None, :, None]
    s1 = np.arange(n1)[None, None, :]
    s2 = np.arange(n2)[:, None, None]
    ang = 2.0 * np.pi * ((k1 * (s2 + n2 * s1)) % s) / s
    ma = np.concatenate([np.cos(ang), -np.sin(ang)], axis=1)
    a = 2.0 * np.pi * ((np.arange(n2)[:, None] * np.arange(n2)[None, :]) % n2) / n2
    c, sn = np.cos(a), np.sin(a)
    mc = np.block([[c, sn], [-sn, c]])
    scale = 1.0 / math.sqrt(s * FOURIER_GROUP_DIM)
    md = np.concatenate([c, sn], axis=0) * scale
    return (jnp.asarray(ma, BF16), jnp.asarray(mc, BF16), jnp.asarray(md, BF16))


FOURIER_K1_BATCH = 4


def _fourier_kernel(x_ref, ma_ref, mc_ref, md_ref, o_ref, xf, yr, yi, ot, *, n1):
    n2 = LANES
    pitch = n2 + SUBLANES
    pitch_o = n1 + SUBLANES
    kb = FOURIER_K1_BATCH
    for s1 in range(n1):
        xf[s1 * pitch:s1 * pitch + n2, :] = x_ref[s1 * n2:(s1 + 1) * n2, :].astype(F32)

    def stage_a(s2, carry):
        xs = xf[pl.ds(s2, n1, stride=pitch), :].astype(BF16)
        y = _dot(ma_ref[s2], xs)
        yr[pl.ds(s2, n1, stride=pitch), :] = y[:n1]
        yi[pl.ds(s2, n1, stride=pitch), :] = y[n1:]
        return carry

    lax.fori_loop(0, n2, stage_a, 0, unroll=8)

    def stage_c(b, carry):
        def slab(ref, j):
            return ref[pl.ds(pl.multiple_of((b * kb + j) * pitch, SUBLANES), n2), :]

        y = jnp.concatenate(
            [jnp.concatenate([slab(yr, j), slab(yi, j)], axis=0) for j in range(kb)], axis=1).astype(BF16)
        xc = _dot(mc_ref[...], y)
        xx = jnp.concatenate(
            [jnp.concatenate([xc[:n2, j * LANES:(j + 1) * LANES], xc[n2:, j * LANES:(j + 1) * LANES]], axis=1)
             for j in range(kb)], axis=0).astype(BF16)
        out = _dot(xx, md_ref[...])
        for j in range(kb):
            ot[pl.ds(b * kb + j, n2, stride=pitch_o), :] = out[j * n2:(j + 1) * n2]
        return carry

    lax.fori_loop(0, n1 // kb, stage_c, 0, unroll=4)
    for k2 in range(n2):
        o_ref[k2 * n1:(k2 + 1) * n1, :] = ot[k2 * pitch_o:k2 * pitch_o + n1, :]


def fourier_mix(z, col0):
    s, w = z.shape
    n2 = LANES
    n1 = s // n2
    assert n1 % FOURIER_K1_BATCH == 0 and n1 % SUBLANES == 0
    c = FOURIER_GROUP_DIM
    ma, mc, md = _fourier_tables(s)
    cb0 = col0 // c
    slabs = pltpu.VMEM((n1 * (n2 + SUBLANES), c), F32)
    return pl.pallas_call(
        functools.partial(_fourier_kernel, n1=n1),
        grid=(N_FOURIER_GROUPS,),
        in_specs=[
            pl.BlockSpec((s, c), lambda g: (0, cb0 + g)),
            pl.BlockSpec((n2, 2 * n1, n1), lambda g: (0, 0, 0)),
            pl.BlockSpec((2 * n2, 2 * n2), lambda g: (0, 0)),
            pl.BlockSpec((2 * c, c), lambda g: (0, 0)),
        ],
        out_specs=pl.BlockSpec((s, c), lambda g: (0, g)),
        out_shape=jax.ShapeDtypeStruct((s, FOURIER_DIM), F32),
        scratch_shapes=[slabs, slabs, slabs, pltpu.VMEM((n2 * (n1 + SUBLANES), c), F32)],
        compiler_params=_cparams(("arbitrary",)),
        name="fourier_mix",
    )(z, ma, mc, md)


def _convpool_kernel(zp_ref, zc_ref, zn_ref, cw_ref, pw_ref, ps_ref, o_ref, *, seq, tt):
    i = pl.program_id(0)
    ext_rows = tt + 2 * HALO
    grow = i * tt - HALO + lax.broadcasted_iota(I32, (ext_rows, 1), 0)
    row_ok = (grow >= 0) & (grow < seq)

    def ext(c0, c1):
        e = jnp.concatenate([zp_ref[:, c0:c1], zc_ref[:, c0:c1], zn_ref[:, c0:c1]], axis=0).astype(F32)
        return jnp.where(row_ok, e, 0.0)

    def shifted(e, d):
        return e[HALO + d:HALO + d + tt]

    prod = ext(CONV_DIM, 2 * CONV_DIM) * ext(2 * CONV_DIM, 3 * CONV_DIM)
    cw = cw_ref[...]
    conv = shifted(prod, -1) * cw[0:1] + shifted(prod, 0) * cw[1:2] + shifted(prod, 1) * cw[2:3]
    o_ref[:, :CONV_DIM] = (zc_ref[:, :CONV_DIM].astype(F32) * conv).astype(o_ref.dtype)

    t = grow[HALO:HALO + tt]
    for g, win in enumerate(POOL_WINDOWS):
        lo = win // 2
        hi = win - 1 - lo
        c0 = 3 * CONV_DIM + g * POOL_GROUP_DIM
        e = ext(c0, c0 + POOL_GROUP_DIM)
        assert win & (win - 1) == 0 and win <= HALO
        run, step = e, 1
        while step < win:
            run = run + pltpu.roll(run, ext_rows - step, axis=0)
            step *= 2
        total = shifted(run, -lo)
        count = (jnp.minimum(t + hi, seq - 1) - jnp.maximum(t - lo, 0) + 1).astype(F32)
        pooled = total / count - shifted(e, 0)
        dg = _dot(pooled.astype(BF16), pw_ref[g].astype(BF16)) * ps_ref[:, g * POOL_GROUP_DIM:(g + 1) * POOL_GROUP_DIM]
        o_ref[:, CONV_DIM + g * POOL_GROUP_DIM:CONV_DIM + (g + 1) * POOL_GROUP_DIM] = dg.astype(o_ref.dtype)


def conv_pool(z, conv_w, pool_w, pool_scale, *, tt):
    s, w = z.shape
    nh = tt // HALO
    last_h = s // HALO - 1
    return pl.pallas_call(
        functools.partial(_convpool_kernel, seq=s, tt=tt),
        grid=(s // tt,),
        in_specs=[
            pl.BlockSpec((HALO, w), lambda i: (jnp.maximum(i * nh - 1, 0), 0)),
            pl.BlockSpec((tt, w), lambda i: (i, 0)),
            pl.BlockSpec((HALO, w), lambda i: (jnp.minimum((i + 1) * nh, last_h), 0)),
            pl.BlockSpec((3, CONV_DIM), lambda i: (0, 0)),
            pl.BlockSpec((POOL_GROUPS, POOL_GROUP_DIM, POOL_GROUP_DIM), lambda i: (0, 0, 0)),
            pl.BlockSpec((1, POOL_DIM), lambda i: (0, 0)),
        ],
        out_specs=pl.BlockSpec((tt, CONV_DIM + POOL_DIM), lambda i: (i, 0)),
        out_shape=jax.ShapeDtypeStruct((s, CONV_DIM + POOL_DIM), BF16),
        compiler_params=_cparams(("arbitrary",)),
        name="conv_pool",
    )(z, z, z, conv_w, pool_w, pool_scale.reshape(1, POOL_DIM))


def _excl_cumsum_lanes(m, upper):
    r, s = m.shape
    off = jnp.zeros((r, 1), F32)
    pieces = []
    for c in range(s // LANES):
        mc = m[:, c * LANES:(c + 1) * LANES]
        pieces.append(_dot(mc.astype(BF16), upper) + off)
        off = off + jnp.sum(mc, axis=1, keepdims=True)
    return jnp.concatenate(pieces, axis=1)


def _select_kernel(aff_ref, upper_ref, pos_ref, sel_ref, v_ref, *, cap):
    a = aff_ref[...]
    e, s = a.shape
    bits = pltpu.bitcast(a, I32)
    thr = jnp.zeros((e, 1), I32)
    for bit in range(30, -1, -1):
        cand = thr | (1 << bit)
        cnt = jnp.sum((bits >= cand).astype(F32), axis=1, keepdims=True)
        thr = jnp.where(cnt >= cap, cand, thr)
    gt = bits > thr
    eq = (bits == thr).astype(F32)
    need = cap - jnp.sum(gt.astype(F32), axis=1, keepdims=True)
    upper = upper_ref[...]
    eq_rank = _excl_cumsum_lanes(eq, upper)
    sel = jnp.where(gt | ((eq > 0.5) & (eq_rank < need)), 1.0, 0.0)
    pos = _excl_cumsum_lanes(sel, upper)
    pos_ref[...] = pos
    sel_ref[...] = sel
    tok = lax.broadcasted_iota(I32, (1, s), 1).astype(F32)
    tok_hi = jnp.floor(tok * (1.0 / LANES))
    tok_lo = tok - tok_hi * LANES
    g1 = a.astype(BF16).astype(F32)
    g2 = (a - g1).astype(BF16).astype(F32)
    g3 = a - g1 - g2
    zero = jnp.zeros((SUBLANES - 5, s), F32)
    for x in range(e):
        v_ref[x] = jnp.concatenate([tok_hi, tok_lo, g1[x:x + 1], g2[x:x + 1], g3[x:x + 1], zero], axis=0)


def select_tokens(aff, cap):
    e, s = aff.shape
    upper = jnp.asarray(np.triu(np.ones((LANES, LANES), np.float32), k=1), BF16)
    full2 = lambda shp: pl.BlockSpec(shp, lambda i: (0,) * len(shp))
    return pl.pallas_call(
        functools.partial(_select_kernel, cap=cap),
        grid=(1,),
        in_specs=[full2((e, s)), full2((LANES, LANES))],
        out_specs=[full2((e, s)), full2((e, s)), full2((e, SUBLANES, s))],
        out_shape=[
            jax.ShapeDtypeStruct((e, s), F32),
            jax.ShapeDtypeStruct((e, s), F32),
            jax.ShapeDtypeStruct((e, SUBLANES, s), F32),
        ],
        compiler_params=_cparams(("arbitrary",)),
        name="select_tokens",
    )(aff, upper)


def _lists_kernel(off_ref, cross_ref, pos_ref, sel_ref, v_ref, o_ref, acc, *, nchunk, ntile, group):
    e0 = pl.program_id(0) * group
    acc[...] = jnp.zeros(acc.shape, F32)
    slot = lax.broadcasted_iota(I32, (LANES, LANES), 0).astype(F32)

    def add_chunk(x, c, tile):
        st = pl.multiple_of(c * LANES, LANES)
        p = pos_ref[x, :, pl.ds(st, LANES)]
        chosen = sel_ref[x, :, pl.ds(st, LANES)] > 0.5
        v = v_ref[x, :, pl.ds(st, LANES)].astype(BF16)
        base = (tile * LANES).astype(F32)
        onehot = jnp.where(((p - base) == slot) & chosen, 1.0, 0.0).astype(BF16)
        acc[x, tile] += _dot_nt(v, onehot)

    def body(c, carry):
        for x in range(group):
            add_chunk(x, c, jnp.minimum(off_ref[(e0 + x) * nchunk + c] // LANES, ntile - 1))
        return carry

    lax.fori_loop(0, nchunk, body, 0, unroll=4)
    for j in range(1, ntile):
        for x in range(group):
            add_chunk(x, cross_ref[(e0 + x) * ntile + j], jnp.int32(j))
    for x in range(group):
        for j in range(ntile):
            o_ref[x, :, j * LANES:(j + 1) * LANES] = acc[x, j]


def build_lists(pos, sel, vals, cap):
    e, s = pos.shape
    nchunk = s // LANES
    ntile = cap // LANES
    off = pos[:, ::LANES].astype(I32)
    bounds = jnp.arange(ntile, dtype=I32) * LANES
    cross = jnp.maximum(jnp.sum((off[:, :, None] < bounds[None, None, :]).astype(I32), axis=1) - 1, 0)
    group = LIST_GROUP
    assert e % group == 0
    row = lambda nrow: pl.BlockSpec((group, nrow, s), lambda x, t, u: (x, 0, 0))
    lists = pl.pallas_call(
        functools.partial(_lists_kernel, nchunk=nchunk, ntile=ntile, group=group),
        grid_spec=pltpu.PrefetchScalarGridSpec(
            num_scalar_prefetch=2,
            grid=(e // group,),
            in_specs=[row(1), row(1), row(SUBLANES)],
            out_specs=pl.BlockSpec((group, SUBLANES, cap), lambda x, t, u: (x, 0, 0)),
            scratch_shapes=[pltpu.VMEM((group, ntile, SUBLANES, LANES), F32)],
        ),
        out_shape=jax.ShapeDtypeStruct((e, SUBLANES, cap), F32),
        compiler_params=_cparams(("arbitrary",)),
        name="build_lists",
    )(off.reshape(-1), cross.reshape(-1), pos.reshape(e, 1, s), sel.reshape(e, 1, s), vals)
    idx = (lists[:, 0] * LANES + lists[:, 1]).astype(I32)
    gate = (lists[:, 2] + lists[:, 3] + lists[:, 4])
    return idx, gate


def _ffn_kernel(idx_ref, h_hbm, gate_ref, wg_ref, wu_ref, wd_ref, y_ref, xbuf, xb, hid, sem_g, *, cap, nh, ne):
    e = pl.program_id(0)
    hs = pl.program_id(1)
    slot = e % 2
    rps = cap // nh

    def gather_row(expert, r, s):
        t = idx_ref[expert * cap + r]
        return pltpu.make_async_copy(h_hbm.at[pl.ds(pl.multiple_of(t * SUBLANES, SUBLANES), SUBLANES)],
                                     xbuf.at[s, pl.ds(pl.multiple_of(r * SUBLANES, SUBLANES), SUBLANES)],
                                     sem_g.at[s])

    def wait_gather(s):
        pltpu.make_async_copy(h_hbm.at[pl.ds(0, cap * SUBLANES)], xbuf.at[s], sem_g.at[s]).wait()

    @pl.when((e == 0) & (hs == 0))
    def _():
        def issue(r, carry):
            gather_row(0, r, 0).start()
            return carry

        lax.fori_loop(0, cap, issue, 0, unroll=8)

    @pl.when(hs == 0)
    def _():
        wait_gather(slot)
        lo, hi = _unpack_bf16_pairs(_load_row_tiles(xbuf.at[slot]))
        xb[...] = jnp.concatenate([lo, hi], axis=1)

    th = wg_ref.shape[1]
    w_gu = jnp.concatenate([wg_ref[...].astype(BF16), wu_ref[...].astype(BF16)], axis=1)
    gu = _dot(xb[...], w_gu)
    gp = gu[:, :th]
    hid[:, pl.ds(pl.multiple_of(hs * th, th), th)] = (gp * (1.0 / (1.0 + jnp.exp(-gp))) * gu[:, th:]).astype(BF16)
    nxt = lax.rem(e + 1, ne)
    for k in range(rps):
        gather_row(nxt, hs * rps + k, 1 - slot).start(priority=k % 2)

    @pl.when(hs == nh - 1)
    def _():
        w_d = wd_ref[...].astype(BF16)
        rb = cap // nh
        for b in range(nh):
            rows = slice(b * rb, (b + 1) * rb)
            y = _dot(hid[rows, :], w_d) * gate_ref[0, rows, :]
            _store_row_tiles(y_ref, _pack_bf16_pairs(y), b * rb)

    @pl.when((e == ne - 1) & (hs == nh - 1))
    def _():
        wait_gather(1 - slot)


def expert_ffn(h, idx, gate, w_gate, w_up, w_down, li, *, th):
    ne, cap = idx.shape
    d = w_gate.shape[2]
    hidden = w_gate.shape[3]
    nh = hidden // th
    assert d == 2 * SUBLANES * LANES
    return pl.pallas_call(
        functools.partial(_ffn_kernel, cap=cap, nh=nh, ne=ne),
        grid_spec=pltpu.PrefetchScalarGridSpec(
            num_scalar_prefetch=1,
            grid=(ne, nh),
            in_specs=[
                pl.BlockSpec(memory_space=pl.ANY),
                pl.BlockSpec((1, cap, 1), lambda x, j, a: (x, 0, 0)),
                pl.BlockSpec((None, None, d, th), lambda x, j, a: (li, x, 0, j)),
                pl.BlockSpec((None, None, d, th), lambda x, j, a: (li, x, 0, j)),
                pl.BlockSpec((None, None, hidden, d), lambda x, j, a: (li, x, 0, 0)),
            ],
            out_specs=pl.BlockSpec((cap * SUBLANES, LANES), lambda x, j, a: (x, 0),
                                   pipeline_mode=pl.Buffered(1)),
            scratch_shapes=[
                pltpu.VMEM((2, cap * SUBLANES, LANES), I32),
                pltpu.VMEM((cap, d), BF16),
                pltpu.VMEM((cap, hidden), BF16),
                pltpu.SemaphoreType.DMA((2,)),
            ],
        ),
        out_shape=jax.ShapeDtypeStruct((ne * cap * SUBLANES, LANES), I32),
        compiler_params=_cparams(("arbitrary", "arbitrary")),
        name="expert_ffn",
    )(idx.reshape(-1), h, gate.reshape(ne, cap, 1), w_gate, w_up, w_down)


def _combine_kernel(off_ref, x_ref, pos_ref, g_ref, y_hbm, *rest, ne, cap, nchunk, final):
    nout = 1 if final else 2
    o_refs = rest[:nout]
    buf, obuf, acc, member, sem, osem = rest[nout:]
    step = pl.program_id(0)
    slot = step % 2
    grp = COMBINE_GROUP
    tc = member.shape[1]
    windows = _CombineWindows(off_ref, ne, cap, nchunk)

    def fetch(e, st, g, s):
        return pltpu.make_async_copy(y_hbm.at[windows.tiles(windows.row(e, st * grp + g, 0), COMBINE_WINDOW)],
                                     buf.at[s, g, windows.tiles(e * COMBINE_WINDOW, COMBINE_WINDOW)], sem.at[s])

    @pl.when(step == 0)
    def _():
        for g in range(grp):
            for e in range(ne):
                fetch(e, 0, g, 0).start()

    @pl.when((step + 1) * grp < nchunk)
    def _():
        for g in range(grp):
            for e in range(ne):
                fetch(e, step + 1, g, 1 - slot).start(priority=e % 2)

    for g in range(grp):
        for e in range(ne):
            fetch(e, step, g, slot).wait()

    for first_pieces in (True, False):
        for g in range(grp):
            _combine_chunk(step * grp + g, g, slot, windows, pos_ref, y_hbm, buf, obuf, acc, member, osem, tc,
                           first_pieces)

    y = x_ref[...] + acc[...]
    if final:
        o_refs[0][...] = _rmsnorm(y, g_ref[...])
    else:
        o_refs[0][...] = y
        o_refs[1][...] = _rmsnorm(y, g_ref[...]).astype(BF16)


class _CombineWindows:
    def __init__(self, off_ref, ne, cap, nchunk):
        self.off_ref, self.ne, self.cap, self.nchunk = off_ref, ne, cap, nchunk

    def off(self, e, cc):
        return self.off_ref[e * (self.nchunk + 1) + cc]

    def row(self, e, cc, p):
        return jnp.minimum(e * self.cap + self.off(e, cc) + p * COMBINE_WINDOW, self.ne * self.cap - COMBINE_WINDOW)

    @staticmethod
    def tiles(row, count):
        return pl.ds(pl.multiple_of(row * SUBLANES, SUBLANES), count * SUBLANES)


def _combine_chunk(c, g, slot, windows, pos_ref, y_hbm, buf, obuf, acc, member, osem, tc, first_pieces):
    win = COMBINE_WINDOW
    ne, cap = windows.ne, windows.cap
    chunk_off, win_row, tiles = windows.off, windows.row, windows.tiles
    half = acc.shape[1] // 2
    rows = slice(g * tc, (g + 1) * tc)
    pos = pos_ref[rows, :]

    def window_col(e, p):
        local = pos[:, e:e + 1] - chunk_off(e, c).astype(F32)
        col = pos[:, e:e + 1] + (e * cap - win_row(e, c, p)).astype(F32)
        ok = (local >= p * win) & (local < (p + 1) * win)
        return jnp.where(ok, col, -1.0)

    if first_pieces:
        lane = lax.broadcasted_iota(I32, (1, LANES), 1)
        first = lane < win
        w = jnp.where(first, lane, lane - win).astype(F32)
        for q in range(ne // 2):
            col = jnp.where(first, window_col(2 * q, 0), window_col(2 * q + 1, 0))
            member[g, :, q * LANES:(q + 1) * LANES] = jnp.where(col == w, 1.0, 0.0).astype(BF16)
        a, b = _unpack_bf16_pairs(_load_row_tiles(buf.at[slot, g]))
        acc[rows, :half] = _dot(member[g], a)
        acc[rows, half:] = _dot(member[g], b)
        return

    for e in range(ne):
        npieces = (chunk_off(e, c + 1) - chunk_off(e, c) + win - 1) // win

        def extra(p, carry, e=e):
            cp = pltpu.make_async_copy(y_hbm.at[tiles(win_row(e, c, p), win)], obuf, osem)
            cp.start()
            cp.wait()
            wi = lax.broadcasted_iota(I32, (1, win), 1).astype(F32)
            m = jnp.where(window_col(e, p) == wi, 1.0, 0.0).astype(BF16)
            oa, ob = _unpack_bf16_pairs(_load_row_tiles(obuf))
            acc[rows, :half] += _dot(m, oa)
            acc[rows, half:] += _dot(m, ob)
            return carry

        lax.fori_loop(1, npieces, extra, 0)


def combine(x, y_rows, pos, sel, cap, g, *, tc, final):
    s, d = x.shape
    ne = pos.shape[0]
    nchunk = s // tc
    grp = COMBINE_GROUP
    assert 2 * COMBINE_WINDOW == LANES and ne % 2 == 0 and ne * cap >= COMBINE_WINDOW and nchunk % grp == 0
    off = jnp.concatenate([pos[:, ::tc].astype(I32), jnp.full((ne, 1), cap, I32)], axis=1).reshape(-1)
    row_block = pl.BlockSpec((grp * tc, d), lambda c, t: (c, 0))
    if final:
        out_specs, out_shape = row_block, jax.ShapeDtypeStruct((s, d), F32)
    else:
        out_specs = [row_block, row_block]
        out_shape = [jax.ShapeDtypeStruct((s, d), F32), jax.ShapeDtypeStruct((s, d), BF16)]
    return pl.pallas_call(
        functools.partial(_combine_kernel, ne=ne, cap=cap, nchunk=nchunk, final=final),
        grid_spec=pltpu.PrefetchScalarGridSpec(
            num_scalar_prefetch=1,
            grid=(nchunk // grp,),
            in_specs=[
                row_block,
                pl.BlockSpec((grp * tc, ne), lambda c, t: (c, 0)),
                pl.BlockSpec((1, d), lambda c, t: (0, 0)),
                pl.BlockSpec(memory_space=pl.ANY),
            ],
            out_specs=out_specs,
            scratch_shapes=[
                pltpu.VMEM((2, grp, ne * COMBINE_WINDOW * SUBLANES, LANES), I32),
                pltpu.VMEM((COMBINE_WINDOW * SUBLANES, LANES), I32),
                pltpu.VMEM((grp * tc, d), F32),
                pltpu.VMEM((grp, tc, ne * COMBINE_WINDOW), BF16),
                pltpu.SemaphoreType.DMA((2,)),
                pltpu.SemaphoreType.DMA(()),
            ],
        ),
        out_shape=out_shape,
        compiler_params=_cparams(("arbitrary",)),
        name="combine",
    )(off, x, jnp.where(sel > 0.5, pos, -1.0).T, g.reshape(1, d), y_rows)


def ec_moe_block(x1, h, aff, w_gate, w_up, w_down, li, g_next, *, final, th, tc):
    s = x1.shape[0]
    cap = CAPACITY_FACTOR * s // N_EXPERTS
    pos, sel, vals = select_tokens(aff, cap)
    idx, gate = build_lists(pos, sel, vals, cap)
    y_rows = expert_ffn(h, idx, gate, w_gate, w_up, w_down, li, th=th)
    return combine(x1, y_rows, pos, sel, cap, g_next, tc=tc, final=final)


def _tiles(s):
    return dict(tm=min(1024, s), tn=1024, to=min(512, s), tt=min(512, s), th=256, tc=min(256, s))


def kernel(x, rel_bias, norm_mix_g, norm_ffn_g, final_norm_g, ev_w_in, ev_sink, ev_w_out, od_w_in, od_conv_w,
           od_pool_w, od_pool_scale, od_w_out, moe_w_router, moe_w_gate, moe_w_up, moe_w_down):
    b, s, d = x.shape
    assert b == 1
    t = _tiles(s)
    depth = norm_mix_g.shape[0]
    bias = window_bias(rel_bias)
    xs = x.reshape(s, d)
    hn = prenorm(xs, norm_mix_g[0], tm=t["tm"])
    for layer in range(depth):
        i = layer // 2
        if layer % 2 == 0:
            z = in_proj(hn, ev_w_in, i, tm=t["tm"], tn=t["tn"])
            parts = [windowed_attention(z, ev_sink[i], bias), fourier_mix(z, Q_DIM + 2 * KV_DIM)]
            w_out = ev_w_out
        else:
            z = in_proj(hn, od_w_in, i, tm=t["tm"], tn=t["tn"])
            parts = [conv_pool(z, od_conv_w[i], od_pool_w[i], od_pool_scale[i], tt=t["tt"])]
            w_out = od_w_out
        x1 = out_proj(xs, parts, w_out, i, tm=t["tm"], tn=t["tn"])
        h, aff = router(x1, norm_ffn_g[layer], moe_w_router[layer], tm=t["to"])
        final = layer == depth - 1
        g_next = final_norm_g if final else norm_mix_g[layer + 1]
        res = ec_moe_block(x1, h, aff.T, moe_w_gate, moe_w_up, moe_w_down, layer, g_next,
                           final=final, th=t["th"], tc=t["tc"])
        if final:
            xs = res
        else:
            xs, hn = res
    return xs.reshape(b, s, d)
```
